```python
import math
import jax, jax.numpy as jnp
from jax import lax
import numpy as np

D_MODEL = 2048
BATCH = 1
SEQ = 16384
DEPTH = 4

GRID_W = 64
CTX_LEN = 256
F32 = jnp.float32

N_MOD = 9
D_FF = 5632
SSD_HEADS = 12
SSD_HEAD_DIM = 64
SSD_INNER = SSD_HEADS * SSD_HEAD_DIM
SSD_GROUPS = 2
SSD_STATE = 128
SSD_BC = SSD_GROUPS * SSD_STATE
SSD_XBC = SSD_INNER + 2 * SSD_BC
SSD_CONV = 5
SSD_CHUNK = 128
S5_WIDTH = 512
S5_GROUP = 16
S5_GROUPS = S5_WIDTH // S5_GROUP
S5_STATE = 64
HEAD_DIM = 64
ATT_Q_HEADS = 8
ATT_KV_HEADS = 2
GQA_GROUP = ATT_Q_HEADS // ATT_KV_HEADS
ATT_WIDTH = ATT_Q_HEADS * HEAD_DIM
ATT_KV_WIDTH = ATT_KV_HEADS * HEAD_DIM
ATT_SCALE = HEAD_DIM ** -0.5
ATT_BLOCK = 128
SWA_WINDOW = 128
ROPE_THETA = 10000.0
NEG_INF = -1e30
N_BRANCH = 4
IN_SIZES = (SSD_INNER, SSD_XBC, 2 * SSD_HEADS, S5_WIDTH,
            ATT_WIDTH, ATT_KV_WIDTH, ATT_KV_WIDTH,
            ATT_WIDTH, ATT_KV_WIDTH, ATT_KV_WIDTH,
            N_BRANCH * D_MODEL)
D_IN = sum(IN_SIZES)

kernel_name = 'hybrid_ssd_s5_swa_gqa_prefix_trunk'


def rms_norm(x, g, eps=1e-6):
    xf = x.astype(F32)
    y = xf * lax.rsqrt(jnp.mean(xf * xf, axis=-1, keepdims=True) + eps)
    return (y * g.astype(F32)).astype(x.dtype)


def modulate(x, g, shift, scale):
    return rms_norm(x, g) * (1 + scale) + shift


def swiglu(h, w_in, w_out):
    a, b = jnp.split(h @ w_in, 2, axis=-1)
    return (jax.nn.silu(a) * b) @ w_out


def split_in(p):
    return jnp.split(p, np.cumsum(IN_SIZES)[:-1].tolist(), axis=-1)


def axial_rope(length):
    rows = length // GRID_W
    row = jnp.repeat(jnp.arange(rows, dtype=F32), GRID_W)
    col = jnp.tile(jnp.arange(GRID_W, dtype=F32), rows)
    n_freq = HEAD_DIM // 4
    inv = ROPE_THETA ** (-jnp.arange(n_freq, dtype=F32) / n_freq)
    ang = jnp.concatenate([row[:, None] * inv, col[:, None] * inv], axis=-1)
    return jnp.cos(ang), jnp.sin(ang)


def apply_rope(x, cos, sin):
    half = x.shape[-1] // 2
    x1, x2 = x[..., :half].astype(F32), x[..., half:].astype(F32)
    c, s = cos[:, None, :], sin[:, None, :]
    return jnp.concatenate([x1 * c - x2 * s, x1 * s + x2 * c], axis=-1).astype(x.dtype)


def centred_dwconv(x, w, b):
    k = w.shape[0]
    y = lax.conv_general_dilated(x, w[:, None, :], window_strides=(1,),
                                 padding=[((k - 1) // 2, (k - 1) // 2)],
                                 dimension_numbers=('NWC', 'WIO', 'NWC'),
                                 feature_group_count=x.shape[-1])
    return y + b


def segsum(a):
    t = a.shape[-1]
    ar = jnp.broadcast_to(a[..., :, None], a.shape + (t,))
    ar = jnp.where(jnp.tril(jnp.ones((t, t), bool), -1), ar, 0.0)
    ss = jnp.cumsum(ar, axis=-2)
    return jnp.where(jnp.tril(jnp.ones((t, t), bool)), ss, -jnp.inf)


def ssd_chunked(xdt, adt, bm, cm, h0):
    bsz, l, h, p = xdt.shape
    g, n = bm.shape[2], bm.shape[3]
    hg = h // g
    t = SSD_CHUNK
    nc = l // t
    xc = xdt.reshape(bsz, nc, t, g, hg, p)
    bc = bm.reshape(bsz, nc, t, g, n)
    cc = cm.reshape(bsz, nc, t, g, n)
    ac = jnp.moveaxis(adt.reshape(bsz, nc, t, g, hg), 2, -1)
    a_cum = jnp.cumsum(ac, axis=-1)
    lmat = jnp.exp(segsum(ac))
    cb = jnp.einsum('bclgn,bcsgn->bcgls', cc, bc)
    y_diag = jnp.einsum('bcgls,bcghls,bcsghp->bclghp', cb, lmat, xc)
    decay_states = jnp.exp(a_cum[..., -1:] - a_cum)
    states = jnp.einsum('bclgn,bcghl,bclghp->bcghpn', bc, decay_states, xc)
    states = jnp.concatenate([h0.reshape(bsz, 1, g, hg, p, n), states], axis=1)
    chunk_a = jnp.pad(jnp.moveaxis(a_cum[..., -1], 1, -1), ((0, 0), (0, 0), (0, 0), (1, 0)))
    decay_chunk = jnp.exp(segsum(chunk_a))
    new_states = jnp.einsum('bghzc,bcghpn->bzghpn', decay_chunk, states)
    prev_states, final = new_states[:, :-1], new_states[:, -1]
    y_off = jnp.einsum('bclgn,bcghpn,bcghl->bclghp', cc, prev_states, jnp.exp(a_cum))
    y = (y_diag + y_off).reshape(bsz, l, h, p)
    return y, final.reshape(bsz, h, p, n)


def ssd_mixer(z_ctx, xbc_ctx, dt_ctx, z_lat, xbc_lat, dt_lat,
              conv_w, conv_b, a_log, dt_bias, d_skip, norm_g, need_ctx):
    dtype = z_lat.dtype
    a = -jnp.exp(a_log.astype(F32))

    def prep(xbc, dt):
        xbc = jax.nn.silu(centred_dwconv(xbc, conv_w, conv_b).astype(F32))
        b, l = xbc.shape[:2]
        xs = xbc[..., :SSD_INNER].reshape(b, l, SSD_HEADS, SSD_HEAD_DIM)
        bm = xbc[..., SSD_INNER:SSD_INNER + SSD_BC].reshape(b, l, SSD_GROUPS, SSD_STATE)
        cm = xbc[..., SSD_INNER + SSD_BC:].reshape(b, l, SSD_GROUPS, SSD_STATE)
        dt = jax.nn.softplus(dt.astype(F32).reshape(b, l, 2, SSD_HEADS) + dt_bias.astype(F32))
        return xs, bm, cm, dt

    def scan_dir(xs, bm, cm, dt, d, h0, reverse):
        if reverse:
            xs, bm, cm, dt = (jnp.flip(v, axis=1) for v in (xs, bm, cm, dt))
        y, h = ssd_chunked(xs * dt[:, :, d, :, None], a[d] * dt[:, :, d], bm, cm, h0)
        return (jnp.flip(y, axis=1) if reverse else y), h

    def finish(y, xs, z):
        b, l = z.shape[:2]
        y = (y + d_skip.astype(F32)[:, None] * xs).reshape(b, l, SSD_INNER) * jax.nn.silu(z.astype(F32))
        return rms_norm(y, norm_g).astype(dtype)

    ctx_in = prep(xbc_ctx, dt_ctx)
    lat_in = prep(xbc_lat, dt_lat)
    h0 = jnp.zeros((z_lat.shape[0], SSD_HEADS, SSD_HEAD_DIM, SSD_STATE), F32)
    yc_f, hc_f = scan_dir(*ctx_in, 0, h0, False)
    yc_b, hc_b = scan_dir(*ctx_in, 1, h0, True)
    yl_f, _ = scan_dir(*lat_in, 0, hc_f, False)
    yl_b, _ = scan_dir(*lat_in, 1, hc_b, True)
    y_lat = finish(yl_f + yl_b, lat_in[0], z_lat)
    y_ctx = finish(yc_f + yc_b, ctx_in[0], z_ctx) if need_ctx else None
    return y_ctx, y_lat


def s5_discretise(a_re, a_im, log_dt, b_re, b_im):
    a_re, a_im, b_re, b_im = (v.astype(F32) for v in (a_re, a_im, b_re, b_im))
    dt = jnp.exp(log_dt.astype(F32))[:, None]
    mag = jnp.exp(a_re * dt)
    lb_re, lb_im = mag * jnp.cos(a_im * dt), mag * jnp.sin(a_im * dt)
    den = a_re * a_re + a_im * a_im
    nr, ni = lb_re - 1.0, lb_im
    f_re = (nr * a_re + ni * a_im) / den
    f_im = (ni * a_re - nr * a_im) / den
    bb_re = f_re[..., None] * b_re - f_im[..., None] * b_im
    bb_im = f_re[..., None] * b_im + f_im[..., None] * b_re
    return lb_re, lb_im, bb_re, bb_im


def s5_combine(e1, e2):
    a1r, a1i, b1r, b1i = e1
    a2r, a2i, b2r, b2i = e2
    return (a2r * a1r - a2i * a1i, a2r * a1i + a2i * a1r,
            a2r * b1r - a2i * b1i + b2r, a2r * b1i + a2i * b1r + b2i)


def s5_scan(u, lb_re, lb_im, bb_re, bb_im, s0_re, s0_im):
    bu_re = jnp.einsum('blgh,gph->blgp', u, bb_re)
    bu_im = jnp.einsum('blgh,gph->blgp', u, bb_im)
    bu_re = bu_re.at[:, 0].add(lb_re * s0_re - lb_im * s0_im)
    bu_im = bu_im.at[:, 0].add(lb_re * s0_im + lb_im * s0_re)
    a_re = jnp.broadcast_to(lb_re, bu_re.shape)
    a_im = jnp.broadcast_to(lb_im, bu_im.shape)
    _, _, s_re, s_im = lax.associative_scan(s5_combine, (a_re, a_im, bu_re, bu_im), axis=1)
    return s_re, s_im


def s5_mixer(u_ctx, u_lat, a_re, a_im, log_dt, b_re, b_im, c_re, c_im, d_skip,
             glu_w, glu_b, need_ctx):
    dtype = u_lat.dtype
    disc = [s5_discretise(a_re[d], a_im[d], log_dt[d], b_re, b_im) for d in range(2)]
    c_re, c_im, d_skip = c_re.astype(F32), c_im.astype(F32), d_skip.astype(F32)

    def groups(u):
        b, l = u.shape[:2]
        return u.astype(F32).reshape(b, l, S5_GROUPS, S5_GROUP)

    def flip(v):
        return jnp.flip(v, axis=1)

    uc, ul = groups(u_ctx), groups(u_lat)
    zero = jnp.zeros((uc.shape[0], S5_GROUPS, S5_STATE), F32)
    fc = s5_scan(uc, *disc[0], zero, zero)
    bc = s5_scan(flip(uc), *disc[1], zero, zero)
    fl = s5_scan(ul, *disc[0], fc[0][:, -1], fc[1][:, -1])
    bl = s5_scan(flip(ul), *disc[1], bc[0][:, -1], bc[1][:, -1])

    def readout(u, sf, sb):
        s_re = sf[0] + flip(sb[0])
        s_im = sf[1] + flip(sb[1])
        y = (jnp.einsum('ghp,blgp->blgh', c_re, s_re) - jnp.einsum('ghp,blgp->blgh', c_im, s_im)
             + d_skip * u)
        b, l = u.shape[:2]
        y = jax.nn.gelu(y.reshape(b, l, S5_WIDTH))
        ya, yg = jnp.split(y @ glu_w.astype(F32) + glu_b.astype(F32), 2, axis=-1)
        return (ya * jax.nn.sigmoid(yg)).astype(dtype)

    y_lat = readout(ul, fl, bl)
    y_ctx = readout(uc, fc, bc) if need_ctx else None
    return y_ctx, y_lat


def attn_heads(q, k, v, qk_g, rope):
    b, l = q.shape[:2]
    q = rms_norm(q.reshape(b, l, ATT_Q_HEADS, HEAD_DIM), qk_g[0])
    k = rms_norm(k.reshape(b, l, ATT_KV_HEADS, HEAD_DIM), qk_g[1])
    if rope is not None:
        q = apply_rope(q, *rope)
        k = apply_rope(k, *rope)
    return (q.reshape(b, l, ATT_KV_HEADS, GQA_GROUP, HEAD_DIM), k,
            v.reshape(b, l, ATT_KV_HEADS, HEAD_DIM))


def softmax_with_sink(logits, sink):
    s = jnp.broadcast_to(sink.astype(F32), logits.shape[:-1] + (1,))
    p = jax.nn.softmax(jnp.concatenate([s, logits], axis=-1), axis=-1)
    return p[..., 1:]


def gqa_dense(q, k, v, sink):
    b, nq = q.shape[:2]
    logits = jnp.einsum('bqkgd,bskd->bkgqs', q, k).astype(F32) * ATT_SCALE
    if sink is None:
        p = jax.nn.softmax(logits, axis=-1)
    else:
        p = softmax_with_sink(logits, sink.reshape(ATT_KV_HEADS, GQA_GROUP)[None, :, :, None, None])
    out = jnp.einsum('bkgqs,bskd->bqkgd', p.astype(v.dtype), v)
    return out.reshape(b, nq, ATT_WIDTH)


def swa_latent(q, k, v, k_ctx, v_ctx, sink):
    b, l = q.shape[:2]
    w = ATT_BLOCK
    nb = l // w
    lc = k_ctx.shape[1]
    qb = q.reshape(b, nb, w, ATT_KV_HEADS, GQA_GROUP, HEAD_DIM)

    def band(t, t_ctx):
        tp = jnp.pad(t, ((0, 0), (w, w), (0, 0), (0, 0))).reshape(b, nb + 2, w, ATT_KV_HEADS, HEAD_DIM)
        tc = jnp.broadcast_to(t_ctx[:, None], (b, nb, lc, ATT_KV_HEADS, HEAD_DIM))
        return jnp.concatenate([tp[:, :-2], tp[:, 1:-1], tp[:, 2:], tc], axis=2)

    kw, vw = band(k, k_ctx), band(v, v_ctx)
    logits = jnp.einsum('bnqkgd,bnskd->bnkgqs', qb, kw).astype(F32) * ATT_SCALE
    qpos = jnp.arange(nb)[:, None] * w + jnp.arange(w)[None, :]
    kpos = jnp.arange(nb)[:, None] * w - w + jnp.arange(3 * w)[None, :]
    valid = ((jnp.abs(kpos[:, None, :] - qpos[:, :, None]) <= SWA_WINDOW)
             & (kpos[:, None, :] >= 0) & (kpos[:, None, :] < l))
    valid = jnp.concatenate([valid, jnp.ones((nb, w, lc), bool)], axis=-1)
    logits = jnp.where(valid[None, :, None, None], logits, NEG_INF)
    p = softmax_with_sink(logits, sink.reshape(ATT_KV_HEADS, GQA_GROUP)[None, None, :, :, None, None])
    out = jnp.einsum('bnkgqs,bnskd->bnqkgd', p.astype(vw.dtype), vw)
    return out.reshape(b, l, ATT_WIDTH)


def global_latent(q, k_all, v_all):
    b, l = q.shape[:2]
    nb = l // ATT_BLOCK
    qb = jnp.moveaxis(q.reshape(b, nb, ATT_BLOCK, ATT_KV_HEADS, GQA_GROUP, HEAD_DIM), 1, 0)
    out = lax.map(lambda qi: gqa_dense(qi, k_all, v_all, None), qb)
    return jnp.moveaxis(out, 0, 1).reshape(b, l, ATT_WIDTH)


def merge_branches(gate_logits, ys, w_brs, w_out):
    gates = jnp.split(jax.nn.sigmoid(gate_logits.astype(F32)), N_BRANCH, axis=-1)
    acc = gates[0] * (ys[0] @ w_brs[0])
    for gb, yb, wb in zip(gates[1:], ys[1:], w_brs[1:]):
        acc = acc + gb * (yb @ wb)
    return acc.astype(gate_logits.dtype) @ w_out


def token_mixing(h_ctx, h_lat, rope, lp, need_ctx):
    pc = split_in(h_ctx @ lp['w_in'])
    pl = split_in(h_lat @ lp['w_in'])
    ya_ctx, ya_lat = ssd_mixer(pc[0], pc[1], pc[2], pl[0], pl[1], pl[2],
                               lp['ssd_conv_w'], lp['ssd_conv_b'], lp['ssd_a_log'],
                               lp['ssd_dt_bias'], lp['ssd_d'], lp['ssd_norm_g'], need_ctx)
    yb_ctx, yb_lat = s5_mixer(pc[3], pl[3], lp['s5_a_re'], lp['s5_a_im'], lp['s5_log_dt'],
                              lp['s5_b_re'], lp['s5_b_im'], lp['s5_c_re'], lp['s5_c_im'],
                              lp['s5_d'], lp['s5_glu_w'], lp['s5_glu_b'], need_ctx)
    qc_c, kc_c, vc_c = attn_heads(pc[4], pc[5], pc[6], lp['swa_qk_g'], None)
    qc_l, kc_l, vc_l = attn_heads(pl[4], pl[5], pl[6], lp['swa_qk_g'], rope)
    yc_lat = swa_latent(qc_l, kc_l, vc_l, kc_c, vc_c, lp['swa_sink'])
    qd_c, kd_c, vd_c = attn_heads(pc[7], pc[8], pc[9], lp['glb_qk_g'], None)
    qd_l, kd_l, vd_l = attn_heads(pl[7], pl[8], pl[9], lp['glb_qk_g'], rope)
    yd_lat = global_latent(qd_l, jnp.concatenate([kd_l, kd_c], axis=1),
                           jnp.concatenate([vd_l, vd_c], axis=1))
    w_brs = (lp['w_br_ssd'], lp['w_br_s5'], lp['w_br_swa'], lp['w_br_glb'])
    y_lat = merge_branches(pl[10], (ya_lat, yb_lat, yc_lat, yd_lat), w_brs, lp['w_out'])
    if not need_ctx:
        return None, y_lat
    yc_ctx = gqa_dense(qc_c, kc_c, vc_c, lp['swa_sink'])
    yd_ctx = gqa_dense(qd_c, kd_c, vd_c, None)
    y_ctx = merge_branches(pc[10], (ya_ctx, yb_ctx, yc_ctx, yd_ctx), w_brs, lp['w_out'])
    return y_ctx, y_lat


def setup_inputs(seed: int = 0) -> dict:
    key = jax.random.key(seed)
    ks = iter(jax.random.split(key, 40))

    def nrm(shape, std):
        return std * jax.random.normal(next(ks), shape, F32)

    def uni(shape, lo, hi):
        return jax.random.uniform(next(ks), shape, F32, lo, hi)

    L = DEPTH
    dt0 = jnp.exp(uni((L, 2, SSD_HEADS), math.log(1e-3), math.log(1e-1)))
    return {
        'x': nrm((BATCH, SEQ, D_MODEL), 1.0),
        'c': nrm((BATCH, D_MODEL), 1.0),
        'ctx': nrm((BATCH, CTX_LEN, D_MODEL), 1.0),
        'c_ctx': nrm((D_MODEL,), 1.0),
        'w_mod': nrm((L, D_MODEL, N_MOD * D_MODEL), 0.5 * D_MODEL ** -0.5),
        'b_mod': nrm((L, N_MOD * D_MODEL), 0.01),
        'norm_g': 1.0 + nrm((L, 3, D_MODEL), 0.05),
        'ffn_in': nrm((L, 2, D_MODEL, 2 * D_FF), D_MODEL ** -0.5),
        'ffn_out': nrm((L, 2, D_FF, D_MODEL), D_FF ** -0.5),
        'w_in': nrm((L, D_MODEL, D_IN), D_MODEL ** -0.5),
        'ssd_conv_w': nrm((L, SSD_CONV, SSD_XBC), SSD_CONV ** -0.5),
        'ssd_conv_b': nrm((L, SSD_XBC), 0.01),
        'ssd_a_log': jnp.log(uni((L, 2, SSD_HEADS), 1.0, 16.0)),
        'ssd_dt_bias': dt0 + jnp.log(-jnp.expm1(-dt0)),
        'ssd_d': 1.0 + nrm((L, SSD_HEADS), 0.1),
        'ssd_norm_g': 1.0 + nrm((L, SSD_INNER), 0.05),
        's5_a_re': -0.5 + nrm((L, 2, S5_GROUPS, S5_STATE), 0.01),
        's5_a_im': jnp.pi * jnp.arange(S5_STATE, dtype=F32) + nrm((L, 2, S5_GROUPS, S5_STATE), 0.01),
        's5_log_dt': uni((L, 2, S5_GROUPS), math.log(1e-3), math.log(1e-1)),
        's5_b_re': nrm((L, S5_GROUPS, S5_STATE, S5_GROUP), (2 * S5_GROUP) ** -0.5),
        's5_b_im': nrm((L, S5_GROUPS, S5_STATE, S5_GROUP), (2 * S5_GROUP) ** -0.5),
        's5_c_re': nrm((L, S5_GROUPS, S5_GROUP, S5_STATE), S5_STATE ** -0.5),
        's5_c_im': nrm((L, S5_GROUPS, S5_GROUP, S5_STATE), S5_STATE ** -0.5),
        's5_d': nrm((L, S5_GROUPS, S5_GROUP), 1.0),
        's5_glu_w': nrm((L, S5_WIDTH, 2 * S5_WIDTH), S5_WIDTH ** -0.5),
        's5_glu_b': nrm((L, 2 * S5_WIDTH), 0.01),
        'swa_qk_g': 1.0 + nrm((L, 2, HEAD_DIM), 0.05),
        'swa_sink': nrm((L, ATT_Q_HEADS), 0.5),
        'glb_qk_g': 1.0 + nrm((L, 2, HEAD_DIM), 0.05),
        'w_br_ssd': nrm((L, SSD_INNER, D_MODEL), SSD_INNER ** -0.5),
        'w_br_s5': nrm((L, S5_WIDTH, D_MODEL), S5_WIDTH ** -0.5),
        'w_br_swa': nrm((L, ATT_WIDTH, D_MODEL), ATT_WIDTH ** -0.5),
        'w_br_glb': nrm((L, ATT_WIDTH, D_MODEL), ATT_WIDTH ** -0.5),
        'w_out': nrm((L, D_MODEL, D_MODEL), D_MODEL ** -0.5),
    }


def reference(x, c, ctx, c_ctx, w_mod, b_mod, norm_g, ffn_in, ffn_out, w_in,
              ssd_conv_w, ssd_conv_b, ssd_a_log, ssd_dt_bias, ssd_d, ssd_norm_g,
              s5_a_re, s5_a_im, s5_log_dt, s5_b_re, s5_b_im, s5_c_re, s5_c_im, s5_d,
              s5_glu_w, s5_glu_b, swa_qk_g, swa_sink, glb_qk_g,
              w_br_ssd, w_br_s5, w_br_swa, w_br_glb, w_out):
    rope = axial_rope(x.shape[1])
    sc = jax.nn.silu(c)
    scc = jax.nn.silu(c_ctx)
    xl, xc = x, ctx
    for i in range(DEPTH):
        need_ctx = i < DEPTH - 1
        ml = jnp.split((sc @ w_mod[i] + b_mod[i])[:, None, :], N_MOD, axis=-1)
        mc = jnp.split((scc @ w_mod[i] + b_mod[i])[None, None, :], N_MOD, axis=-1)
        xl = xl + 0.5 * ml[2] * swiglu(modulate(xl, norm_g[i, 0], ml[0], ml[1]), ffn_in[i, 0], ffn_out[i, 0])
        xc = xc + 0.5 * mc[2] * swiglu(modulate(xc, norm_g[i, 0], mc[0], mc[1]), ffn_in[i, 0], ffn_out[i, 0])
        lp = dict(w_in=w_in[i], ssd_conv_w=ssd_conv_w[i], ssd_conv_b=ssd_conv_b[i],
                  ssd_a_log=ssd_a_log[i], ssd_dt_bias=ssd_dt_bias[i], ssd_d=ssd_d[i],
                  ssd_norm_g=ssd_norm_g[i], s5_a_re=s5_a_re[i], s5_a_im=s5_a_im[i],
                  s5_log_dt=s5_log_dt[i], s5_b_re=s5_b_re[i], s5_b_im=s5_b_im[i],
                  s5_c_re=s5_c_re[i], s5_c_im=s5_c_im[i], s5_d=s5_d[i],
                  s5_glu_w=s5_glu_w[i], s5_glu_b=s5_glu_b[i], swa_qk_g=swa_qk_g[i],
                  swa_sink=swa_sink[i], glb_qk_g=glb_qk_g[i], w_br_ssd=w_br_ssd[i],
                  w_br_s5=w_br_s5[i], w_br_swa=w_br_swa[i], w_br_glb=w_br_glb[i], w_out=w_out[i])
        y_ctx, y_lat = token_mixing(modulate(xc, norm_g[i, 1], mc[3], mc[4]),
                                    modulate(xl, norm_g[i, 1], ml[3], ml[4]), rope, lp, need_ctx)
        xl = xl + ml[5] * y_lat
        xl = xl + 0.5 * ml[8] * swiglu(modulate(xl, norm_g[i, 2], ml[6], ml[7]), ffn_in[i, 1], ffn_out[i, 1])
        if need_ctx:
            xc = xc + mc[5] * y_ctx
            xc = xc + 0.5 * mc[8] * swiglu(modulate(xc, norm_g[i, 2], mc[6], mc[7]), ffn_in[i, 1], ffn_out[i, 1])
    return xl
```

```python
import functools
import math

import numpy as np
import jax
import jax.numpy as jnp
from jax import lax
from jax.experimental import pallas as pl
from jax.experimental.pallas import tpu as pltpu

F32 = jnp.float32
BF16 = jnp.bfloat16

D_MODEL = 2048
GRID_W = 64
CTX_LEN = 256
N_MOD = 9
D_FF = 5632
SSD_HEADS = 12
SSD_HEAD_DIM = 64
SSD_INNER = SSD_HEADS * SSD_HEAD_DIM
SSD_GROUPS = 2
SSD_STATE = 128
SSD_BC = SSD_GROUPS * SSD_STATE
SSD_XBC = SSD_INNER + 2 * SSD_BC
SSD_CONV = 5
S5_WIDTH = 512
S5_GROUP = 16
S5_GROUPS = S5_WIDTH // S5_GROUP
S5_STATE = 64
HEAD_DIM = 64
ATT_Q_HEADS = 8
ATT_KV_HEADS = 2
GQA_GROUP = ATT_Q_HEADS // ATT_KV_HEADS
ATT_WIDTH = ATT_Q_HEADS * HEAD_DIM
ATT_KV_WIDTH = ATT_KV_HEADS * HEAD_DIM
ATT_SCALE = HEAD_DIM ** -0.5
SWA_WINDOW = 128
ROPE_THETA = 10000.0
NEG_INF = -1e30
N_BRANCH = 4
EPS = 1e-6

LANES = 128
VMEM_LIMIT = 56 * 1024 * 1024

TM = 512
SSD_T = 128
S5_T = 16
S5_GB = 4
PROJ_TM = 256
GLB_TQ = 256
GLB_TK = 1280
DT_PAD = LANES


def _params(sem, vmem=VMEM_LIMIT):
    return pltpu.CompilerParams(dimension_semantics=sem, vmem_limit_bytes=vmem)


def _silu(x):
    return x * jax.nn.sigmoid(x)


def _norm_mod(x, g, shift, scale):
    ms = jnp.mean(x * x, axis=-1, keepdims=True)
    return (x * lax.rsqrt(ms + EPS) * g) * (1.0 + scale) + shift


def _dot(a, b):
    return jnp.dot(a, b, preferred_element_type=F32)


def _dot_nt(a, b):
    return lax.dot_general(a, b, (((1,), (1,)), ((), ())), preferred_element_type=F32)


def _split_rhs_dot(m, x, parts):
    acc = None
    r = x
    for _ in range(parts):
        hi = r.astype(BF16)
        t = _dot(m, hi)
        acc = t if acc is None else acc + t
        r = r - hi.astype(F32)
    return acc


def _split_lhs_dot(x, m, parts):
    acc = None
    r = x
    for _ in range(parts):
        hi = r.astype(BF16)
        t = _dot(hi, m)
        acc = t if acc is None else acc + t
        r = r - hi.astype(F32)
    return acc


def _mod_kernel(s_ref, w_ref, b_ref, o_ref):
    s = _silu(s_ref[...])
    o_ref[...] = _dot(s.astype(BF16), w_ref[...].astype(BF16)) + b_ref[...]


def _mods(c, c_ctx, w_mod, b_mod):
    depth = w_mod.shape[0]
    s = jnp.zeros((8, D_MODEL), F32).at[0].set(c_ctx).at[1].set(c[0])
    tn = 1024
    out = pl.pallas_call(
        _mod_kernel,
        grid=(depth, N_MOD * D_MODEL // tn),
        in_specs=[pl.BlockSpec((8, D_MODEL), lambda l, j: (0, 0)),
                  pl.BlockSpec((None, D_MODEL, tn), lambda l, j: (l, 0, j)),
                  pl.BlockSpec((None, 1, tn), lambda l, j: (l, 0, j))],
        out_specs=pl.BlockSpec((None, 8, tn), lambda l, j: (l, 0, j)),
        out_shape=jax.ShapeDtypeStruct((depth, 8, N_MOD * D_MODEL), F32),
        compiler_params=_params(("parallel", "parallel")),
        name="mods",
    )(s, w_mod, b_mod.reshape(depth, 1, N_MOD * D_MODEL))
    return out[:, :2].reshape(depth, 2, N_MOD, D_MODEL)


def _mod_spec():
    return pl.BlockSpec((None, N_MOD, D_MODEL), lambda i, *_: (jnp.minimum(i, 1), 0, 0))


def _ffn_kernel(x_ref, mod_ref, g_ref, wa_ref, wb_ref, wo_ref, o_ref, h_ref, acc_ref, *, sub, nj):
    j = pl.program_id(1)

    @pl.when(j == 0)
    def _():
        h = _norm_mod(x_ref[...], g_ref[...], mod_ref[3 * sub:3 * sub + 1, :], mod_ref[3 * sub + 1:3 * sub + 2, :])
        h_ref[...] = h.astype(BF16)
        acc_ref[...] = jnp.zeros_like(acc_ref)

    h = h_ref[...]
    a = _dot(h, wa_ref[...])
    b = _dot(h, wb_ref[...])
    acc_ref[...] += _dot((_silu(a) * b).astype(BF16), wo_ref[...])

    @pl.when(j == nj - 1)
    def _():
        o_ref[...] = x_ref[...] + (0.5 * mod_ref[3 * sub + 2:3 * sub + 3, :]) * acc_ref[...]


def _ffn(x, mods_l, g, w_in, w_out, sub, tf=512):
    n = x.shape[0]
    nj = D_FF // tf
    return pl.pallas_call(
        functools.partial(_ffn_kernel, sub=sub, nj=nj),
        grid=(n // TM, nj),
        in_specs=[pl.BlockSpec((TM, D_MODEL), lambda i, j: (i, 0)),
                  _mod_spec(),
                  pl.BlockSpec((1, D_MODEL), lambda i, j: (0, 0)),
                  pl.BlockSpec((D_MODEL, tf), lambda i, j: (0, j)),
                  pl.BlockSpec((D_MODEL, tf), lambda i, j: (0, j + nj)),
                  pl.BlockSpec((tf, D_MODEL), lambda i, j: (j, 0))],
        out_specs=pl.BlockSpec((TM, D_MODEL), lambda i, j: (i, 0)),
        out_shape=jax.ShapeDtypeStruct((n, D_MODEL), F32),
        scratch_shapes=[pltpu.VMEM((TM, D_MODEL), BF16), pltpu.VMEM((TM, D_MODEL), F32)],
        compiler_params=_params(("parallel", "arbitrary")),
        name="ffn",
    )(x, mods_l, g, w_in, w_in, w_out)


PROJ_WIDTHS = (SSD_INNER, SSD_XBC, DT_PAD, S5_WIDTH, ATT_WIDTH + 2 * ATT_KV_WIDTH, ATT_WIDTH + 2 * ATT_KV_WIDTH)


def _proj_kernel(x_ref, mod_ref, g_ref, *refs):
    nw = len(PROJ_WIDTHS)
    w_refs, h_ref, o_refs = refs[:nw], refs[nw], refs[nw + 1:]
    h = _norm_mod(x_ref[...], g_ref[...], mod_ref[3:4, :], mod_ref[4:5, :]).astype(BF16)
    h_ref[...] = h
    for w_ref, o_ref in zip(w_refs, o_refs):
        o_ref[...] = _dot(h, w_ref[...])


def _proj(x, mods_l, g, ws):
    n = x.shape[0]
    tm = PROJ_TM
    row = lambda w: pl.BlockSpec((tm, w), lambda i: (i, 0))
    return pl.pallas_call(
        _proj_kernel,
        grid=(n // tm,),
        in_specs=[row(D_MODEL),
                  pl.BlockSpec((None, N_MOD, D_MODEL), lambda i: (jnp.minimum(i // (TM // tm), 1), 0, 0)),
                  pl.BlockSpec((1, D_MODEL), lambda i: (0, 0))]
                 + [pl.BlockSpec((D_MODEL, w), lambda i: (0, 0), pipeline_mode=pl.Buffered(1)) for w in PROJ_WIDTHS],
        out_specs=[row(D_MODEL)] + [row(w) for w in PROJ_WIDTHS],
        out_shape=[jax.ShapeDtypeStruct((n, D_MODEL), BF16)]
                  + [jax.ShapeDtypeStruct((n, w), F32) for w in PROJ_WIDTHS],
        compiler_params=_params(("parallel",)),
        name="proj",
    )(x, mods_l, g, *ws)


QK_W = ATT_WIDTH + ATT_KV_WIDTH


def _attn_prep_kernel(qkv_ref, g_ref, cos_ref, sin_ref, bd_ref, q_ref, kt_ref, v_ref):
    x = qkv_ref[...]
    qk = x[:, :QK_W]
    sq = qk * qk
    hi = sq.astype(BF16)
    lo = (sq - hi.astype(F32)).astype(BF16)
    bd = bd_ref[...]
    ms = _dot(hi, bd) + _dot(lo, bd)
    y = qk * lax.rsqrt(ms + EPS) * g_ref[...]
    cos = cos_ref[...]
    sin = sin_ref[...]
    lane = lax.broadcasted_iota(jnp.int32, (x.shape[0], LANES), 1)
    first_half = (lane % HEAD_DIM) < (HEAD_DIM // 2)
    tiles = []
    for t in range(QK_W // LANES):
        yt = y[:, t * LANES:(t + 1) * LANES]
        rot = jnp.where(first_half, pltpu.roll(yt, LANES - HEAD_DIM // 2, 1), pltpu.roll(yt, HEAD_DIM // 2, 1))
        tiles.append(yt * cos + rot * sin)
    for h in range(ATT_Q_HEADS):
        t = tiles[h // 2]
        q_ref[h] = (t[:, (h % 2) * HEAD_DIM:(h % 2 + 1) * HEAD_DIM] * ATT_SCALE).astype(BF16)
    kt = tiles[ATT_WIDTH // LANES].T
    kt_ref[0] = kt[:HEAD_DIM].astype(BF16)
    kt_ref[1] = kt[HEAD_DIM:].astype(BF16)
    v = x[:, QK_W:QK_W + LANES]
    one_col = jnp.where(lane == HEAD_DIM, 1.0, 0.0)
    v_ref[0] = jnp.where(lane < HEAD_DIM, v, one_col).astype(BF16)
    v_ref[1] = jnp.where(lane < HEAD_DIM, pltpu.roll(v, HEAD_DIM, 1), one_col).astype(BF16)


def _attn_prep(qkv, g640, cos_t, sin_t, bd):
    s = cos_t.shape[0]
    tm = PROJ_TM
    skip = TM // tm - CTX_LEN // tm
    src = lambda i: jnp.where(i < CTX_LEN // tm, i, i + skip)
    return pl.pallas_call(
        _attn_prep_kernel,
        grid=(s // tm,),
        in_specs=[pl.BlockSpec((tm, ATT_WIDTH + 2 * ATT_KV_WIDTH), lambda i: (src(i), 0)),
                  pl.BlockSpec((1, QK_W), lambda i: (0, 0)),
                  pl.BlockSpec((tm, LANES), lambda i: (i, 0)),
                  pl.BlockSpec((tm, LANES), lambda i: (i, 0)),
                  pl.BlockSpec((QK_W, QK_W), lambda i: (0, 0))],
        out_specs=[pl.BlockSpec((ATT_Q_HEADS, tm, HEAD_DIM), lambda i: (0, i, 0)),
                   pl.BlockSpec((ATT_KV_HEADS, HEAD_DIM, tm), lambda i: (0, 0, i)),
                   pl.BlockSpec((ATT_KV_HEADS, tm, LANES), lambda i: (0, i, 0))],
        out_shape=[jax.ShapeDtypeStruct((ATT_Q_HEADS, s, HEAD_DIM), BF16),
                   jax.ShapeDtypeStruct((ATT_KV_HEADS, HEAD_DIM, s), BF16),
                   jax.ShapeDtypeStruct((ATT_KV_HEADS, s, LANES), BF16)],
        compiler_params=_params(("parallel",)),
        name="attn_prep",
    )(qkv, g640, cos_t, sin_t, bd)


def _heads_out(acc, rows):
    outs = []
    for h in range(GQA_GROUP):
        a = acc[h * rows:(h + 1) * rows]
        outs.append(a[:, :HEAD_DIM] / a[:, HEAD_DIM:HEAD_DIM + 1])
    return jnp.concatenate(outs, axis=1)


def _glb_kernel(q_ref, kt_ref, v_ref, o_ref, *, nk):
    tq = q_ref.shape[1]
    m_rows = GQA_GROUP * tq
    q = q_ref[...].reshape(m_rows, HEAD_DIM)

    def body(c, carry):
        m, acc = carry
        s = _dot(q, kt_ref[c])
        m_new = jnp.maximum(m, jnp.max(s, axis=1, keepdims=True))
        p = jnp.exp(s - m_new)
        acc = jnp.exp(m - m_new) * acc + _dot(p.astype(BF16), v_ref[c])
        return m_new, acc

    m0 = jnp.full((m_rows, 1), NEG_INF, F32)
    acc0 = jnp.zeros((m_rows, LANES), F32)
    _, acc = lax.fori_loop(0, nk, body, (m0, acc0))
    o_ref[...] = _heads_out(acc, tq).astype(BF16)


def _glb_attn(q, kt, v, n_rows):
    s = q.shape[1]
    nk, tk = kt.shape[1], kt.shape[3]
    tq = GLB_TQ
    nq = (s - CTX_LEN) // tq
    half = GQA_GROUP * HEAD_DIM
    return pl.pallas_call(
        functools.partial(_glb_kernel, nk=nk),
        grid=(ATT_KV_HEADS, nq),
        in_specs=[pl.BlockSpec((GQA_GROUP, tq, HEAD_DIM), lambda kv, i: (kv, i + CTX_LEN // tq, 0)),
                  pl.BlockSpec((None, nk, HEAD_DIM, tk), lambda kv, i: (kv, 0, 0, 0)),
                  pl.BlockSpec((None, nk, tk, LANES), lambda kv, i: (kv, 0, 0, 0))],
        out_specs=pl.BlockSpec((tq, half), lambda kv, i: (i + TM // tq, kv)),
        out_shape=jax.ShapeDtypeStruct((n_rows, ATT_WIDTH), BF16),
        compiler_params=_params(("arbitrary", "arbitrary")),
        name="glb_attn",
    )(q, kt, v)


def _swa_kernel(sink_ref, q_ref, ktp_ref, ktc_ref, ktn_ref, ktx_ref, vp_ref, vc_ref, vn_ref, vx_ref, o_ref, *, nb):
    n = pl.program_id(0)
    w = SWA_WINDOW
    rows = GQA_GROUP * w
    qi = lax.broadcasted_iota(jnp.int32, (rows, w), 0) % w
    kj = lax.broadcasted_iota(jnp.int32, (rows, w), 1)
    ok_prev = (kj >= qi) & (n > 0)
    ok_next = (kj <= qi) & (n < nb - 1)
    outs = []
    for kv in range(ATT_KV_HEADS):
        q = q_ref[kv * GQA_GROUP:(kv + 1) * GQA_GROUP].reshape(rows, HEAD_DIM)
        sp = jnp.where(ok_prev, _dot(q, ktp_ref[kv]), NEG_INF)
        sc = _dot(q, ktc_ref[kv])
        sn = jnp.where(ok_next, _dot(q, ktn_ref[kv]), NEG_INF)
        sx = _dot(q, ktx_ref[kv])
        sink = jnp.concatenate([jnp.full((w, 1), sink_ref[kv * GQA_GROUP + h], F32) for h in range(GQA_GROUP)], axis=0)
        m = jnp.maximum(jnp.maximum(jnp.max(sp, axis=1, keepdims=True), jnp.max(sc, axis=1, keepdims=True)),
                        jnp.maximum(jnp.max(sn, axis=1, keepdims=True), jnp.max(sx, axis=1, keepdims=True)))
        m = jnp.maximum(m, sink)
        acc = (_dot(jnp.exp(sp - m).astype(BF16), vp_ref[kv]) + _dot(jnp.exp(sc - m).astype(BF16), vc_ref[kv])
               + _dot(jnp.exp(sn - m).astype(BF16), vn_ref[kv]) + _dot(jnp.exp(sx - m).astype(BF16), vx_ref[kv]))
        lane = lax.broadcasted_iota(jnp.int32, acc.shape, 1)
        acc = acc + jnp.where(lane == HEAD_DIM, jnp.exp(sink - m), 0.0)
        outs.append(_heads_out(acc, w))
    o_ref[...] = jnp.concatenate(outs, axis=1).astype(BF16)


def _swa_attn(q, kt, v, sink, n_rows):
    s = q.shape[1]
    w = SWA_WINDOW
    nb = (s - CTX_LEN) // w
    c0 = CTX_LEN // w
    prv = lambda n: jnp.maximum(n - 1, 0) + c0
    nxt = lambda n: jnp.minimum(n + 1, nb - 1) + c0
    kt_spec = lambda f: pl.BlockSpec((ATT_KV_HEADS, HEAD_DIM, w), lambda n: (0, 0, f(n)))
    v_spec = lambda f: pl.BlockSpec((ATT_KV_HEADS, w, LANES), lambda n: (0, f(n), 0))
    return pl.pallas_call(
        functools.partial(_swa_kernel, nb=nb),
        grid=(nb,),
        in_specs=[pl.BlockSpec(memory_space=pltpu.SMEM),
                  pl.BlockSpec((ATT_Q_HEADS, w, HEAD_DIM), lambda n: (0, n + c0, 0)),
                  kt_spec(prv), kt_spec(lambda n: n + c0), kt_spec(nxt),
                  pl.BlockSpec((ATT_KV_HEADS, HEAD_DIM, CTX_LEN), lambda n: (0, 0, 0)),
                  v_spec(prv), v_spec(lambda n: n + c0), v_spec(nxt),
                  pl.BlockSpec((ATT_KV_HEADS, CTX_LEN, LANES), lambda n: (0, 0, 0))],
        out_specs=pl.BlockSpec((w, ATT_WIDTH), lambda n: (n + TM // w, 0)),
        out_shape=jax.ShapeDtypeStruct((n_rows, ATT_WIDTH), BF16),
        compiler_params=_params(("parallel",)),
        name="swa_attn",
    )(sink, q, kt, kt, kt, kt, v, v, v, v)


def _ctx_attn_kernel(sink_ref, q_ref, kt_ref, v_ref, prev_ref, o_ref, *, use_sink):
    del prev_ref
    rows = GQA_GROUP * CTX_LEN
    outs = []
    for kv in range(ATT_KV_HEADS):
        q = q_ref[kv * GQA_GROUP:(kv + 1) * GQA_GROUP].reshape(rows, HEAD_DIM)
        s = _dot(q, kt_ref[kv])
        m = jnp.max(s, axis=1, keepdims=True)
        if use_sink:
            sink = jnp.concatenate([jnp.full((CTX_LEN, 1), sink_ref[kv * GQA_GROUP + h], F32)
                                    for h in range(GQA_GROUP)], axis=0)
            m = jnp.maximum(m, sink)
        acc = _dot(jnp.exp(s - m).astype(BF16), v_ref[kv])
        if use_sink:
            lane = lax.broadcasted_iota(jnp.int32, acc.shape, 1)
            acc = acc + jnp.where(lane == HEAD_DIM, jnp.exp(sink - m), 0.0)
        outs.append(_heads_out(acc, CTX_LEN))
    o_ref[:CTX_LEN] = jnp.concatenate(outs, axis=1).astype(BF16)
    o_ref[CTX_LEN:] = jnp.zeros((TM - CTX_LEN, ATT_WIDTH), BF16)


def _ctx_attn(q, kt, v, sink, y_prev, use_sink):
    return pl.pallas_call(
        functools.partial(_ctx_attn_kernel, use_sink=use_sink),
        grid=(1,),
        in_specs=[pl.BlockSpec(memory_space=pltpu.SMEM),
                  pl.BlockSpec((ATT_Q_HEADS, CTX_LEN, HEAD_DIM), lambda i: (0, 0, 0)),
                  pl.BlockSpec((ATT_KV_HEADS, HEAD_DIM, CTX_LEN), lambda i: (0, 0, 0)),
                  pl.BlockSpec((ATT_KV_HEADS, CTX_LEN, LANES), lambda i: (0, 0, 0)),
                  pl.BlockSpec(memory_space=pl.ANY)],
        out_specs=pl.BlockSpec((TM, ATT_WIDTH), lambda i: (0, 0)),
        out_shape=jax.ShapeDtypeStruct(y_prev.shape, y_prev.dtype),
        input_output_aliases={4: 0},
        compiler_params=_params(("arbitrary",)),
        name="ctx_attn",
    )(sink, q, kt, v, y_prev)


def _ssd_prep_kernel(xp_ref, xc_ref, xn_ref, dt_ref, w_ref, b_ref, dtb_ref, xo_ref, dto_ref, *, nblk):
    b = pl.program_id(0)
    t = SSD_T
    lat0 = TM // t
    has_prev = jnp.logical_and(b != 0, b != lat0)
    has_next = jnp.logical_and(b != CTX_LEN // t - 1, b != nblk - 1)
    cur = xc_ref[...]
    prev = jnp.where(has_prev, xp_ref[...], 0.0)
    nxt = jnp.where(has_next, xn_ref[...], 0.0)
    row = lax.broadcasted_iota(jnp.int32, cur.shape, 0)
    w = w_ref[...]
    acc = cur * w[2:3, :] + b_ref[...]
    for k in (1, 2):
        down = jnp.where(row < k, pltpu.roll(prev, k, 0), pltpu.roll(cur, k, 0))
        up = jnp.where(row >= t - k, pltpu.roll(nxt, t - k, 0), pltpu.roll(cur, t - k, 0))
        acc = acc + down * w[2 - k:3 - k, :] + up * w[2 + k:3 + k, :]
    xo_ref[...] = _silu(acc)
    d = dt_ref[...] + dtb_ref[...]
    dto_ref[...] = jnp.maximum(d, 0.0) + jnp.log1p(jnp.exp(-jnp.abs(d)))


def _ssd_prep(xbc, dt, conv_w, conv_b, dt_bias):
    n = xbc.shape[0]
    t = SSD_T
    nblk = n // t
    return pl.pallas_call(
        functools.partial(_ssd_prep_kernel, nblk=nblk),
        grid=(nblk,),
        in_specs=[pl.BlockSpec((t, SSD_XBC), lambda b: (jnp.maximum(b - 1, 0), 0)),
                  pl.BlockSpec((t, SSD_XBC), lambda b: (b, 0)),
                  pl.BlockSpec((t, SSD_XBC), lambda b: (jnp.minimum(b + 1, nblk - 1), 0)),
                  pl.BlockSpec((t, DT_PAD), lambda b: (b, 0)),
                  pl.BlockSpec((8, SSD_XBC), lambda b: (0, 0)),
                  pl.BlockSpec((1, SSD_XBC), lambda b: (0, 0)),
                  pl.BlockSpec((1, DT_PAD), lambda b: (0, 0))],
        out_specs=[pl.BlockSpec((t, SSD_XBC), lambda b: (b, 0)),
                   pl.BlockSpec((t, DT_PAD), lambda b: (b, 0))],
        out_shape=[jax.ShapeDtypeStruct((n, SSD_XBC), F32), jax.ShapeDtypeStruct((n, DT_PAD), F32)],
        compiler_params=_params(("parallel",)),
        name="ssd_prep",
    )(xbc, xbc, xbc, dt, conv_w, conv_b, dt_bias)


def _ssd_scan_kernel(tbl_ref, x_ref, dt_ref, arow_ref, tri_ref, e_ref, *rest, direction, final):
    if final:
        yf_ref, z_ref, dsk_ref, ng_ref, o_ref, st_ref = rest
    else:
        o_ref, st_ref = rest
    step = pl.program_id(0)
    flag = tbl_ref[1, step]
    t = SSD_T
    gw = SSD_INNER // SSD_GROUPS
    hpg = SSD_HEADS // SSD_GROUPS

    @pl.when(flag == 2)
    def _():
        o_ref[...] = jnp.zeros_like(o_ref)

    @pl.when(flag == 1)
    def _():
        st_ref[...] = jnp.zeros_like(st_ref)

    @pl.when(flag != 2)
    def _():
        x = x_ref[...]
        xs = x[:, :SSD_INNER]
        bm = x[:, SSD_INNER:SSD_INNER + SSD_BC]
        cm = x[:, SSD_INNER + SSD_BC:]
        dt = dt_ref[...]
        adt = dt * arow_ref[...]
        tri = tri_ref[...]
        keep = tri > 0
        cum = _split_rhs_dot(tri, adt, 3)
        tot = _split_rhs_dot(jnp.ones((t, t), BF16), adt, 3)
        e = e_ref[...]
        dt_e = _split_lhs_dot(dt, e, 2)
        expc_e = _split_lhs_dot(jnp.exp(cum), e, 2)
        decs_e = _split_lhs_dot(jnp.exp(tot - cum), e, 2)
        dch_e = _split_lhs_dot(jnp.exp(tot[:8]), e, 2)[0:1]
        cum_t = cum.T
        xdt = xs * dt_e
        ys = []
        for g in range(SSD_GROUPS):
            bg = bm[:, g * SSD_STATE:(g + 1) * SSD_STATE]
            cgb = cm[:, g * SSD_STATE:(g + 1) * SSD_STATE].astype(BF16)
            cb = _dot_nt(cgb, bg.astype(BF16))
            stg = st_ref[g]
            y_off = _dot(cgb, stg.astype(BF16)) * expc_e[:, g * gw:(g + 1) * gw]
            parts = []
            for j in range(hpg):
                h = g * hpg + j
                r = direction * SSD_HEADS + h
                diff = cum[:, r:r + 1] - cum_t[r:r + 1, :]
                lmat = jnp.exp(jnp.where(keep, diff, NEG_INF))
                wmat = (cb * lmat).astype(BF16)
                parts.append(_dot(wmat, xdt[:, h * SSD_HEAD_DIM:(h + 1) * SSD_HEAD_DIM].astype(BF16)))
            ys.append(jnp.concatenate(parts, axis=1) + y_off)
            xw = (xdt[:, g * gw:(g + 1) * gw] * decs_e[:, g * gw:(g + 1) * gw]).astype(BF16)
            st_ref[g] = stg * dch_e[:, g * gw:(g + 1) * gw] + _dot(bg.T.astype(BF16), xw)
        y = jnp.concatenate(ys, axis=1)
        if final:
            y = yf_ref[...] + y + dsk_ref[...] * xs
            y = y * _silu(z_ref[...])
            ms = jnp.mean(y * y, axis=-1, keepdims=True)
            o_ref[...] = (y * lax.rsqrt(ms + EPS) * ng_ref[...]).astype(BF16)
        else:
            o_ref[...] = y


def _ssd_scan(tbl, xact, dtsp, arow, tri, e, direction, extra=None):
    n = xact.shape[0]
    t = SSD_T
    final = extra is not None
    blk = lambda w: pl.BlockSpec((t, w), lambda s, tb: (tb[0, s], 0))
    const = lambda r, w: pl.BlockSpec((r, w), lambda s, tb: (0, 0))
    in_specs = [blk(SSD_XBC), blk(DT_PAD), const(1, DT_PAD), const(t, t), const(DT_PAD, SSD_INNER)]
    args = [xact, dtsp, arow, tri, e]
    if final:
        in_specs += [blk(SSD_INNER), blk(SSD_INNER), const(1, SSD_INNER), const(1, SSD_INNER)]
        args += list(extra)
    return pl.pallas_call(
        functools.partial(_ssd_scan_kernel, direction=direction, final=final),
        grid_spec=pltpu.PrefetchScalarGridSpec(
            num_scalar_prefetch=1, grid=(tbl.shape[1],), in_specs=in_specs,
            out_specs=blk(SSD_INNER),
            scratch_shapes=[pltpu.VMEM((SSD_GROUPS, SSD_STATE, SSD_INNER // SSD_GROUPS), F32)]),
        out_shape=jax.ShapeDtypeStruct((n, SSD_INNER), BF16 if final else F32),
        compiler_params=_params(("arbitrary",)),
        name="ssd_scan_bwd" if final else "ssd_scan_fwd",
    )(tbl, *args)


def _ssd_tables(n_rows):
    t = SSD_T
    ctx = list(range(CTX_LEN // t))
    pad = list(range(CTX_LEN // t, TM // t))
    lat = list(range(TM // t, n_rows // t))
    fwd = ctx + lat + pad
    bwd = ctx[::-1] + lat[::-1] + pad
    flags = [1] + [0] * (len(ctx) + len(lat) - 1) + [2] * len(pad)
    return (jnp.asarray(np.array([fwd, flags], np.int32)), jnp.asarray(np.array([bwd, flags], np.int32)))


def _s5_consts(a_re, a_im, log_dt, b_re, b_im, c_re, c_im):
    t_len = S5_T
    a_re, a_im, b_re, b_im, c_re, c_im = (v.astype(F32) for v in (a_re, a_im, b_re, b_im, c_re, c_im))
    bs_cols, cs_cols, kmats, lam = [], [], [], []
    for d in range(2):
        dt = jnp.exp(log_dt[d].astype(F32))[:, None]
        mag = jnp.exp(a_re[d] * dt)
        lr, li = mag * jnp.cos(a_im[d] * dt), mag * jnp.sin(a_im[d] * dt)
        den = a_re[d] * a_re[d] + a_im[d] * a_im[d]
        nr, ni = lr - 1.0, li
        f_re = (nr * a_re[d] + ni * a_im[d]) / den
        f_im = (ni * a_re[d] - nr * a_im[d]) / den
        bb_re = f_re[..., None] * b_re - f_im[..., None] * b_im
        bb_im = f_re[..., None] * b_im + f_im[..., None] * b_re
        pw = [(jnp.ones_like(lr), jnp.zeros_like(lr))]
        for _ in range(t_len):
            pr, pi = pw[-1]
            pw.append((pr * lr - pi * li, pr * li + pi * lr))
        pw_re = jnp.stack([p[0] for p in pw])
        pw_im = jnp.stack([p[1] for p in pw])
        e_in = (t_len - 1 - np.arange(t_len)) if d == 0 else np.arange(t_len)
        gr = pw_re[e_in][:, :, :, None] * bb_re[None] - pw_im[e_in][:, :, :, None] * bb_im[None]
        gi = pw_re[e_in][:, :, :, None] * bb_im[None] + pw_im[e_in][:, :, :, None] * bb_re[None]
        to_rows = lambda v: jnp.transpose(v, (1, 0, 3, 2)).reshape(v.shape[1], t_len * S5_GROUP, S5_STATE)
        bs_cols.append((to_rows(gr), to_rows(gi)))
        e_out = (np.arange(t_len) + 1) if d == 0 else (t_len - np.arange(t_len))
        hr = c_re[None] * pw_re[e_out][:, :, None, :] - c_im[None] * pw_im[e_out][:, :, None, :]
        hi = c_re[None] * pw_im[e_out][:, :, None, :] + c_im[None] * pw_re[e_out][:, :, None, :]
        to_rows_c = lambda v: jnp.transpose(v, (1, 0, 2, 3)).reshape(v.shape[1], t_len * S5_GROUP, S5_STATE)
        cs_cols.append((to_rows_c(hr), -to_rows_c(hi)))
        tau = np.arange(t_len)
        kr = pw_re[tau][:, :, :, None] * bb_re[None] - pw_im[tau][:, :, :, None] * bb_im[None]
        ki = pw_re[tau][:, :, :, None] * bb_im[None] + pw_im[tau][:, :, :, None] * bb_re[None]
        kmats.append(jnp.einsum('tgpi,ghp->gtih', kr, c_re, precision=lax.Precision.HIGHEST)
                     - jnp.einsum('tgpi,ghp->gtih', ki, c_im, precision=lax.Precision.HIGHEST))
        lam.append((pw_re[t_len], pw_im[t_len]))
    g = a_re.shape[1]
    ti = np.arange(t_len)
    lag = ti[None, :] - ti[:, None]
    mf = jnp.where((lag >= 0)[None, :, :, None, None], kmats[0][:, np.clip(lag, 0, None)], 0.0)
    mb = jnp.where((lag <= 0)[None, :, :, None, None], kmats[1][:, np.clip(-lag, 0, None)], 0.0)
    m = jnp.transpose(mf + mb, (0, 1, 3, 2, 4)).reshape(g, t_len * S5_GROUP, t_len * S5_GROUP)
    zero = jnp.zeros_like(bs_cols[0][0])
    bs = jnp.concatenate([bs_cols[0][0], bs_cols[1][0], bs_cols[0][1], bs_cols[1][1]], axis=-1)
    csa = jnp.concatenate([cs_cols[0][0], zero, cs_cols[0][1], zero], axis=-1)
    csb = jnp.concatenate([zero, cs_cols[1][0], zero, cs_cols[1][1]], axis=-1)
    lam_re = jnp.concatenate([lam[0][0], lam[1][0]], axis=-1)
    lam_im = jnp.concatenate([lam[0][1], lam[1][1]], axis=-1)
    return m.astype(BF16), bs.astype(BF16), csa.astype(BF16), csb.astype(BF16), lam_re, lam_im


def _s5_kernel(u_ref, m_ref, bs_ref, csa_ref, csb_ref, lre_ref, lim_ref, y_ref, v_ref, sa_ref, sb_ref, *, nctx, nch):
    gb = S5_GB
    for j in range(gb):
        v = _dot(u_ref[j], bs_ref[j])
        v_ref.at[0][pl.ds(j, nch, stride=gb), :] = v[:, :LANES]
        v_ref.at[1][pl.ds(j, nch, stride=gb), :] = v[:, LANES:]
    lre = lre_ref[0]
    lim = lim_ref[0]
    is_fwd = lax.broadcasted_iota(jnp.int32, (gb, LANES), 1) < S5_STATE

    def body(i, carry):
        sre, sim = carry
        rf = i * gb
        rb = jnp.where(i < nctx, nctx - 1 - i, nch + nctx - 1 - i) * gb
        sa_ref[0, pl.ds(rf, gb), :] = sre
        sa_ref[1, pl.ds(rf, gb), :] = sim
        sb_ref[0, pl.ds(rb, gb), :] = sre
        sb_ref[1, pl.ds(rb, gb), :] = sim
        vre = jnp.where(is_fwd, v_ref[0, pl.ds(rf, gb), :], v_ref[0, pl.ds(rb, gb), :])
        vim = jnp.where(is_fwd, v_ref[1, pl.ds(rf, gb), :], v_ref[1, pl.ds(rb, gb), :])
        return lre * sre - lim * sim + vre, lre * sim + lim * sre + vim

    zero = jnp.zeros((gb, LANES), F32)
    lax.fori_loop(0, nch, body, (zero, zero))
    for j in range(gb):
        rows = pl.ds(j, nch, stride=gb)
        sa = jnp.concatenate([sa_ref.at[0][rows, :], sa_ref.at[1][rows, :]], axis=1).astype(BF16)
        sb = jnp.concatenate([sb_ref.at[0][rows, :], sb_ref.at[1][rows, :]], axis=1).astype(BF16)
        y_ref[j] = _dot(u_ref[j], m_ref[j]) + _dot_nt(sa, csa_ref[j]) + _dot_nt(sb, csb_ref[j])


def _s5_scan(ug, m, bs, csa, csb, lam_re, lam_im):
    g, nch, tw = ug.shape
    gb = S5_GB
    sw = 4 * S5_STATE
    blk = lambda a, b: pl.BlockSpec((gb, a, b), lambda i: (i, 0, 0))
    lam_spec = pl.BlockSpec((1, gb, 2 * S5_STATE), lambda i: (i, 0, 0))
    return pl.pallas_call(
        functools.partial(_s5_kernel, nctx=CTX_LEN // S5_T, nch=nch),
        grid=(g // gb,),
        in_specs=[blk(nch, tw), blk(tw, tw), blk(tw, sw), blk(tw, sw), blk(tw, sw), lam_spec, lam_spec],
        out_specs=blk(nch, tw),
        out_shape=jax.ShapeDtypeStruct((g, nch, tw), F32),
        scratch_shapes=[pltpu.VMEM((2, nch * gb, LANES), F32) for _ in range(3)],
        compiler_params=_params(("parallel",)),
        name="s5_scan",
    )(ug, m, bs, csa, csb, lam_re.reshape(g // gb, gb, -1), lam_im.reshape(g // gb, gb, -1))


def _s5_glu_kernel(y_ref, u_ref, d_ref, w_ref, b_ref, o_ref):
    y = jax.nn.gelu(y_ref[...] + d_ref[...] * u_ref[...])
    t = _dot(y.astype(BF16), w_ref[...]) + b_ref[...]
    o_ref[...] = (t[:, :S5_WIDTH] * jax.nn.sigmoid(t[:, S5_WIDTH:])).astype(BF16)


def _s5_glu(y, u, d, w, b):
    n = y.shape[0]
    row = lambda wd: pl.BlockSpec((TM, wd), lambda i: (i, 0))
    return pl.pallas_call(
        _s5_glu_kernel,
        grid=(n // TM,),
        in_specs=[row(S5_WIDTH), row(S5_WIDTH), pl.BlockSpec((1, S5_WIDTH), lambda i: (0, 0)),
                  pl.BlockSpec((S5_WIDTH, 2 * S5_WIDTH), lambda i: (0, 0)),
                  pl.BlockSpec((1, 2 * S5_WIDTH), lambda i: (0, 0))],
        out_specs=row(S5_WIDTH),
        out_shape=jax.ShapeDtypeStruct((n, S5_WIDTH), BF16),
        compiler_params=_params(("parallel",)),
        name="s5_glu",
    )(y, u, d, w, b)


BR_WIDTHS = (SSD_INNER, S5_WIDTH, ATT_WIDTH, ATT_WIDTH)


def _merge_kernel(h_ref, *refs):
    y_refs, wg_refs, wb_refs, o_ref = refs[:4], refs[4:8], refs[8:12], refs[12]
    h = h_ref[...]
    acc = None
    for y_ref, wg_ref, wb_ref in zip(y_refs, wg_refs, wb_refs):
        t = jax.nn.sigmoid(_dot(h, wg_ref[...])) * _dot(y_ref[...], wb_ref[...])
        acc = t if acc is None else acc + t
    o_ref[...] = acc.astype(BF16)


def _merge(hb, ys, w_gate, w_brs, tn=512):
    n = hb.shape[0]
    nj = D_MODEL // tn
    row = lambda w: pl.BlockSpec((TM, w), lambda i, j: (i, 0))
    return pl.pallas_call(
        _merge_kernel,
        grid=(n // TM, nj),
        in_specs=[row(D_MODEL)] + [row(w) for w in BR_WIDTHS]
                 + [pl.BlockSpec((D_MODEL, tn), lambda i, j, b=b: (0, b * nj + j)) for b in range(N_BRANCH)]
                 + [pl.BlockSpec((w, tn), lambda i, j: (0, j)) for w in BR_WIDTHS],
        out_specs=pl.BlockSpec((TM, tn), lambda i, j: (i, j)),
        out_shape=jax.ShapeDtypeStruct((n, D_MODEL), BF16),
        compiler_params=_params(("parallel", "arbitrary")),
        name="merge",
    )(hb, *ys, w_gate, w_gate, w_gate, w_gate, *w_brs)


def _out_kernel(x_ref, a_ref, mod_ref, w_ref, o_ref):
    o_ref[...] = x_ref[...] + mod_ref[5:6, :] * _dot(a_ref[...], w_ref[...])


def _out_proj(x, acc, mods_l, w_out):
    n = x.shape[0]
    row = lambda: pl.BlockSpec((TM, D_MODEL), lambda i: (i, 0))
    return pl.pallas_call(
        _out_kernel,
        grid=(n // TM,),
        in_specs=[row(), row(), _mod_spec(),
                  pl.BlockSpec((D_MODEL, D_MODEL), lambda i: (0, 0), pipeline_mode=pl.Buffered(1))],
        out_specs=row(),
        out_shape=jax.ShapeDtypeStruct((n, D_MODEL), F32),
        compiler_params=_params(("parallel",)),
        name="out_proj",
    )(x, acc, mods_l, w_out)


def _rope_tables(seq):
    rows = seq // GRID_W
    row = jnp.repeat(jnp.arange(rows, dtype=F32), GRID_W)
    col = jnp.tile(jnp.arange(GRID_W, dtype=F32), rows)
    n_freq = HEAD_DIM // 4
    inv = ROPE_THETA ** (-jnp.arange(n_freq, dtype=F32) / n_freq)
    ang = jnp.concatenate([row[:, None] * inv, col[:, None] * inv], axis=-1)
    cos, sin = jnp.cos(ang), jnp.sin(ang)
    reps = LANES // HEAD_DIM
    cos_t = jnp.tile(jnp.concatenate([cos, cos], axis=-1), (1, reps))
    sin_t = jnp.tile(jnp.concatenate([-sin, sin], axis=-1), (1, reps))
    cos_t = jnp.concatenate([jnp.ones((CTX_LEN, LANES), F32), cos_t], axis=0)
    sin_t = jnp.concatenate([jnp.zeros((CTX_LEN, LANES), F32), sin_t], axis=0)
    return cos_t, sin_t


def _ssd_expand(direction):
    e = np.zeros((DT_PAD, SSD_INNER), np.float32)
    for h in range(SSD_HEADS):
        e[direction * SSD_HEADS + h, h * SSD_HEAD_DIM:(h + 1) * SSD_HEAD_DIM] = 1.0
    return jnp.asarray(e, BF16)


def _to_stream(ctx_rows, lat_rows):
    pad = jnp.zeros((TM - CTX_LEN,) + ctx_rows.shape[1:], ctx_rows.dtype)
    return jnp.concatenate([ctx_rows, pad, lat_rows], axis=0)


def kernel(x, c, ctx, c_ctx, w_mod, b_mod, norm_g, ffn_in, ffn_out, w_in, ssd_conv_w, ssd_conv_b, ssd_a_log, ssd_dt_bias, ssd_d, ssd_norm_g, s5_a_re, s5_a_im, s5_log_dt, s5_b_re, s5_b_im, s5_c_re, s5_c_im, s5_d, s5_glu_w, s5_glu_b, swa_qk_g, swa_sink, glb_qk_g, w_br_ssd, w_br_s5, w_br_swa, w_br_glb, w_out):
    assert x.shape[0] == 1 and ctx.shape[1] == CTX_LEN
    depth = w_mod.shape[0]
    seq = x.shape[1]
    n_rows = TM + seq
    s_len = CTX_LEN + seq

    ffn_in_b = ffn_in.astype(BF16)
    ffn_out_b = ffn_out.astype(BF16)
    offs = np.cumsum((0, SSD_INNER, SSD_XBC, 2 * SSD_HEADS, S5_WIDTH, ATT_WIDTH, ATT_KV_WIDTH, ATT_KV_WIDTH,
                      ATT_WIDTH, ATT_KV_WIDTH, ATT_KV_WIDTH))
    seg = lambda a, b: w_in[:, :, offs[a]:offs[b]].astype(BF16)
    w_dt = jnp.pad(seg(2, 3), ((0, 0), (0, 0), (0, DT_PAD - 2 * SSD_HEADS)))
    w_proj = (seg(0, 1), seg(1, 2), w_dt, seg(3, 4), seg(4, 7), seg(7, 10))
    w_gate = w_in[:, :, offs[10]:].astype(BF16)
    w_brs = tuple(w.astype(BF16) for w in (w_br_ssd, w_br_s5, w_br_swa, w_br_glb))
    w_out_b = w_out.astype(BF16)
    glu_w_b = s5_glu_w.astype(BF16)

    mods = _mods(c, c_ctx, w_mod, b_mod)
    cos_t, sin_t = _rope_tables(seq)
    bd = jnp.asarray(np.kron(np.eye(QK_W // HEAD_DIM), np.ones((HEAD_DIM, HEAD_DIM))) / HEAD_DIM, BF16)
    tbl_f, tbl_b = _ssd_tables(n_rows)
    tril = jnp.asarray(np.tril(np.ones((SSD_T, SSD_T), np.float32)), BF16)
    triu = jnp.asarray(np.triu(np.ones((SSD_T, SSD_T), np.float32)), BF16)
    e_dirs = (_ssd_expand(0), _ssd_expand(1))
    nch = s_len // S5_T
    nk = s_len // GLB_TK

    xs = _to_stream(ctx[0], x[0])
    for i in range(depth):
        ml = mods[i]
        xs = _ffn(xs, ml, norm_g[i, 0:1], ffn_in_b[i, 0], ffn_out_b[i, 0], 0)
        hb, z, xbc, dt, u, qkv_swa, qkv_glb = _proj(xs, ml, norm_g[i, 1:2], [w[i] for w in w_proj])

        conv_w = jnp.pad(ssd_conv_w[i], ((0, 8 - SSD_CONV), (0, 0)))
        dt_bias = jnp.pad(ssd_dt_bias[i].reshape(1, -1), ((0, 0), (0, DT_PAD - 2 * SSD_HEADS)))
        xact, dtsp = _ssd_prep(xbc, dt, conv_w, ssd_conv_b[i].reshape(1, -1), dt_bias)
        a_neg = -jnp.exp(ssd_a_log[i].astype(F32))
        arow = lambda d: jnp.zeros((1, DT_PAD), F32).at[0, d * SSD_HEADS:(d + 1) * SSD_HEADS].set(a_neg[d])
        y_f = _ssd_scan(tbl_f, xact, dtsp, arow(0), tril, e_dirs[0], 0)
        d_exp = jnp.repeat(ssd_d[i].astype(F32), SSD_HEAD_DIM).reshape(1, -1)
        y_ssd = _ssd_scan(tbl_b, xact, dtsp, arow(1), triu, e_dirs[1], 1,
                          extra=(y_f, z, d_exp, ssd_norm_g[i].reshape(1, -1)))

        s5m, s5bs, s5csa, s5csb, lam_re, lam_im = _s5_consts(
            s5_a_re[i], s5_a_im[i], s5_log_dt[i], s5_b_re[i], s5_b_im[i], s5_c_re[i], s5_c_im[i])
        u_c = jnp.concatenate([u[:CTX_LEN], u[TM:]], axis=0)
        ug = u_c.reshape(nch, S5_T, S5_GROUPS, S5_GROUP).transpose(2, 0, 1, 3).reshape(S5_GROUPS, nch, -1)
        yg = _s5_scan(ug.astype(BF16), s5m, s5bs, s5csa, s5csb, lam_re, lam_im)
        y_c = yg.reshape(S5_GROUPS, nch, S5_T, S5_GROUP).transpose(1, 2, 0, 3).reshape(s_len, S5_WIDTH)
        y_s5 = _s5_glu(_to_stream(y_c[:CTX_LEN], y_c[CTX_LEN:]), u, s5_d[i].reshape(1, -1), glu_w_b[i],
                       s5_glu_b[i].reshape(1, -1))

        g_swa = jnp.concatenate([jnp.tile(swa_qk_g[i, 0], ATT_Q_HEADS), jnp.tile(swa_qk_g[i, 1], ATT_KV_HEADS)])[None]
        q, kt, v = _attn_prep(qkv_swa, g_swa, cos_t, sin_t, bd)
        y_swa = _swa_attn(q, kt, v, swa_sink[i], n_rows)
        y_swa = _ctx_attn(q, kt, v, swa_sink[i], y_swa, True)

        g_glb = jnp.concatenate([jnp.tile(glb_qk_g[i, 0], ATT_Q_HEADS), jnp.tile(glb_qk_g[i, 1], ATT_KV_HEADS)])[None]
        q, kt, v = _attn_prep(qkv_glb, g_glb, cos_t, sin_t, bd)
        kt_c = kt.reshape(ATT_KV_HEADS, HEAD_DIM, nk, GLB_TK).transpose(0, 2, 1, 3)
        y_glb = _glb_attn(q, kt_c, v.reshape(ATT_KV_HEADS, nk, GLB_TK, LANES), n_rows)
        y_glb = _ctx_attn(q, kt, v, swa_sink[i], y_glb, False)

        acc = _merge(hb, (y_ssd, y_s5, y_swa, y_glb), w_gate[i], [w[i] for w in w_brs])
        xs = _out_proj(xs, acc, ml, w_out_b[i])
        xs = _ffn(xs, ml, norm_g[i, 2:3], ffn_in_b[i, 1], ffn_out_b[i, 1], 2)
    return xs[TM:][None]
```

```python
import functools
import math

import numpy as np
import jax
import jax.numpy as jnp
from jax import lax
from jax.experimental import pallas as pl
from jax.experimental.pallas import tpu as pltpu

F32 = jnp.float32
BF16 = jnp.bfloat16

D_MODEL = 2048
GRID_W = 64
CTX_LEN = 256
N_MOD = 9
D_FF = 5632
SSD_HEADS = 12
SSD_HEAD_DIM = 64
SSD_INNER = SSD_HEADS * SSD_HEAD_DIM
SSD_GROUPS = 2
SSD_STATE = 128
SSD_BC = SSD_GROUPS * SSD_STATE
SSD_XBC = SSD_INNER + 2 * SSD_BC
SSD_CONV = 5
S5_WIDTH = 512
S5_GROUP = 16
S5_GROUPS = S5_WIDTH // S5_GROUP
S5_STATE = 64
HEAD_DIM = 64
ATT_Q_HEADS = 8
ATT_KV_HEADS = 2
GQA_GROUP = ATT_Q_HEADS // ATT_KV_HEADS
ATT_WIDTH = ATT_Q_HEADS * HEAD_DIM
ATT_KV_WIDTH = ATT_KV_HEADS * HEAD_DIM
ATT_SCALE = HEAD_DIM ** -0.5
SWA_WINDOW = 128
ROPE_THETA = 10000.0
NEG_INF = -1e30
N_BRANCH = 4
EPS = 1e-6

LANES = 128
VMEM_LIMIT = 56 * 1024 * 1024

TM = 512
SSD_T = 128
S5_T = 16
S5_GB = 4
PROJ_TM = 256
GLB_TQ = 256
SWA_TQ = 256
GLB_TK = 1280
DT_PAD = LANES


def _params(sem, vmem=VMEM_LIMIT):
    return pltpu.CompilerParams(dimension_semantics=sem, vmem_limit_bytes=vmem)


def _silu(x):
    return x * jax.nn.sigmoid(x)


def _norm_mod(x, g, shift, scale):
    ms = jnp.mean(x * x, axis=-1, keepdims=True)
    return (x * lax.rsqrt(ms + EPS) * g) * (1.0 + scale) + shift


def _dot(a, b):
    return jnp.dot(a, b, preferred_element_type=F32)


def _dot_nt(a, b):
    return lax.dot_general(a, b, (((1,), (1,)), ((), ())), preferred_element_type=F32)


def _split_rhs_dot(m, x, parts):
    acc = None
    r = x
    for _ in range(parts):
        hi = r.astype(BF16)
        t = _dot(m, hi)
        acc = t if acc is None else acc + t
        r = r - hi.astype(F32)
    return acc


def _split_lhs_dot(x, m, parts):
    acc = None
    r = x
    for _ in range(parts):
        hi = r.astype(BF16)
        t = _dot(hi, m)
        acc = t if acc is None else acc + t
        r = r - hi.astype(F32)
    return acc


def _mod_kernel(s_ref, w_ref, b_ref, o_ref):
    s = _silu(s_ref[...])
    o_ref[...] = _dot(s.astype(BF16), w_ref[...].astype(BF16)) + b_ref[...]


def _mods(c, c_ctx, w_mod, b_mod):
    depth = w_mod.shape[0]
    s = jnp.zeros((8, D_MODEL), F32).at[0].set(c_ctx).at[1].set(c[0])
    tn = 1024
    out = pl.pallas_call(
        _mod_kernel,
        grid=(depth, N_MOD * D_MODEL // tn),
        in_specs=[pl.BlockSpec((8, D_MODEL), lambda l, j: (0, 0)),
                  pl.BlockSpec((None, D_MODEL, tn), lambda l, j: (l, 0, j)),
                  pl.BlockSpec((None, 1, tn), lambda l, j: (l, 0, j))],
        out_specs=pl.BlockSpec((None, 8, tn), lambda l, j: (l, 0, j)),
        out_shape=jax.ShapeDtypeStruct((depth, 8, N_MOD * D_MODEL), F32),
        compiler_params=_params(("parallel", "parallel")),
        name="mods",
    )(s, w_mod, b_mod.reshape(depth, 1, N_MOD * D_MODEL))
    return out[:, :2].reshape(depth, 2, N_MOD, D_MODEL)


def _mod_spec():
    return pl.BlockSpec((None, N_MOD, D_MODEL), lambda i, *_: (jnp.minimum(i, 1), 0, 0))


def _ffn_kernel(x_ref, mod_ref, g_ref, wa_ref, wb_ref, wo_ref, o_ref, h_ref, acc_ref, *, sub, nj):
    j = pl.program_id(1)

    @pl.when(j == 0)
    def _():
        h = _norm_mod(x_ref[...], g_ref[...], mod_ref[3 * sub:3 * sub + 1, :], mod_ref[3 * sub + 1:3 * sub + 2, :])
        h_ref[...] = h.astype(BF16)
        acc_ref[...] = jnp.zeros_like(acc_ref)

    h = h_ref[...]
    a = _dot(h, wa_ref[...])
    b = _dot(h, wb_ref[...])
    acc_ref[...] += _dot((_silu(a) * b).astype(BF16), wo_ref[...])

    @pl.when(j == nj - 1)
    def _():
        o_ref[...] = x_ref[...] + (0.5 * mod_ref[3 * sub + 2:3 * sub + 3, :]) * acc_ref[...]


def _ffn(x, mods_l, g, w_in, w_out, sub, tf=512):
    n = x.shape[0]
    nj = D_FF // tf
    return pl.pallas_call(
        functools.partial(_ffn_kernel, sub=sub, nj=nj),
        grid=(n // TM, nj),
        in_specs=[pl.BlockSpec((TM, D_MODEL), lambda i, j: (i, 0)),
                  _mod_spec(),
                  pl.BlockSpec((1, D_MODEL), lambda i, j: (0, 0)),
                  pl.BlockSpec((D_MODEL, tf), lambda i, j: (0, j)),
                  pl.BlockSpec((D_MODEL, tf), lambda i, j: (0, j + nj)),
                  pl.BlockSpec((tf, D_MODEL), lambda i, j: (j, 0))],
        out_specs=pl.BlockSpec((TM, D_MODEL), lambda i, j: (i, 0)),
        out_shape=jax.ShapeDtypeStruct((n, D_MODEL), F32),
        scratch_shapes=[pltpu.VMEM((TM, D_MODEL), BF16), pltpu.VMEM((TM, D_MODEL), F32)],
        compiler_params=_params(("parallel", "arbitrary")),
        name="ffn",
    )(x, mods_l, g, w_in, w_in, w_out)


PROJ_WIDTHS = (SSD_INNER, SSD_XBC, DT_PAD, S5_WIDTH, ATT_WIDTH + 2 * ATT_KV_WIDTH, ATT_WIDTH + 2 * ATT_KV_WIDTH)


def _proj_kernel(x_ref, mod_ref, g_ref, *refs):
    nw = len(PROJ_WIDTHS)
    w_refs, h_ref, o_refs = refs[:nw], refs[nw], refs[nw + 1:]
    h = _norm_mod(x_ref[...], g_ref[...], mod_ref[3:4, :], mod_ref[4:5, :]).astype(BF16)
    h_ref[...] = h
    for w_ref, o_ref in zip(w_refs, o_refs):
        o_ref[...] = _dot(h, w_ref[...])


def _proj(x, mods_l, g, ws):
    n = x.shape[0]
    tm = PROJ_TM
    row = lambda w: pl.BlockSpec((tm, w), lambda i: (i, 0))
    return pl.pallas_call(
        _proj_kernel,
        grid=(n // tm,),
        in_specs=[row(D_MODEL),
                  pl.BlockSpec((None, N_MOD, D_MODEL), lambda i: (jnp.minimum(i // (TM // tm), 1), 0, 0)),
                  pl.BlockSpec((1, D_MODEL), lambda i: (0, 0))]
                 + [pl.BlockSpec((D_MODEL, w), lambda i: (0, 0), pipeline_mode=pl.Buffered(1)) for w in PROJ_WIDTHS],
        out_specs=[row(D_MODEL)] + [row(w) for w in PROJ_WIDTHS],
        out_shape=[jax.ShapeDtypeStruct((n, D_MODEL), BF16)]
                  + [jax.ShapeDtypeStruct((n, w), F32) for w in PROJ_WIDTHS],
        compiler_params=_params(("parallel",)),
        name="proj",
    )(x, mods_l, g, *ws)


QK_W = ATT_WIDTH + ATT_KV_WIDTH


def _attn_prep_kernel(qkv_ref, g_ref, cos_ref, sin_ref, bd_ref, q_ref, kt_ref, v_ref):
    x = qkv_ref[...]
    qk = x[:, :QK_W]
    sq = qk * qk
    hi = sq.astype(BF16)
    lo = (sq - hi.astype(F32)).astype(BF16)
    bd = bd_ref[...]
    ms = _dot(hi, bd) + _dot(lo, bd)
    y = qk * lax.rsqrt(ms + EPS) * g_ref[...]
    cos = cos_ref[...]
    sin = sin_ref[...]
    lane = lax.broadcasted_iota(jnp.int32, (x.shape[0], LANES), 1)
    first_half = (lane % HEAD_DIM) < (HEAD_DIM // 2)
    tiles = []
    for t in range(QK_W // LANES):
        yt = y[:, t * LANES:(t + 1) * LANES]
        rot = jnp.where(first_half, pltpu.roll(yt, LANES - HEAD_DIM // 2, 1), pltpu.roll(yt, HEAD_DIM // 2, 1))
        tiles.append(yt * cos + rot * sin)
    for h in range(ATT_Q_HEADS):
        t = tiles[h // 2]
        q_ref[h] = (t[:, (h % 2) * HEAD_DIM:(h % 2 + 1) * HEAD_DIM] * ATT_SCALE).astype(BF16)
    kt = tiles[ATT_WIDTH // LANES].T
    kt_ref[0] = kt[:HEAD_DIM].astype(BF16)
    kt_ref[1] = kt[HEAD_DIM:].astype(BF16)
    v = x[:, QK_W:QK_W + LANES]
    one_col = jnp.where(lane == HEAD_DIM, 1.0, 0.0)
    v_ref[0] = jnp.where(lane < HEAD_DIM, v, one_col).astype(BF16)
    v_ref[1] = jnp.where(lane < HEAD_DIM, pltpu.roll(v, HEAD_DIM, 1), one_col).astype(BF16)


def _attn_prep(qkv, g640, cos_t, sin_t, bd):
    s = cos_t.shape[0]
    tm = PROJ_TM
    skip = TM // tm - CTX_LEN // tm
    src = lambda i: jnp.where(i < CTX_LEN // tm, i, i + skip)
    return pl.pallas_call(
        _attn_prep_kernel,
        grid=(s // tm,),
        in_specs=[pl.BlockSpec((tm, ATT_WIDTH + 2 * ATT_KV_WIDTH), lambda i: (src(i), 0)),
                  pl.BlockSpec((1, QK_W), lambda i: (0, 0)),
                  pl.BlockSpec((tm, LANES), lambda i: (i, 0)),
                  pl.BlockSpec((tm, LANES), lambda i: (i, 0)),
                  pl.BlockSpec((QK_W, QK_W), lambda i: (0, 0))],
        out_specs=[pl.BlockSpec((ATT_Q_HEADS, tm, HEAD_DIM), lambda i: (0, i, 0)),
                   pl.BlockSpec((ATT_KV_HEADS, HEAD_DIM, tm), lambda i: (0, 0, i)),
                   pl.BlockSpec((ATT_KV_HEADS, tm, LANES), lambda i: (0, i, 0))],
        out_shape=[jax.ShapeDtypeStruct((ATT_Q_HEADS, s, HEAD_DIM), BF16),
                   jax.ShapeDtypeStruct((ATT_KV_HEADS, HEAD_DIM, s), BF16),
                   jax.ShapeDtypeStruct((ATT_KV_HEADS, s, LANES), BF16)],
        compiler_params=_params(("parallel",)),
        name="attn_prep",
    )(qkv, g640, cos_t, sin_t, bd)


def _heads_out(acc, rows):
    outs = []
    for h in range(GQA_GROUP):
        a = acc[h * rows:(h + 1) * rows]
        outs.append(a[:, :HEAD_DIM] / a[:, HEAD_DIM:HEAD_DIM + 1])
    return jnp.concatenate(outs, axis=1)


def _glb_kernel(q_ref, kt_ref, v_ref, o_ref, *, nk):
    tq = q_ref.shape[1]
    m_rows = GQA_GROUP * tq
    q = q_ref[...].reshape(m_rows, HEAD_DIM)

    def body(c, carry):
        m, acc = carry
        s = _dot(q, kt_ref[c])
        m_new = jnp.maximum(m, jnp.max(s, axis=1, keepdims=True))
        p = jnp.exp(s - m_new)
        acc = jnp.exp(m - m_new) * acc + _dot(p.astype(BF16), v_ref[c])
        return m_new, acc

    m0 = jnp.full((m_rows, 1), NEG_INF, F32)
    acc0 = jnp.zeros((m_rows, LANES), F32)
    _, acc = lax.fori_loop(0, nk, body, (m0, acc0), unroll=True)
    o_ref[...] = _heads_out(acc, tq).astype(BF16)


def _glb_attn(q, kt, v, n_rows):
    s = q.shape[1]
    nk, tk = kt.shape[1], kt.shape[3]
    tq = GLB_TQ
    nq = (s - CTX_LEN) // tq
    half = GQA_GROUP * HEAD_DIM
    return pl.pallas_call(
        functools.partial(_glb_kernel, nk=nk),
        grid=(ATT_KV_HEADS, nq),
        in_specs=[pl.BlockSpec((GQA_GROUP, tq, HEAD_DIM), lambda kv, i: (kv, i + CTX_LEN // tq, 0)),
                  pl.BlockSpec((None, nk, HEAD_DIM, tk), lambda kv, i: (kv, 0, 0, 0)),
                  pl.BlockSpec((None, nk, tk, LANES), lambda kv, i: (kv, 0, 0, 0))],
        out_specs=pl.BlockSpec((tq, half), lambda kv, i: (i + TM // tq, kv)),
        out_shape=jax.ShapeDtypeStruct((n_rows, ATT_WIDTH), BF16),
        compiler_params=_params(("arbitrary", "arbitrary")),
        name="glb_attn",
    )(q, kt, v)


def _swa_kernel(sink_ref, q_ref, *refs, nt):
    nkb = SWA_TQ // SWA_WINDOW + 2
    kt_refs, ktx_ref = refs[:nkb], refs[nkb]
    v_refs, vx_ref, o_ref = refs[nkb + 1:2 * nkb + 1], refs[2 * nkb + 1], refs[2 * nkb + 2]
    n = pl.program_id(0)
    w = SWA_WINDOW
    tq = SWA_TQ
    rows = GQA_GROUP * tq
    qi = lax.broadcasted_iota(jnp.int32, (rows, nkb * w), 0) % tq
    kj = lax.broadcasted_iota(jnp.int32, (rows, nkb * w), 1)
    ok = (kj >= qi) & (kj <= qi + 2 * w)
    ok = ok & jnp.logical_or(n > 0, kj >= w) & jnp.logical_or(n < nt - 1, kj < (nkb - 1) * w)
    outs = []
    for kv in range(ATT_KV_HEADS):
        q = q_ref[kv * GQA_GROUP:(kv + 1) * GQA_GROUP].reshape(rows, HEAD_DIM)
        kt = jnp.concatenate([r[kv] for r in kt_refs], axis=1)
        s = jnp.concatenate([jnp.where(ok, _dot(q, kt), NEG_INF), _dot(q, ktx_ref[kv])], axis=1)
        sink = jnp.concatenate([jnp.full((tq, 1), sink_ref[kv * GQA_GROUP + h], F32) for h in range(GQA_GROUP)], axis=0)
        m = jnp.maximum(jnp.max(s, axis=1, keepdims=True), sink)
        v = jnp.concatenate([r[kv] for r in v_refs] + [vx_ref[kv]], axis=0)
        acc = _dot(jnp.exp(s - m).astype(BF16), v)
        lane = lax.broadcasted_iota(jnp.int32, acc.shape, 1)
        acc = acc + jnp.where(lane == HEAD_DIM, jnp.exp(sink - m), 0.0)
        outs.append(_heads_out(acc, tq))
    o_ref[...] = jnp.concatenate(outs, axis=1).astype(BF16)


def _swa_attn(q, kt, v, sink, n_rows):
    s = q.shape[1]
    w = SWA_WINDOW
    tq = SWA_TQ
    nb = (s - CTX_LEN) // w
    nt = (s - CTX_LEN) // tq
    nkb = tq // w + 2
    c0 = CTX_LEN // w
    blk = lambda b: (lambda n: jnp.clip(n * (tq // w) - 1 + b, 0, nb - 1) + c0)
    kt_spec = lambda f: pl.BlockSpec((ATT_KV_HEADS, HEAD_DIM, w), lambda n: (0, 0, f(n)))
    v_spec = lambda f: pl.BlockSpec((ATT_KV_HEADS, w, LANES), lambda n: (0, f(n), 0))
    return pl.pallas_call(
        functools.partial(_swa_kernel, nt=nt),
        grid=(nt,),
        in_specs=[pl.BlockSpec(memory_space=pltpu.SMEM),
                  pl.BlockSpec((ATT_Q_HEADS, tq, HEAD_DIM), lambda n: (0, n + CTX_LEN // tq, 0))]
                 + [kt_spec(blk(b)) for b in range(nkb)]
                 + [pl.BlockSpec((ATT_KV_HEADS, HEAD_DIM, CTX_LEN), lambda n: (0, 0, 0))]
                 + [v_spec(blk(b)) for b in range(nkb)]
                 + [pl.BlockSpec((ATT_KV_HEADS, CTX_LEN, LANES), lambda n: (0, 0, 0))],
        out_specs=pl.BlockSpec((tq, ATT_WIDTH), lambda n: (n + TM // tq, 0)),
        out_shape=jax.ShapeDtypeStruct((n_rows, ATT_WIDTH), BF16),
        compiler_params=_params(("parallel",)),
        name="swa_attn",
    )(sink, q, *([kt] * (nkb + 1)), *([v] * (nkb + 1)))


def _ctx_attn_kernel(sink_ref, q_ref, kt_ref, v_ref, prev_ref, o_ref, *, use_sink):
    del prev_ref
    rows = GQA_GROUP * CTX_LEN
    outs = []
    for kv in range(ATT_KV_HEADS):
        q = q_ref[kv * GQA_GROUP:(kv + 1) * GQA_GROUP].reshape(rows, HEAD_DIM)
        s = _dot(q, kt_ref[kv])
        m = jnp.max(s, axis=1, keepdims=True)
        if use_sink:
            sink = jnp.concatenate([jnp.full((CTX_LEN, 1), sink_ref[kv * GQA_GROUP + h], F32)
                                    for h in range(GQA_GROUP)], axis=0)
            m = jnp.maximum(m, sink)
        acc = _dot(jnp.exp(s - m).astype(BF16), v_ref[kv])
        if use_sink:
            lane = lax.broadcasted_iota(jnp.int32, acc.shape, 1)
            acc = acc + jnp.where(lane == HEAD_DIM, jnp.exp(sink - m), 0.0)
        outs.append(_heads_out(acc, CTX_LEN))
    o_ref[:CTX_LEN] = jnp.concatenate(outs, axis=1).astype(BF16)
    o_ref[CTX_LEN:] = jnp.zeros((TM - CTX_LEN, ATT_WIDTH), BF16)


def _ctx_attn(q, kt, v, sink, y_prev, use_sink):
    return pl.pallas_call(
        functools.partial(_ctx_attn_kernel, use_sink=use_sink),
        grid=(1,),
        in_specs=[pl.BlockSpec(memory_space=pltpu.SMEM),
                  pl.BlockSpec((ATT_Q_HEADS, CTX_LEN, HEAD_DIM), lambda i: (0, 0, 0)),
                  pl.BlockSpec((ATT_KV_HEADS, HEAD_DIM, CTX_LEN), lambda i: (0, 0, 0)),
                  pl.BlockSpec((ATT_KV_HEADS, CTX_LEN, LANES), lambda i: (0, 0, 0)),
                  pl.BlockSpec(memory_space=pl.ANY)],
        out_specs=pl.BlockSpec((TM, ATT_WIDTH), lambda i: (0, 0)),
        out_shape=jax.ShapeDtypeStruct(y_prev.shape, y_prev.dtype),
        input_output_aliases={4: 0},
        compiler_params=_params(("arbitrary",)),
        name="ctx_attn",
    )(sink, q, kt, v, y_prev)


def _ssd_prep_kernel(xp_ref, xc_ref, xn_ref, dt_ref, w_ref, b_ref, dtb_ref, xo_ref, dto_ref, *, nblk):
    b = pl.program_id(0)
    t = SSD_T
    lat0 = TM // t
    has_prev = jnp.logical_and(b != 0, b != lat0)
    has_next = jnp.logical_and(b != CTX_LEN // t - 1, b != nblk - 1)
    cur = xc_ref[...]
    prev = jnp.where(has_prev, xp_ref[...], 0.0)
    nxt = jnp.where(has_next, xn_ref[...], 0.0)
    row = lax.broadcasted_iota(jnp.int32, cur.shape, 0)
    w = w_ref[...]
    acc = cur * w[2:3, :] + b_ref[...]
    for k in (1, 2):
        down = jnp.where(row < k, pltpu.roll(prev, k, 0), pltpu.roll(cur, k, 0))
        up = jnp.where(row >= t - k, pltpu.roll(nxt, t - k, 0), pltpu.roll(cur, t - k, 0))
        acc = acc + down * w[2 - k:3 - k, :] + up * w[2 + k:3 + k, :]
    xo_ref[...] = _silu(acc)
    d = dt_ref[...] + dtb_ref[...]
    dto_ref[...] = jnp.maximum(d, 0.0) + jnp.log1p(jnp.exp(-jnp.abs(d)))


def _ssd_prep(xbc, dt, conv_w, conv_b, dt_bias):
    n = xbc.shape[0]
    t = SSD_T
    nblk = n // t
    return pl.pallas_call(
        functools.partial(_ssd_prep_kernel, nblk=nblk),
        grid=(nblk,),
        in_specs=[pl.BlockSpec((t, SSD_XBC), lambda b: (jnp.maximum(b - 1, 0), 0)),
                  pl.BlockSpec((t, SSD_XBC), lambda b: (b, 0)),
                  pl.BlockSpec((t, SSD_XBC), lambda b: (jnp.minimum(b + 1, nblk - 1), 0)),
                  pl.BlockSpec((t, DT_PAD), lambda b: (b, 0)),
                  pl.BlockSpec((8, SSD_XBC), lambda b: (0, 0)),
                  pl.BlockSpec((1, SSD_XBC), lambda b: (0, 0)),
                  pl.BlockSpec((1, DT_PAD), lambda b: (0, 0))],
        out_specs=[pl.BlockSpec((t, SSD_XBC), lambda b: (b, 0)),
                   pl.BlockSpec((t, DT_PAD), lambda b: (b, 0))],
        out_shape=[jax.ShapeDtypeStruct((n, SSD_XBC), F32), jax.ShapeDtypeStruct((n, DT_PAD), F32)],
        compiler_params=_params(("parallel",)),
        name="ssd_prep",
    )(xbc, xbc, xbc, dt, conv_w, conv_b, dt_bias)


def _ssd_scan_kernel(tbl_ref, x_ref, dt_ref, arow_ref, tri_ref, e_ref, *rest, direction, final):
    if final:
        yf_ref, z_ref, dsk_ref, ng_ref, o_ref, st_ref = rest
    else:
        o_ref, st_ref = rest
    step = pl.program_id(0)
    flag = tbl_ref[1, step]
    t = SSD_T
    gw = SSD_INNER // SSD_GROUPS
    hpg = SSD_HEADS // SSD_GROUPS

    @pl.when(flag == 2)
    def _():
        o_ref[...] = jnp.zeros_like(o_ref)

    @pl.when(flag == 1)
    def _():
        st_ref[...] = jnp.zeros_like(st_ref)

    @pl.when(flag != 2)
    def _():
        x = x_ref[...]
        xs = x[:, :SSD_INNER]
        bm = x[:, SSD_INNER:SSD_INNER + SSD_BC]
        cm = x[:, SSD_INNER + SSD_BC:]
        dt = dt_ref[...]
        adt = dt * arow_ref[...]
        tri = tri_ref[...]
        keep = tri > 0
        cum = _split_rhs_dot(tri, adt, 3)
        tot = _split_rhs_dot(jnp.ones((t, t), BF16), adt, 3)
        e = e_ref[...]
        dt_e = _split_lhs_dot(dt, e, 2)
        expc_e = _split_lhs_dot(jnp.exp(cum), e, 2)
        decs_e = _split_lhs_dot(jnp.exp(tot - cum), e, 2)
        dch_e = _split_lhs_dot(jnp.exp(tot[:8]), e, 2)[0:1]
        cum_t = cum.T
        xdt = xs * dt_e
        ys = []
        for g in range(SSD_GROUPS):
            bg = bm[:, g * SSD_STATE:(g + 1) * SSD_STATE]
            cgb = cm[:, g * SSD_STATE:(g + 1) * SSD_STATE].astype(BF16)
            cb = _dot_nt(cgb, bg.astype(BF16))
            stg = st_ref[g]
            y_off = _dot(cgb, stg.astype(BF16)) * expc_e[:, g * gw:(g + 1) * gw]
            parts = []
            for j in range(hpg):
                h = g * hpg + j
                r = direction * SSD_HEADS + h
                diff = cum[:, r:r + 1] - cum_t[r:r + 1, :]
                lmat = jnp.exp(jnp.where(keep, diff, NEG_INF))
                wmat = (cb * lmat).astype(BF16)
                parts.append(_dot(wmat, xdt[:, h * SSD_HEAD_DIM:(h + 1) * SSD_HEAD_DIM].astype(BF16)))
            ys.append(jnp.concatenate(parts, axis=1) + y_off)
            xw = (xdt[:, g * gw:(g + 1) * gw] * decs_e[:, g * gw:(g + 1) * gw]).astype(BF16)
            st_ref[g] = stg * dch_e[:, g * gw:(g + 1) * gw] + _dot(bg.T.astype(BF16), xw)
        y = jnp.concatenate(ys, axis=1)
        if final:
            y = yf_ref[...] + y + dsk_ref[...] * xs
            y = y * _silu(z_ref[...])
            ms = jnp.mean(y * y, axis=-1, keepdims=True)
            o_ref[...] = (y * lax.rsqrt(ms + EPS) * ng_ref[...]).astype(BF16)
        else:
            o_ref[...] = y


def _ssd_scan(tbl, xact, dtsp, arow, tri, e, direction, extra=None):
    n = xact.shape[0]
    t = SSD_T
    final = extra is not None
    blk = lambda w: pl.BlockSpec((t, w), lambda s, tb: (tb[0, s], 0))
    const = lambda r, w: pl.BlockSpec((r, w), lambda s, tb: (0, 0))
    in_specs = [blk(SSD_XBC), blk(DT_PAD), const(1, DT_PAD), const(t, t), const(DT_PAD, SSD_INNER)]
    args = [xact, dtsp, arow, tri, e]
    if final:
        in_specs += [blk(SSD_INNER), blk(SSD_INNER), const(1, SSD_INNER), const(1, SSD_INNER)]
        args += list(extra)
    return pl.pallas_call(
        functools.partial(_ssd_scan_kernel, direction=direction, final=final),
        grid_spec=pltpu.PrefetchScalarGridSpec(
            num_scalar_prefetch=1, grid=(tbl.shape[1],), in_specs=in_specs,
            out_specs=blk(SSD_INNER),
            scratch_shapes=[pltpu.VMEM((SSD_GROUPS, SSD_STATE, SSD_INNER // SSD_GROUPS), F32)]),
        out_shape=jax.ShapeDtypeStruct((n, SSD_INNER), BF16 if final else F32),
        compiler_params=_params(("arbitrary",)),
        name="ssd_scan_bwd" if final else "ssd_scan_fwd",
    )(tbl, *args)


def _ssd_tables(n_rows):
    t = SSD_T
    ctx = list(range(CTX_LEN // t))
    pad = list(range(CTX_LEN // t, TM // t))
    lat = list(range(TM // t, n_rows // t))
    fwd = ctx + lat + pad
    bwd = ctx[::-1] + lat[::-1] + pad
    flags = [1] + [0] * (len(ctx) + len(lat) - 1) + [2] * len(pad)
    return (jnp.asarray(np.array([fwd, flags], np.int32)), jnp.asarray(np.array([bwd, flags], np.int32)))


def _dot_nt_split(a, b):
    a_hi = a.astype(BF16)
    b_hi = b.astype(BF16)
    a_lo = (a - a_hi.astype(F32)).astype(BF16)
    b_lo = (b - b_hi.astype(F32)).astype(BF16)
    return _dot_nt(a_hi, b_hi) + _dot_nt(a_hi, b_lo) + _dot_nt(a_lo, b_hi)


def _s5_setup_kernel(pa_ref, pb_ref, pc_ref, tile_ref, m_ref, bs_ref, csa_ref, csb_ref, lre_ref, lim_ref):
    t_len = S5_T
    hh = S5_GROUP
    a_re = pa_ref[0:1, :]
    a_im = pa_ref[1:2, :]
    dt = jnp.exp(pa_ref[2:3, :])
    mag = jnp.exp(a_re * dt)
    lr = mag * jnp.cos(a_im * dt)
    li = mag * jnp.sin(a_im * dt)
    den = a_re * a_re + a_im * a_im
    nr = lr - 1.0
    f_re = (nr * a_re + li * a_im) / den
    f_im = (li * a_re - nr * a_im) / den
    b_re, b_im = pb_ref[0:hh], pb_ref[hh:2 * hh]
    c_re, c_im = pc_ref[0:hh], pc_ref[hh:2 * hh]
    bb_re = f_re * b_re - f_im * b_im
    bb_im = f_re * b_im + f_im * b_re
    pw = [(jnp.ones_like(lr), jnp.zeros_like(lr))]
    for _ in range(t_len):
        pr, pi = pw[-1]
        pw.append((pr * lr - pi * li, pr * li + pi * lr))
    fwd = lax.broadcasted_iota(jnp.int32, lr.shape, 1) < S5_STATE
    pick = lambda kf, kb: (jnp.where(fwd, pw[kf][0], pw[kb][0]), jnp.where(fwd, pw[kf][1], pw[kb][1]))
    bs_re, bs_im, cs_re, cs_im = [], [], [], []
    for t in range(t_len):
        er, ei = pick(t_len - 1 - t, t)
        bs_re.append(er * bb_re - ei * bb_im)
        bs_im.append(er * bb_im + ei * bb_re)
        fr, fi = pick(t + 1, t_len - t)
        cs_re.append(c_re * fr - c_im * fi)
        cs_im.append(c_re * fi + c_im * fr)
    bs_full = jnp.concatenate([jnp.concatenate(bs_re, axis=0), jnp.concatenate(bs_im, axis=0)], axis=1)
    cs_full = jnp.concatenate([jnp.concatenate(cs_re, axis=0), -jnp.concatenate(cs_im, axis=0)], axis=1)
    wide = bs_full.shape
    fwd_w = (lax.broadcasted_iota(jnp.int32, wide, 1) % (2 * S5_STATE)) < S5_STATE
    bs_ref[...] = bs_full.astype(BF16)
    csa_ref[...] = jnp.where(fwd_w, cs_full, 0.0).astype(BF16)
    csb_ref[...] = jnp.where(fwd_w, 0.0, cs_full).astype(BF16)
    lre_ref[...] = pw[t_len][0]
    lim_ref[...] = pw[t_len][1]
    cc = jnp.concatenate([c_re, -c_im], axis=1)
    cc = jnp.concatenate([cc, jnp.zeros((LANES - hh, wide[1]), F32)], axis=0)
    fwd_c = (lax.broadcasted_iota(jnp.int32, cc.shape, 1) % (2 * S5_STATE)) < S5_STATE
    kr_f = _dot_nt_split(bs_full, jnp.where(fwd_c, cc, 0.0))
    k_b = _dot_nt_split(bs_full, jnp.where(fwd_c, 0.0, cc))
    tile = tile_ref[...]
    kr_w = _split_lhs_dot(kr_f, tile, 3)
    kb_w = _split_lhs_dot(k_b, tile, 3)
    lane_blk = lax.broadcasted_iota(jnp.int32, wide, 1) // hh
    m = jnp.zeros(wide, F32)
    for t in range(t_len):
        up, dn = kr_w, kb_w
        if t < t_len - 1:
            r = (t_len - 1 - t) * hh
            up = jnp.concatenate([kr_w[r:], jnp.zeros((r, wide[1]), F32)], axis=0)
        if t > 0:
            r = t * hh
            dn = jnp.concatenate([jnp.zeros((r, wide[1]), F32), kb_w[:wide[0] - r]], axis=0)
        m = m + jnp.where(lane_blk == t, up + dn, 0.0)
    m_ref[...] = m.astype(BF16)


def _s5_setup(a_re, a_im, log_dt, b_re, b_im, c_re, c_im):
    g = a_re.shape[1]
    tw = S5_T * S5_GROUP
    both = lambda v: jnp.concatenate([v[0], v[1]], axis=-1)
    ldt = jnp.broadcast_to(log_dt[:, :, None], (2, g, S5_STATE))
    pa = jnp.stack([both(a_re), both(a_im), both(ldt)], axis=1).astype(F32)
    pa = jnp.pad(pa, ((0, 0), (0, 5), (0, 0)))
    dup = lambda v: jnp.concatenate([v, v], axis=-1)
    pb = jnp.concatenate([dup(jnp.swapaxes(b_re, 1, 2)), dup(jnp.swapaxes(b_im, 1, 2))], axis=1).astype(F32)
    pc = jnp.concatenate([dup(c_re), dup(c_im)], axis=1).astype(F32)
    tile = jnp.asarray(np.tile(np.eye(LANES, S5_GROUP).astype(np.float32), (1, S5_T)), BF16)
    per_g = lambda r, c: pl.BlockSpec((None, r, c), lambda i: (i, 0, 0))
    mat = jax.ShapeDtypeStruct((g, tw, tw), BF16)
    lam = jax.ShapeDtypeStruct((g, 1, 2 * S5_STATE), F32)
    m, bs, csa, csb, lre, lim = pl.pallas_call(
        _s5_setup_kernel,
        grid=(g,),
        in_specs=[per_g(8, 2 * S5_STATE), per_g(2 * S5_GROUP, 2 * S5_STATE), per_g(2 * S5_GROUP, 2 * S5_STATE),
                  pl.BlockSpec((LANES, tw), lambda i: (0, 0))],
        out_specs=[per_g(tw, tw)] * 4 + [per_g(1, 2 * S5_STATE)] * 2,
        out_shape=[mat] * 4 + [lam] * 2,
        compiler_params=_params(("parallel",)),
        name="s5_setup",
    )(pa, pb, pc, tile)
    return m, bs, csa, csb, lre[:, 0], lim[:, 0]


def _s5_kernel(u_ref, m_ref, bs_ref, csa_ref, csb_ref, lre_ref, lim_ref, y_ref, v_ref, sa_ref, sb_ref, *, nctx, nch):
    gb = S5_GB
    for j in range(gb):
        v = _dot(u_ref[j], bs_ref[j])
        v_ref.at[0][pl.ds(j, nch, stride=gb), :] = v[:, :LANES]
        v_ref.at[1][pl.ds(j, nch, stride=gb), :] = v[:, LANES:]
    lre = lre_ref[0]
    lim = lim_ref[0]
    is_fwd = lax.broadcasted_iota(jnp.int32, (gb, LANES), 1) < S5_STATE

    def body(i, carry):
        sre, sim = carry
        rf = i * gb
        rb = jnp.where(i < nctx, nctx - 1 - i, nch + nctx - 1 - i) * gb
        sa_ref[0, pl.ds(rf, gb), :] = sre
        sa_ref[1, pl.ds(rf, gb), :] = sim
        sb_ref[0, pl.ds(rb, gb), :] = sre
        sb_ref[1, pl.ds(rb, gb), :] = sim
        vre = jnp.where(is_fwd, v_ref[0, pl.ds(rf, gb), :], v_ref[0, pl.ds(rb, gb), :])
        vim = jnp.where(is_fwd, v_ref[1, pl.ds(rf, gb), :], v_ref[1, pl.ds(rb, gb), :])
        return lre * sre - lim * sim + vre, lre * sim + lim * sre + vim

    zero = jnp.zeros((gb, LANES), F32)
    lax.fori_loop(0, nch, body, (zero, zero))
    for j in range(gb):
        rows = pl.ds(j, nch, stride=gb)
        sa = jnp.concatenate([sa_ref.at[0][rows, :], sa_ref.at[1][rows, :]], axis=1).astype(BF16)
        sb = jnp.concatenate([sb_ref.at[0][rows, :], sb_ref.at[1][rows, :]], axis=1).astype(BF16)
        y_ref[j] = _dot(u_ref[j], m_ref[j]) + _dot_nt(sa, csa_ref[j]) + _dot_nt(sb, csb_ref[j])


def _s5_scan(ug, m, bs, csa, csb, lam_re, lam_im):
    g, nch, tw = ug.shape
    gb = S5_GB
    sw = 4 * S5_STATE
    blk = lambda a, b: pl.BlockSpec((gb, a, b), lambda i: (i, 0, 0))
    lam_spec = pl.BlockSpec((1, gb, 2 * S5_STATE), lambda i: (i, 0, 0))
    return pl.pallas_call(
        functools.partial(_s5_kernel, nctx=CTX_LEN // S5_T, nch=nch),
        grid=(g // gb,),
        in_specs=[blk(nch, tw), blk(tw, tw), blk(tw, sw), blk(tw, sw), blk(tw, sw), lam_spec, lam_spec],
        out_specs=blk(nch, tw),
        out_shape=jax.ShapeDtypeStruct((g, nch, tw), F32),
        scratch_shapes=[pltpu.VMEM((2, nch * gb, LANES), F32) for _ in range(3)],
        compiler_params=_params(("parallel",)),
        name="s5_scan",
    )(ug, m, bs, csa, csb, lam_re.reshape(g // gb, gb, -1), lam_im.reshape(g // gb, gb, -1))


def _s5_glu_kernel(y_ref, u_ref, d_ref, w_ref, b_ref, o_ref):
    y = jax.nn.gelu(y_ref[...] + d_ref[...] * u_ref[...])
    t = _dot(y.astype(BF16), w_ref[...]) + b_ref[...]
    o_ref[...] = (t[:, :S5_WIDTH] * jax.nn.sigmoid(t[:, S5_WIDTH:])).astype(BF16)


def _s5_glu(y, u, d, w, b):
    n = y.shape[0]
    row = lambda wd: pl.BlockSpec((TM, wd), lambda i: (i, 0))
    return pl.pallas_call(
        _s5_glu_kernel,
        grid=(n // TM,),
        in_specs=[row(S5_WIDTH), row(S5_WIDTH), pl.BlockSpec((1, S5_WIDTH), lambda i: (0, 0)),
                  pl.BlockSpec((S5_WIDTH, 2 * S5_WIDTH), lambda i: (0, 0)),
                  pl.BlockSpec((1, 2 * S5_WIDTH), lambda i: (0, 0))],
        out_specs=row(S5_WIDTH),
        out_shape=jax.ShapeDtypeStruct((n, S5_WIDTH), BF16),
        compiler_params=_params(("parallel",)),
        name="s5_glu",
    )(y, u, d, w, b)


BR_WIDTHS = (SSD_INNER, S5_WIDTH, ATT_WIDTH, ATT_WIDTH)


def _merge_kernel(h_ref, *refs):
    y_refs, wg_refs, wb_refs, o_ref = refs[:4], refs[4:8], refs[8:12], refs[12]
    h = h_ref[...]
    acc = None
    for y_ref, wg_ref, wb_ref in zip(y_refs, wg_refs, wb_refs):
        t = jax.nn.sigmoid(_dot(h, wg_ref[...])) * _dot(y_ref[...], wb_ref[...])
        acc = t if acc is None else acc + t
    o_ref[...] = acc.astype(BF16)


def _merge(hb, ys, w_gate, w_brs, tn=512):
    n = hb.shape[0]
    nj = D_MODEL // tn
    row = lambda w: pl.BlockSpec((TM, w), lambda i, j: (i, 0))
    return pl.pallas_call(
        _merge_kernel,
        grid=(n // TM, nj),
        in_specs=[row(D_MODEL)] + [row(w) for w in BR_WIDTHS]
                 + [pl.BlockSpec((D_MODEL, tn), lambda i, j, b=b: (0, b * nj + j)) for b in range(N_BRANCH)]
                 + [pl.BlockSpec((w, tn), lambda i, j: (0, j)) for w in BR_WIDTHS],
        out_specs=pl.BlockSpec((TM, tn), lambda i, j: (i, j)),
        out_shape=jax.ShapeDtypeStruct((n, D_MODEL), BF16),
        compiler_params=_params(("parallel", "arbitrary")),
        name="merge",
    )(hb, *ys, w_gate, w_gate, w_gate, w_gate, *w_brs)


def _out_kernel(x_ref, a_ref, mod_ref, w_ref, o_ref):
    o_ref[...] = x_ref[...] + mod_ref[5:6, :] * _dot(a_ref[...], w_ref[...])


def _out_proj(x, acc, mods_l, w_out):
    n = x.shape[0]
    row = lambda: pl.BlockSpec((TM, D_MODEL), lambda i: (i, 0))
    return pl.pallas_call(
        _out_kernel,
        grid=(n // TM,),
        in_specs=[row(), row(), _mod_spec(),
                  pl.BlockSpec((D_MODEL, D_MODEL), lambda i: (0, 0), pipeline_mode=pl.Buffered(1))],
        out_specs=row(),
        out_shape=jax.ShapeDtypeStruct((n, D_MODEL), F32),
        compiler_params=_params(("parallel",)),
        name="out_proj",
    )(x, acc, mods_l, w_out)


def _rope_tables(seq):
    rows = seq // GRID_W
    row = jnp.repeat(jnp.arange(rows, dtype=F32), GRID_W)
    col = jnp.tile(jnp.arange(GRID_W, dtype=F32), rows)
    n_freq = HEAD_DIM // 4
    inv = ROPE_THETA ** (-jnp.arange(n_freq, dtype=F32) / n_freq)
    ang = jnp.concatenate([row[:, None] * inv, col[:, None] * inv], axis=-1)
    cos, sin = jnp.cos(ang), jnp.sin(ang)
    reps = LANES // HEAD_DIM
    cos_t = jnp.tile(jnp.concatenate([cos, cos], axis=-1), (1, reps))
    sin_t = jnp.tile(jnp.concatenate([-sin, sin], axis=-1), (1, reps))
    cos_t = jnp.concatenate([jnp.ones((CTX_LEN, LANES), F32), cos_t], axis=0)
    sin_t = jnp.concatenate([jnp.zeros((CTX_LEN, LANES), F32), sin_t], axis=0)
    return cos_t, sin_t


def _ssd_expand(direction):
    e = np.zeros((DT_PAD, SSD_INNER), np.float32)
    for h in range(SSD_HEADS):
        e[direction * SSD_HEADS + h, h * SSD_HEAD_DIM:(h + 1) * SSD_HEAD_DIM] = 1.0
    return jnp.asarray(e, BF16)


def _to_stream(ctx_rows, lat_rows):
    pad = jnp.zeros((TM - CTX_LEN,) + ctx_rows.shape[1:], ctx_rows.dtype)
    return jnp.concatenate([ctx_rows, pad, lat_rows], axis=0)


def kernel(x, c, ctx, c_ctx, w_mod, b_mod, norm_g, ffn_in, ffn_out, w_in, ssd_conv_w, ssd_conv_b, ssd_a_log, ssd_dt_bias, ssd_d, ssd_norm_g, s5_a_re, s5_a_im, s5_log_dt, s5_b_re, s5_b_im, s5_c_re, s5_c_im, s5_d, s5_glu_w, s5_glu_b, swa_qk_g, swa_sink, glb_qk_g, w_br_ssd, w_br_s5, w_br_swa, w_br_glb, w_out):
    assert x.shape[0] == 1 and ctx.shape[1] == CTX_LEN
    depth = w_mod.shape[0]
    seq = x.shape[1]
    n_rows = TM + seq
    s_len = CTX_LEN + seq

    ffn_in_b = ffn_in.astype(BF16)
    ffn_out_b = ffn_out.astype(BF16)
    offs = np.cumsum((0, SSD_INNER, SSD_XBC, 2 * SSD_HEADS, S5_WIDTH, ATT_WIDTH, ATT_KV_WIDTH, ATT_KV_WIDTH,
                      ATT_WIDTH, ATT_KV_WIDTH, ATT_KV_WIDTH))
    seg = lambda a, b: w_in[:, :, offs[a]:offs[b]].astype(BF16)
    w_dt = jnp.pad(seg(2, 3), ((0, 0), (0, 0), (0, DT_PAD - 2 * SSD_HEADS)))
    w_proj = (seg(0, 1), seg(1, 2), w_dt, seg(3, 4), seg(4, 7), seg(7, 10))
    w_gate = w_in[:, :, offs[10]:].astype(BF16)
    w_brs = tuple(w.astype(BF16) for w in (w_br_ssd, w_br_s5, w_br_swa, w_br_glb))
    w_out_b = w_out.astype(BF16)
    glu_w_b = s5_glu_w.astype(BF16)

    mods = _mods(c, c_ctx, w_mod, b_mod)
    cos_t, sin_t = _rope_tables(seq)
    bd = jnp.asarray(np.kron(np.eye(QK_W // HEAD_DIM), np.ones((HEAD_DIM, HEAD_DIM))) / HEAD_DIM, BF16)
    tbl_f, tbl_b = _ssd_tables(n_rows)
    tril = jnp.asarray(np.tril(np.ones((SSD_T, SSD_T), np.float32)), BF16)
    triu = jnp.asarray(np.triu(np.ones((SSD_T, SSD_T), np.float32)), BF16)
    e_dirs = (_ssd_expand(0), _ssd_expand(1))
    nch = s_len // S5_T
    nk = s_len // GLB_TK

    xs = _to_stream(ctx[0], x[0])
    for i in range(depth):
        ml = mods[i]
        xs = _ffn(xs, ml, norm_g[i, 0:1], ffn_in_b[i, 0], ffn_out_b[i, 0], 0)
        hb, z, xbc, dt, u, qkv_swa, qkv_glb = _proj(xs, ml, norm_g[i, 1:2], [w[i] for w in w_proj])

        conv_w = jnp.pad(ssd_conv_w[i], ((0, 8 - SSD_CONV), (0, 0)))
        dt_bias = jnp.pad(ssd_dt_bias[i].reshape(1, -1), ((0, 0), (0, DT_PAD - 2 * SSD_HEADS)))
        xact, dtsp = _ssd_prep(xbc, dt, conv_w, ssd_conv_b[i].reshape(1, -1), dt_bias)
        a_neg = -jnp.exp(ssd_a_log[i].astype(F32))
        arow = lambda d: jnp.zeros((1, DT_PAD), F32).at[0, d * SSD_HEADS:(d + 1) * SSD_HEADS].set(a_neg[d])
        y_f = _ssd_scan(tbl_f, xact, dtsp, arow(0), tril, e_dirs[0], 0)
        d_exp = jnp.repeat(ssd_d[i].astype(F32), SSD_HEAD_DIM).reshape(1, -1)
        y_ssd = _ssd_scan(tbl_b, xact, dtsp, arow(1), triu, e_dirs[1], 1,
                          extra=(y_f, z, d_exp, ssd_norm_g[i].reshape(1, -1)))

        s5m, s5bs, s5csa, s5csb, lam_re, lam_im = _s5_setup(
            s5_a_re[i], s5_a_im[i], s5_log_dt[i], s5_b_re[i], s5_b_im[i], s5_c_re[i], s5_c_im[i])
        u_c = jnp.concatenate([u[:CTX_LEN], u[TM:]], axis=0)
        ug = u_c.reshape(nch, S5_T, S5_GROUPS, S5_GROUP).transpose(2, 0, 1, 3).reshape(S5_GROUPS, nch, -1)
        yg = _s5_scan(ug.astype(BF16), s5m, s5bs, s5csa, s5csb, lam_re, lam_im)
        y_c = yg.reshape(S5_GROUPS, nch, S5_T, S5_GROUP).transpose(1, 2, 0, 3).reshape(s_len, S5_WIDTH)
        y_s5 = _s5_glu(_to_stream(y_c[:CTX_LEN], y_c[CTX_LEN:]), u, s5_d[i].reshape(1, -1), glu_w_b[i],
                       s5_glu_b[i].reshape(1, -1))

        g_swa = jnp.concatenate([jnp.tile(swa_qk_g[i, 0], ATT_Q_HEADS), jnp.tile(swa_qk_g[i, 1], ATT_KV_HEADS)])[None]
        q, kt, v = _attn_prep(qkv_swa, g_swa, cos_t, sin_t, bd)
        y_swa = _swa_attn(q, kt, v, swa_sink[i], n_rows)
        y_swa = _ctx_attn(q, kt, v, swa_sink[i], y_swa, True)

        g_glb = jnp.concatenate([jnp.tile(glb_qk_g[i, 0], ATT_Q_HEADS), jnp.tile(glb_qk_g[i, 1], ATT_KV_HEADS)])[None]
        q, kt, v = _attn_prep(qkv_glb, g_glb, cos_t, sin_t, bd)
        kt_c = kt.reshape(ATT_KV_HEADS, HEAD_DIM, nk, GLB_TK).transpose(0, 2, 1, 3)
        y_glb = _glb_attn(q, kt_c, v.reshape(ATT_KV_HEADS, nk, GLB_TK, LANES), n_rows)
        y_glb = _ctx_attn(q, kt, v, swa_sink[i], y_glb, False)

        acc = _merge(hb, (y_ssd, y_s5, y_swa, y_glb), w_gate[i], [w[i] for w in w_brs])
        xs = _out_proj(xs, acc, ml, w_out_b[i])
        xs = _ffn(xs, ml, norm_g[i, 2:3], ffn_in_b[i, 1], ffn_out_b[i, 1], 2)
    return xs[TM:][None]
```

```python
import functools
import math

import numpy as np
import jax
import jax.numpy as jnp
from jax import lax
from jax.experimental import pallas as pl
from jax.experimental.pallas import tpu as pltpu

F32 = jnp.float32
BF16 = jnp.bfloat16

D_MODEL = 2048
GRID_W = 64
CTX_LEN = 256
N_MOD = 9
D_FF = 5632
SSD_HEADS = 12
SSD_HEAD_DIM = 64
SSD_INNER = SSD_HEADS * SSD_HEAD_DIM
SSD_GROUPS = 2
SSD_STATE = 128
SSD_BC = SSD_GROUPS * SSD_STATE
SSD_XBC = SSD_INNER + 2 * SSD_BC
SSD_CONV = 5
S5_WIDTH = 512
S5_GROUP = 16
S5_GROUPS = S5_WIDTH // S5_GROUP
S5_STATE = 64
HEAD_DIM = 64
ATT_Q_HEADS = 8
ATT_KV_HEADS = 2
GQA_GROUP = ATT_Q_HEADS // ATT_KV_HEADS
ATT_WIDTH = ATT_Q_HEADS * HEAD_DIM
ATT_KV_WIDTH = ATT_KV_HEADS * HEAD_DIM
ATT_SCALE = HEAD_DIM ** -0.5
SWA_WINDOW = 128
ROPE_THETA = 10000.0
NEG_INF = -1e30
N_BRANCH = 4
EPS = 1e-6

LANES = 128
VMEM_LIMIT = 56 * 1024 * 1024

TM = 512
SSD_T = 128
S5_T = 16
S5_GB = 4
PROJ_TM = 256
GLB_TQ = 256
SWA_TQ = 256
GLB_TK = 1280
DT_PAD = LANES


def _params(sem, vmem=VMEM_LIMIT):
    return pltpu.CompilerParams(dimension_semantics=sem, vmem_limit_bytes=vmem)


def _silu(x):
    return x * jax.nn.sigmoid(x)


def _norm_mod(x, g, shift, scale):
    ms = jnp.mean(x * x, axis=-1, keepdims=True)
    return (x * lax.rsqrt(ms + EPS)) * (g * (1.0 + scale)) + shift


def _dot(a, b):
    return jnp.dot(a, b, preferred_element_type=F32)


def _dot_nt(a, b):
    return lax.dot_general(a, b, (((1,), (1,)), ((), ())), preferred_element_type=F32)


def _split_rhs_dot(m, x, parts):
    acc = None
    r = x
    for _ in range(parts):
        hi = r.astype(BF16)
        t = _dot(m, hi)
        acc = t if acc is None else acc + t
        r = r - hi.astype(F32)
    return acc


def _split_lhs_dot(x, m, parts):
    acc = None
    r = x
    for _ in range(parts):
        hi = r.astype(BF16)
        t = _dot(hi, m)
        acc = t if acc is None else acc + t
        r = r - hi.astype(F32)
    return acc


def _mod_kernel(s_ref, w_ref, b_ref, o_ref):
    s = _silu(s_ref[...])
    o_ref[...] = _dot(s.astype(BF16), w_ref[...].astype(BF16)) + b_ref[...]


def _mods(c, c_ctx, w_mod, b_mod):
    depth = w_mod.shape[0]
    s = jnp.zeros((8, D_MODEL), F32).at[0].set(c_ctx).at[1].set(c[0])
    tn = 1024
    out = pl.pallas_call(
        _mod_kernel,
        grid=(depth, N_MOD * D_MODEL // tn),
        in_specs=[pl.BlockSpec((8, D_MODEL), lambda l, j: (0, 0)),
                  pl.BlockSpec((None, D_MODEL, tn), lambda l, j: (l, 0, j)),
                  pl.BlockSpec((None, 1, tn), lambda l, j: (l, 0, j))],
        out_specs=pl.BlockSpec((None, 8, tn), lambda l, j: (l, 0, j)),
        out_shape=jax.ShapeDtypeStruct((depth, 8, N_MOD * D_MODEL), F32),
        compiler_params=_params(("parallel", "parallel")),
        name="mods",
    )(s, w_mod, b_mod.reshape(depth, 1, N_MOD * D_MODEL))
    return out[:, :2].reshape(depth, 2, N_MOD, D_MODEL)


def _mod_spec():
    return pl.BlockSpec((None, N_MOD, D_MODEL), lambda i, *_: (jnp.minimum(i, 1), 0, 0))


def _ffn_kernel(x_ref, mod_ref, g_ref, wa_ref, wb_ref, wo_ref, o_ref, h_ref, act_ref, acc_ref, *, sub, nj):
    j = pl.program_id(1)

    def up():
        h = h_ref[...]
        return (_silu(_dot(h, wa_ref[...])) * _dot(h, wb_ref[...])).astype(BF16)

    @pl.when(j == 0)
    def _():
        h = _norm_mod(x_ref[...], g_ref[...], mod_ref[3 * sub:3 * sub + 1, :], mod_ref[3 * sub + 1:3 * sub + 2, :])
        h_ref[...] = h.astype(BF16)
        acc_ref[...] = jnp.zeros_like(acc_ref)
        act_ref[...] = up()

    @pl.when(jnp.logical_and(j > 0, j < nj))
    def _():
        down = _dot(act_ref[...], wo_ref[...])
        act_ref[...] = up()
        acc_ref[...] += down

    @pl.when(j == nj)
    def _():
        acc = acc_ref[...] + _dot(act_ref[...], wo_ref[...])
        o_ref[...] = x_ref[...] + (0.5 * mod_ref[3 * sub + 2:3 * sub + 3, :]) * acc


def _ffn(x, mods_l, g, w_in, w_out, sub, tf=512):
    n = x.shape[0]
    nj = D_FF // tf
    return pl.pallas_call(
        functools.partial(_ffn_kernel, sub=sub, nj=nj),
        grid=(n // TM, nj + 1),
        in_specs=[pl.BlockSpec((TM, D_MODEL), lambda i, j: (i, 0)),
                  _mod_spec(),
                  pl.BlockSpec((1, D_MODEL), lambda i, j: (0, 0)),
                  pl.BlockSpec((D_MODEL, tf), lambda i, j: (0, jnp.minimum(j, nj - 1))),
                  pl.BlockSpec((D_MODEL, tf), lambda i, j: (0, jnp.minimum(j, nj - 1) + nj)),
                  pl.BlockSpec((tf, D_MODEL), lambda i, j: (jnp.maximum(j - 1, 0), 0))],
        out_specs=pl.BlockSpec((TM, D_MODEL), lambda i, j: (i, 0)),
        out_shape=jax.ShapeDtypeStruct((n, D_MODEL), F32),
        scratch_shapes=[pltpu.VMEM((TM, D_MODEL), BF16), pltpu.VMEM((TM, tf), BF16),
                        pltpu.VMEM((TM, D_MODEL), F32)],
        compiler_params=_params(("parallel", "arbitrary")),
        name="ffn",
    )(x, mods_l, g, w_in, w_in, w_out)


PROJ_WIDTHS = (SSD_INNER, SSD_XBC, DT_PAD, S5_WIDTH, ATT_WIDTH + 2 * ATT_KV_WIDTH, ATT_WIDTH + 2 * ATT_KV_WIDTH)


def _proj_kernel(x_ref, mod_ref, g_ref, *refs):
    nw = len(PROJ_WIDTHS)
    w_refs, h_ref, o_refs = refs[:nw], refs[nw], refs[nw + 1:]
    h = _norm_mod(x_ref[...], g_ref[...], mod_ref[3:4, :], mod_ref[4:5, :]).astype(BF16)
    h_ref[...] = h
    for w_ref, o_ref in zip(w_refs, o_refs):
        o_ref[...] = _dot(h, w_ref[...])


def _proj(x, mods_l, g, ws):
    n = x.shape[0]
    tm = PROJ_TM
    row = lambda w: pl.BlockSpec((tm, w), lambda i: (i, 0))
    return pl.pallas_call(
        _proj_kernel,
        grid=(n // tm,),
        in_specs=[row(D_MODEL),
                  pl.BlockSpec((None, N_MOD, D_MODEL), lambda i: (jnp.minimum(i // (TM // tm), 1), 0, 0)),
                  pl.BlockSpec((1, D_MODEL), lambda i: (0, 0))]
                 + [pl.BlockSpec((D_MODEL, w), lambda i: (0, 0), pipeline_mode=pl.Buffered(1)) for w in PROJ_WIDTHS],
        out_specs=[row(D_MODEL)] + [row(w) for w in PROJ_WIDTHS],
        out_shape=[jax.ShapeDtypeStruct((n, D_MODEL), BF16)]
                  + [jax.ShapeDtypeStruct((n, w), F32) for w in PROJ_WIDTHS],
        compiler_params=_params(("parallel",)),
        name="proj",
    )(x, mods_l, g, *ws)


QK_W = ATT_WIDTH + ATT_KV_WIDTH


def _attn_prep_kernel(qkv_ref, g_ref, cos_ref, sin_ref, bd_ref, q_ref, kt_ref, v_ref):
    x = qkv_ref[...]
    qk = x[:, :QK_W]
    sq = qk * qk
    hi = sq.astype(BF16)
    lo = (sq - hi.astype(F32)).astype(BF16)
    bd = bd_ref[...]
    ms = jnp.concatenate([_dot(hi[:, t * LANES:(t + 1) * LANES], bd) + _dot(lo[:, t * LANES:(t + 1) * LANES], bd)
                          for t in range(QK_W // LANES)], axis=1)
    y = qk * lax.rsqrt(ms + EPS) * g_ref[...]
    cos = cos_ref[...]
    sin = sin_ref[...]
    lane = lax.broadcasted_iota(jnp.int32, (x.shape[0], LANES), 1)
    first_half = (lane % HEAD_DIM) < (HEAD_DIM // 2)
    tiles = []
    for t in range(QK_W // LANES):
        yt = y[:, t * LANES:(t + 1) * LANES]
        rot = jnp.where(first_half, pltpu.roll(yt, LANES - HEAD_DIM // 2, 1), pltpu.roll(yt, HEAD_DIM // 2, 1))
        tiles.append(yt * cos + rot * sin)
    for h in range(ATT_Q_HEADS):
        t = tiles[h // 2]
        q_ref[h] = (t[:, (h % 2) * HEAD_DIM:(h % 2 + 1) * HEAD_DIM] * ATT_SCALE).astype(BF16)
    kt = tiles[ATT_WIDTH // LANES].T
    kt_ref[0] = kt[:HEAD_DIM].astype(BF16)
    kt_ref[1] = kt[HEAD_DIM:].astype(BF16)
    v = x[:, QK_W:QK_W + LANES]
    one_col = jnp.where(lane == HEAD_DIM, 1.0, 0.0)
    v_ref[0] = jnp.where(lane < HEAD_DIM, v, one_col).astype(BF16)
    v_ref[1] = jnp.where(lane < HEAD_DIM, pltpu.roll(v, HEAD_DIM, 1), one_col).astype(BF16)


def _attn_prep(qkv, g640, cos_t, sin_t, bd):
    s = cos_t.shape[0]
    tm = PROJ_TM
    skip = TM // tm - CTX_LEN // tm
    src = lambda i: jnp.where(i < CTX_LEN // tm, i, i + skip)
    return pl.pallas_call(
        _attn_prep_kernel,
        grid=(s // tm,),
        in_specs=[pl.BlockSpec((tm, ATT_WIDTH + 2 * ATT_KV_WIDTH), lambda i: (src(i), 0)),
                  pl.BlockSpec((1, QK_W), lambda i: (0, 0)),
                  pl.BlockSpec((tm, LANES), lambda i: (i, 0)),
                  pl.BlockSpec((tm, LANES), lambda i: (i, 0)),
                  pl.BlockSpec((LANES, LANES), lambda i: (0, 0))],
        out_specs=[pl.BlockSpec((ATT_Q_HEADS, tm, HEAD_DIM), lambda i: (0, i, 0)),
                   pl.BlockSpec((ATT_KV_HEADS, HEAD_DIM, tm), lambda i: (0, 0, i)),
                   pl.BlockSpec((ATT_KV_HEADS, tm, LANES), lambda i: (0, i, 0))],
        out_shape=[jax.ShapeDtypeStruct((ATT_Q_HEADS, s, HEAD_DIM), BF16),
                   jax.ShapeDtypeStruct((ATT_KV_HEADS, HEAD_DIM, s), BF16),
                   jax.ShapeDtypeStruct((ATT_KV_HEADS, s, LANES), BF16)],
        compiler_params=_params(("parallel",)),
        name="attn_prep",
    )(qkv, g640, cos_t, sin_t, bd)


def _heads_out(acc, rows):
    outs = []
    for h in range(GQA_GROUP):
        a = acc[h * rows:(h + 1) * rows]
        outs.append(a[:, :HEAD_DIM] / a[:, HEAD_DIM:HEAD_DIM + 1])
    return jnp.concatenate(outs, axis=1)


def _glb_kernel(q_ref, kt_ref, v_ref, o_ref, *, nk):
    tq = q_ref.shape[1]
    m_rows = GQA_GROUP * tq
    q = q_ref[...].reshape(m_rows, HEAD_DIM)

    def body(c, carry):
        m, acc = carry
        s = _dot(q, kt_ref[c])
        m_new = jnp.maximum(m, jnp.max(s, axis=1, keepdims=True))
        p = jnp.exp(s - m_new)
        acc = jnp.exp(m - m_new) * acc + _dot(p.astype(BF16), v_ref[c])
        return m_new, acc

    m0 = jnp.full((m_rows, 1), NEG_INF, F32)
    acc0 = jnp.zeros((m_rows, LANES), F32)
    _, acc = lax.fori_loop(0, nk, body, (m0, acc0), unroll=True)
    o_ref[...] = _heads_out(acc, tq).astype(BF16)


def _glb_attn(q, kt, v, n_rows):
    s = q.shape[1]
    nk, tk = kt.shape[1], kt.shape[3]
    tq = GLB_TQ
    nq = (s - CTX_LEN) // tq
    half = GQA_GROUP * HEAD_DIM
    return pl.pallas_call(
        functools.partial(_glb_kernel, nk=nk),
        grid=(ATT_KV_HEADS, nq),
        in_specs=[pl.BlockSpec((GQA_GROUP, tq, HEAD_DIM), lambda kv, i: (kv, i + CTX_LEN // tq, 0)),
                  pl.BlockSpec((None, nk, HEAD_DIM, tk), lambda kv, i: (kv, 0, 0, 0)),
                  pl.BlockSpec((None, nk, tk, LANES), lambda kv, i: (kv, 0, 0, 0))],
        out_specs=pl.BlockSpec((tq, half), lambda kv, i: (i + TM // tq, kv)),
        out_shape=jax.ShapeDtypeStruct((n_rows, ATT_WIDTH), BF16),
        compiler_params=_params(("arbitrary", "arbitrary")),
        name="glb_attn",
    )(q, kt, v)


def _swa_kernel(sink_ref, q_ref, *refs, nt):
    nkb = SWA_TQ // SWA_WINDOW + 2
    kt_refs, ktx_ref = refs[:nkb], refs[nkb]
    v_refs, vx_ref, o_ref = refs[nkb + 1:2 * nkb + 1], refs[2 * nkb + 1], refs[2 * nkb + 2]
    n = pl.program_id(0)
    w = SWA_WINDOW
    tq = SWA_TQ
    rows = GQA_GROUP * tq
    qi = lax.broadcasted_iota(jnp.int32, (rows, nkb * w), 0) % tq
    kj = lax.broadcasted_iota(jnp.int32, (rows, nkb * w), 1)
    ok = (kj >= qi) & (kj <= qi + 2 * w)
    ok = ok & jnp.logical_or(n > 0, kj >= w) & jnp.logical_or(n < nt - 1, kj < (nkb - 1) * w)
    outs = []
    for kv in range(ATT_KV_HEADS):
        q = q_ref[kv * GQA_GROUP:(kv + 1) * GQA_GROUP].reshape(rows, HEAD_DIM)
        kt = jnp.concatenate([r[kv] for r in kt_refs], axis=1)
        s = jnp.concatenate([jnp.where(ok, _dot(q, kt), NEG_INF), _dot(q, ktx_ref[kv])], axis=1)
        sink = jnp.concatenate([jnp.full((tq, 1), sink_ref[kv * GQA_GROUP + h], F32) for h in range(GQA_GROUP)], axis=0)
        m = jnp.maximum(jnp.max(s, axis=1, keepdims=True), sink)
        v = jnp.concatenate([r[kv] for r in v_refs] + [vx_ref[kv]], axis=0)
        acc = _dot(jnp.exp(s - m).astype(BF16), v)
        lane = lax.broadcasted_iota(jnp.int32, acc.shape, 1)
        acc = acc + jnp.where(lane == HEAD_DIM, jnp.exp(sink - m), 0.0)
        outs.append(_heads_out(acc, tq))
    o_ref[...] = jnp.concatenate(outs, axis=1).astype(BF16)


def _swa_attn(q, kt, v, sink, n_rows):
    s = q.shape[1]
    w = SWA_WINDOW
    tq = SWA_TQ
    nb = (s - CTX_LEN) // w
    nt = (s - CTX_LEN) // tq
    nkb = tq // w + 2
    c0 = CTX_LEN // w
    blk = lambda b: (lambda n: jnp.clip(n * (tq // w) - 1 + b, 0, nb - 1) + c0)
    kt_spec = lambda f: pl.BlockSpec((ATT_KV_HEADS, HEAD_DIM, w), lambda n: (0, 0, f(n)))
    v_spec = lambda f: pl.BlockSpec((ATT_KV_HEADS, w, LANES), lambda n: (0, f(n), 0))
    return pl.pallas_call(
        functools.partial(_swa_kernel, nt=nt),
        grid=(nt,),
        in_specs=[pl.BlockSpec(memory_space=pltpu.SMEM),
                  pl.BlockSpec((ATT_Q_HEADS, tq, HEAD_DIM), lambda n: (0, n + CTX_LEN // tq, 0))]
                 + [kt_spec(blk(b)) for b in range(nkb)]
                 + [pl.BlockSpec((ATT_KV_HEADS, HEAD_DIM, CTX_LEN), lambda n: (0, 0, 0))]
                 + [v_spec(blk(b)) for b in range(nkb)]
                 + [pl.BlockSpec((ATT_KV_HEADS, CTX_LEN, LANES), lambda n: (0, 0, 0))],
        out_specs=pl.BlockSpec((tq, ATT_WIDTH), lambda n: (n + TM // tq, 0)),
        out_shape=jax.ShapeDtypeStruct((n_rows, ATT_WIDTH), BF16),
        compiler_params=_params(("parallel",)),
        name="swa_attn",
    )(sink, q, *([kt] * (nkb + 1)), *([v] * (nkb + 1)))


def _ctx_attn_kernel(sink_ref, q_ref, kt_ref, v_ref, prev_ref, o_ref, *, use_sink):
    del prev_ref
    rows = GQA_GROUP * CTX_LEN
    outs = []
    for kv in range(ATT_KV_HEADS):
        q = q_ref[kv * GQA_GROUP:(kv + 1) * GQA_GROUP].reshape(rows, HEAD_DIM)
        s = _dot(q, kt_ref[kv])
        m = jnp.max(s, axis=1, keepdims=True)
        if use_sink:
            sink = jnp.concatenate([jnp.full((CTX_LEN, 1), sink_ref[kv * GQA_GROUP + h], F32)
                                    for h in range(GQA_GROUP)], axis=0)
            m = jnp.maximum(m, sink)
        acc = _dot(jnp.exp(s - m).astype(BF16), v_ref[kv])
        if use_sink:
            lane = lax.broadcasted_iota(jnp.int32, acc.shape, 1)
            acc = acc + jnp.where(lane == HEAD_DIM, jnp.exp(sink - m), 0.0)
        outs.append(_heads_out(acc, CTX_LEN))
    o_ref[:CTX_LEN] = jnp.concatenate(outs, axis=1).astype(BF16)
    o_ref[CTX_LEN:] = jnp.zeros((TM - CTX_LEN, ATT_WIDTH), BF16)


def _ctx_attn(q, kt, v, sink, y_prev, use_sink):
    return pl.pallas_call(
        functools.partial(_ctx_attn_kernel, use_sink=use_sink),
        grid=(1,),
        in_specs=[pl.BlockSpec(memory_space=pltpu.SMEM),
                  pl.BlockSpec((ATT_Q_HEADS, CTX_LEN, HEAD_DIM), lambda i: (0, 0, 0)),
                  pl.BlockSpec((ATT_KV_HEADS, HEAD_DIM, CTX_LEN), lambda i: (0, 0, 0)),
                  pl.BlockSpec((ATT_KV_HEADS, CTX_LEN, LANES), lambda i: (0, 0, 0)),
                  pl.BlockSpec(memory_space=pl.ANY)],
        out_specs=pl.BlockSpec((TM, ATT_WIDTH), lambda i: (0, 0)),
        out_shape=jax.ShapeDtypeStruct(y_prev.shape, y_prev.dtype),
        input_output_aliases={4: 0},
        compiler_params=_params(("arbitrary",)),
        name="ctx_attn",
    )(sink, q, kt, v, y_prev)


def _ssd_prep_kernel(xp_ref, xc_ref, xn_ref, dt_ref, w_ref, b_ref, dtb_ref, xo_ref, dto_ref, *, nblk):
    b = pl.program_id(0)
    t = SSD_T
    lat0 = TM // t
    has_prev = jnp.logical_and(b != 0, b != lat0)
    has_next = jnp.logical_and(b != CTX_LEN // t - 1, b != nblk - 1)
    cur = xc_ref[...]
    prev = jnp.where(has_prev, xp_ref[...], 0.0)
    nxt = jnp.where(has_next, xn_ref[...], 0.0)
    row = lax.broadcasted_iota(jnp.int32, cur.shape, 0)
    halo = prev.shape[0]
    rep = lambda v: jnp.concatenate([v] * (t // halo), axis=0)
    w = w_ref[...]
    acc = cur * w[2:3, :] + b_ref[...]
    for k in (1, 2):
        down = jnp.where(row < k, rep(pltpu.roll(prev, k, 0)), pltpu.roll(cur, k, 0))
        up = jnp.where(row >= t - k, rep(pltpu.roll(nxt, halo - k, 0)), pltpu.roll(cur, t - k, 0))
        acc = acc + down * w[2 - k:3 - k, :] + up * w[2 + k:3 + k, :]
    xo_ref[...] = _silu(acc)
    d = dt_ref[...] + dtb_ref[...]
    dto_ref[...] = jnp.maximum(d, 0.0) + jnp.log1p(jnp.exp(-jnp.abs(d)))


def _ssd_prep(xbc, dt, conv_w, conv_b, dt_bias):
    n = xbc.shape[0]
    t = SSD_T
    nblk = n // t
    return pl.pallas_call(
        functools.partial(_ssd_prep_kernel, nblk=nblk),
        grid=(nblk,),
        in_specs=[pl.BlockSpec((8, SSD_XBC), lambda b: (jnp.maximum(b * (t // 8) - 1, 0), 0)),
                  pl.BlockSpec((t, SSD_XBC), lambda b: (b, 0)),
                  pl.BlockSpec((8, SSD_XBC), lambda b: (jnp.minimum((b + 1) * (t // 8), n // 8 - 1), 0)),
                  pl.BlockSpec((t, DT_PAD), lambda b: (b, 0)),
                  pl.BlockSpec((8, SSD_XBC), lambda b: (0, 0)),
                  pl.BlockSpec((1, SSD_XBC), lambda b: (0, 0)),
                  pl.BlockSpec((1, DT_PAD), lambda b: (0, 0))],
        out_specs=[pl.BlockSpec((t, SSD_XBC), lambda b: (b, 0)),
                   pl.BlockSpec((t, DT_PAD), lambda b: (b, 0))],
        out_shape=[jax.ShapeDtypeStruct((n, SSD_XBC), F32), jax.ShapeDtypeStruct((n, DT_PAD), F32)],
        compiler_params=_params(("parallel",)),
        name="ssd_prep",
    )(xbc, xbc, xbc, dt, conv_w, conv_b, dt_bias)


def _ssd_scan_kernel(tbl_ref, x_ref, dt_ref, arow_ref, tri_ref, e_ref, *rest, direction, final):
    if final:
        yf_ref, z_ref, dsk_ref, ng_ref, o_ref, st_ref = rest
    else:
        o_ref, st_ref = rest
    step = pl.program_id(0)
    flag = tbl_ref[1, step]
    t = SSD_T
    gw = SSD_INNER // SSD_GROUPS
    hpg = SSD_HEADS // SSD_GROUPS

    @pl.when(flag == 2)
    def _():
        o_ref[...] = jnp.zeros_like(o_ref)

    @pl.when(flag == 1)
    def _():
        st_ref[...] = jnp.zeros_like(st_ref)

    @pl.when(flag != 2)
    def _():
        x = x_ref[...]
        xs = x[:, :SSD_INNER]
        bm = x[:, SSD_INNER:SSD_INNER + SSD_BC]
        cm = x[:, SSD_INNER + SSD_BC:]
        dt = dt_ref[...]
        adt = dt * arow_ref[...]
        tri = tri_ref[...]
        keep = tri > 0
        cum = _split_rhs_dot(tri, adt, 3)
        tot = cum[t - 1:t, :] if direction == 0 else cum[0:1, :]
        e = e_ref[...]
        dt_e = _split_lhs_dot(dt, e, 2)
        expc_e = _split_lhs_dot(jnp.exp(cum), e, 2)
        decs_e = _split_lhs_dot(jnp.exp(tot - cum), e, 2)
        dch_e = _split_lhs_dot(jnp.broadcast_to(jnp.exp(tot), (8, DT_PAD)), e, 2)[0:1]
        cum_t = cum.T
        xdt = xs * dt_e
        ys = []
        for g in range(SSD_GROUPS):
            bg = bm[:, g * SSD_STATE:(g + 1) * SSD_STATE]
            cgb = cm[:, g * SSD_STATE:(g + 1) * SSD_STATE].astype(BF16)
            cb = _dot_nt(cgb, bg.astype(BF16))
            stg = st_ref[g]
            y_off = _dot(cgb, stg.astype(BF16)) * expc_e[:, g * gw:(g + 1) * gw]
            parts = []
            for j in range(hpg):
                h = g * hpg + j
                r = direction * SSD_HEADS + h
                diff = cum[:, r:r + 1] - cum_t[r:r + 1, :]
                lmat = jnp.exp(jnp.where(keep, diff, NEG_INF))
                wmat = (cb * lmat).astype(BF16)
                parts.append(_dot(wmat, xdt[:, h * SSD_HEAD_DIM:(h + 1) * SSD_HEAD_DIM].astype(BF16)))
            ys.append(jnp.concatenate(parts, axis=1) + y_off)
            xw = (xdt[:, g * gw:(g + 1) * gw] * decs_e[:, g * gw:(g + 1) * gw]).astype(BF16)
            st_ref[g] = stg * dch_e[:, g * gw:(g + 1) * gw] + _dot(bg.T.astype(BF16), xw)
        y = jnp.concatenate(ys, axis=1)
        if final:
            y = yf_ref[...] + y + dsk_ref[...] * xs
            y = y * _silu(z_ref[...])
            ms = jnp.mean(y * y, axis=-1, keepdims=True)
            o_ref[...] = (y * lax.rsqrt(ms + EPS) * ng_ref[...]).astype(BF16)
        else:
            o_ref[...] = y


def _ssd_scan(tbl, xact, dtsp, arow, tri, e, direction, extra=None):
    n = xact.shape[0]
    t = SSD_T
    final = extra is not None
    blk = lambda w: pl.BlockSpec((t, w), lambda s, tb: (tb[0, s], 0))
    const = lambda r, w: pl.BlockSpec((r, w), lambda s, tb: (0, 0))
    in_specs = [blk(SSD_XBC), blk(DT_PAD), const(1, DT_PAD), const(t, t), const(DT_PAD, SSD_INNER)]
    args = [xact, dtsp, arow, tri, e]
    if final:
        in_specs += [blk(SSD_INNER), blk(SSD_INNER), const(1, SSD_INNER), const(1, SSD_INNER)]
        args += list(extra)
    return pl.pallas_call(
        functools.partial(_ssd_scan_kernel, direction=direction, final=final),
        grid_spec=pltpu.PrefetchScalarGridSpec(
            num_scalar_prefetch=1, grid=(tbl.shape[1],), in_specs=in_specs,
            out_specs=blk(SSD_INNER),
            scratch_shapes=[pltpu.VMEM((SSD_GROUPS, SSD_STATE, SSD_INNER // SSD_GROUPS), F32)]),
        out_shape=jax.ShapeDtypeStruct((n, SSD_INNER), BF16 if final else F32),
        compiler_params=_params(("arbitrary",)),
        name="ssd_scan_bwd" if final else "ssd_scan_fwd",
    )(tbl, *args)


def _ssd_tables(n_rows):
    t = SSD_T
    ctx = list(range(CTX_LEN // t))
    pad = list(range(CTX_LEN // t, TM // t))
    lat = list(range(TM // t, n_rows // t))
    fwd = ctx + lat + pad
    bwd = ctx[::-1] + lat[::-1] + pad
    flags = [1] + [0] * (len(ctx) + len(lat) - 1) + [2] * len(pad)
    return (jnp.asarray(np.array([fwd, flags], np.int32)), jnp.asarray(np.array([bwd, flags], np.int32)))


def _dot_nt_split(a, b):
    a_hi = a.astype(BF16)
    b_hi = b.astype(BF16)
    a_lo = (a - a_hi.astype(F32)).astype(BF16)
    b_lo = (b - b_hi.astype(F32)).astype(BF16)
    return _dot_nt(a_hi, b_hi) + _dot_nt(a_hi, b_lo) + _dot_nt(a_lo, b_hi)


def _s5_setup_kernel(pa_ref, pb_ref, pc_ref, tile_ref, m_ref, bs_ref, csa_ref, csb_ref, lre_ref, lim_ref):
    t_len = S5_T
    hh = S5_GROUP
    a_re = pa_ref[0:1, :]
    a_im = pa_ref[1:2, :]
    dt = jnp.exp(pa_ref[2:3, :])
    mag = jnp.exp(a_re * dt)
    lr = mag * jnp.cos(a_im * dt)
    li = mag * jnp.sin(a_im * dt)
    den = a_re * a_re + a_im * a_im
    nr = lr - 1.0
    f_re = (nr * a_re + li * a_im) / den
    f_im = (li * a_re - nr * a_im) / den
    b_re, b_im = pb_ref[0:hh], pb_ref[hh:2 * hh]
    c_re, c_im = pc_ref[0:hh], pc_ref[hh:2 * hh]
    bb_re = f_re * b_re - f_im * b_im
    bb_im = f_re * b_im + f_im * b_re
    pw = [(jnp.ones_like(lr), jnp.zeros_like(lr))]
    for _ in range(t_len):
        pr, pi = pw[-1]
        pw.append((pr * lr - pi * li, pr * li + pi * lr))
    fwd = lax.broadcasted_iota(jnp.int32, lr.shape, 1) < S5_STATE
    pick = lambda kf, kb: (jnp.where(fwd, pw[kf][0], pw[kb][0]), jnp.where(fwd, pw[kf][1], pw[kb][1]))
    bs_re, bs_im, cs_re, cs_im = [], [], [], []
    for t in range(t_len):
        er, ei = pick(t_len - 1 - t, t)
        bs_re.append(er * bb_re - ei * bb_im)
        bs_im.append(er * bb_im + ei * bb_re)
        fr, fi = pick(t + 1, t_len - t)
        cs_re.append(c_re * fr - c_im * fi)
        cs_im.append(c_re * fi + c_im * fr)
    bs_full = jnp.concatenate([jnp.concatenate(bs_re, axis=0), jnp.concatenate(bs_im, axis=0)], axis=1)
    cs_full = jnp.concatenate([jnp.concatenate(cs_re, axis=0), -jnp.concatenate(cs_im, axis=0)], axis=1)
    wide = bs_full.shape
    fwd_w = (lax.broadcasted_iota(jnp.int32, wide, 1) % (2 * S5_STATE)) < S5_STATE
    bs_ref[...] = bs_full.astype(BF16)
    csa_ref[...] = jnp.where(fwd_w, cs_full, 0.0).astype(BF16)
    csb_ref[...] = jnp.where(fwd_w, 0.0, cs_full).astype(BF16)
    lre_ref[...] = pw[t_len][0]
    lim_ref[...] = pw[t_len][1]
    cc = jnp.concatenate([c_re, -c_im], axis=1)
    cc = jnp.concatenate([cc, jnp.zeros((LANES - hh, wide[1]), F32)], axis=0)
    fwd_c = (lax.broadcasted_iota(jnp.int32, cc.shape, 1) % (2 * S5_STATE)) < S5_STATE
    kr_f = _dot_nt_split(bs_full, jnp.where(fwd_c, cc, 0.0))
    k_b = _dot_nt_split(bs_full, jnp.where(fwd_c, 0.0, cc))
    tile = tile_ref[...]
    kr_w = _split_lhs_dot(kr_f, tile, 3)
    kb_w = _split_lhs_dot(k_b, tile, 3)
    lane_blk = lax.broadcasted_iota(jnp.int32, wide, 1) // hh
    m = jnp.zeros(wide, F32)
    for t in range(t_len):
        up, dn = kr_w, kb_w
        if t < t_len - 1:
            r = (t_len - 1 - t) * hh
            up = jnp.concatenate([kr_w[r:], jnp.zeros((r, wide[1]), F32)], axis=0)
        if t > 0:
            r = t * hh
            dn = jnp.concatenate([jnp.zeros((r, wide[1]), F32), kb_w[:wide[0] - r]], axis=0)
        m = m + jnp.where(lane_blk == t, up + dn, 0.0)
    m_ref[...] = m.astype(BF16)


def _s5_setup(a_re, a_im, log_dt, b_re, b_im, c_re, c_im):
    g = a_re.shape[1]
    tw = S5_T * S5_GROUP
    both = lambda v: jnp.concatenate([v[0], v[1]], axis=-1)
    ldt = jnp.broadcast_to(log_dt[:, :, None], (2, g, S5_STATE))
    pa = jnp.stack([both(a_re), both(a_im), both(ldt)], axis=1).astype(F32)
    pa = jnp.pad(pa, ((0, 0), (0, 5), (0, 0)))
    dup = lambda v: jnp.concatenate([v, v], axis=-1)
    pb = jnp.concatenate([dup(jnp.swapaxes(b_re, 1, 2)), dup(jnp.swapaxes(b_im, 1, 2))], axis=1).astype(F32)
    pc = jnp.concatenate([dup(c_re), dup(c_im)], axis=1).astype(F32)
    tile = jnp.asarray(np.tile(np.eye(LANES, S5_GROUP).astype(np.float32), (1, S5_T)), BF16)
    per_g = lambda r, c: pl.BlockSpec((None, r, c), lambda i: (i, 0, 0))
    mat = jax.ShapeDtypeStruct((g, tw, tw), BF16)
    lam = jax.ShapeDtypeStruct((g, 1, 2 * S5_STATE), F32)
    m, bs, csa, csb, lre, lim = pl.pallas_call(
        _s5_setup_kernel,
        grid=(g,),
        in_specs=[per_g(8, 2 * S5_STATE), per_g(2 * S5_GROUP, 2 * S5_STATE), per_g(2 * S5_GROUP, 2 * S5_STATE),
                  pl.BlockSpec((LANES, tw), lambda i: (0, 0))],
        out_specs=[per_g(tw, tw)] * 4 + [per_g(1, 2 * S5_STATE)] * 2,
        out_shape=[mat] * 4 + [lam] * 2,
        compiler_params=_params(("parallel",)),
        name="s5_setup",
    )(pa, pb, pc, tile)
    return m, bs, csa, csb, lre[:, 0], lim[:, 0]


def _s5_kernel(u_ref, m_ref, bs_ref, csa_ref, csb_ref, lre_ref, lim_ref, y_ref, v_ref, sa_ref, sb_ref, *, nctx, npad, nch):
    gb = S5_GB
    for j in range(gb):
        v = _dot(u_ref[j], bs_ref[j])
        v_ref.at[0][pl.ds(j, nch, stride=gb), :] = v[:, :LANES]
        v_ref.at[1][pl.ds(j, nch, stride=gb), :] = v[:, LANES:]
    lre = lre_ref[0]
    lim = lim_ref[0]
    is_fwd = lax.broadcasted_iota(jnp.int32, (gb, LANES), 1) < S5_STATE

    nlat = nch - nctx - npad

    def body(i, carry):
        sre, sim = carry
        tail = i - nlat
        rf = jnp.where(i < nctx, i, jnp.where(i < nctx + nlat, i + npad, tail)) * gb
        rb = jnp.where(i < nctx, nctx - 1 - i, jnp.where(i < nctx + nlat, nch + nctx - 1 - i, tail)) * gb
        sa_ref[0, pl.ds(rf, gb), :] = sre
        sa_ref[1, pl.ds(rf, gb), :] = sim
        sb_ref[0, pl.ds(rb, gb), :] = sre
        sb_ref[1, pl.ds(rb, gb), :] = sim
        vre = jnp.where(is_fwd, v_ref[0, pl.ds(rf, gb), :], v_ref[0, pl.ds(rb, gb), :])
        vim = jnp.where(is_fwd, v_ref[1, pl.ds(rf, gb), :], v_ref[1, pl.ds(rb, gb), :])
        return lre * sre - lim * sim + vre, lre * sim + lim * sre + vim

    zero = jnp.zeros((gb, LANES), F32)
    lax.fori_loop(0, nch, body, (zero, zero))
    for j in range(gb):
        rows = pl.ds(j, nch, stride=gb)
        sa = jnp.concatenate([sa_ref.at[0][rows, :], sa_ref.at[1][rows, :]], axis=1).astype(BF16)
        sb = jnp.concatenate([sb_ref.at[0][rows, :], sb_ref.at[1][rows, :]], axis=1).astype(BF16)
        y_ref[j] = _dot(u_ref[j], m_ref[j]) + _dot_nt(sa, csa_ref[j]) + _dot_nt(sb, csb_ref[j])


def _s5_scan(ug, m, bs, csa, csb, lam_re, lam_im):
    g, nch, tw = ug.shape
    gb = S5_GB
    sw = 4 * S5_STATE
    blk = lambda a, b: pl.BlockSpec((gb, a, b), lambda i: (i, 0, 0))
    lam_spec = pl.BlockSpec((1, gb, 2 * S5_STATE), lambda i: (i, 0, 0))
    return pl.pallas_call(
        functools.partial(_s5_kernel, nctx=CTX_LEN // S5_T, npad=(TM - CTX_LEN) // S5_T, nch=nch),
        grid=(g // gb,),
        in_specs=[blk(nch, tw), blk(tw, tw), blk(tw, sw), blk(tw, sw), blk(tw, sw), lam_spec, lam_spec],
        out_specs=blk(nch, tw),
        out_shape=jax.ShapeDtypeStruct((g, nch, tw), F32),
        scratch_shapes=[pltpu.VMEM((2, nch * gb, LANES), F32) for _ in range(3)],
        compiler_params=_params(("parallel",)),
        name="s5_scan",
    )(ug, m, bs, csa, csb, lam_re.reshape(g // gb, gb, -1), lam_im.reshape(g // gb, gb, -1))


def _s5_select():
    gpt = LANES // S5_GROUP
    sel = np.zeros((gpt, S5_T * LANES, S5_T * S5_GROUP), np.float32)
    for q in range(gpt):
        for t in range(S5_T):
            for h in range(S5_GROUP):
                sel[q, t * LANES + q * S5_GROUP + h, t * S5_GROUP + h] = 1.0
    return jnp.asarray(sel, BF16)


def _s5_gather_kernel(u_ref, sel_ref, o_ref):
    rb = o_ref.shape[1]
    u2 = jnp.concatenate([u_ref[pl.ds(t, rb, stride=S5_T), :] for t in range(S5_T)], axis=1).astype(BF16)
    for q in range(LANES // S5_GROUP):
        o_ref[q] = _dot(u2, sel_ref[q]).astype(BF16)


def _s5_gather(u, sel):
    n = u.shape[0]
    gpt = LANES // S5_GROUP
    rb = n // S5_T // 4
    tw = S5_T * S5_GROUP
    return pl.pallas_call(
        _s5_gather_kernel,
        grid=(S5_WIDTH // LANES, 4),
        in_specs=[pl.BlockSpec((rb * S5_T, LANES), lambda j, r: (r, j)),
                  pl.BlockSpec((gpt, S5_T * LANES, tw), lambda j, r: (0, 0, 0), pipeline_mode=pl.Buffered(1))],
        out_specs=pl.BlockSpec((gpt, rb, tw), lambda j, r: (j, r, 0)),
        out_shape=jax.ShapeDtypeStruct((S5_GROUPS, n // S5_T, tw), BF16),
        compiler_params=_params(("parallel", "parallel")),
        name="s5_gather",
    )(u, sel)


def _s5_scatter_kernel(y_ref, sel_ref, o_ref):
    rb = y_ref.shape[1]
    acc = None
    for q in range(LANES // S5_GROUP):
        y = y_ref[q]
        hi = y.astype(BF16)
        lo = (y - hi.astype(F32)).astype(BF16)
        part = _dot_nt(hi, sel_ref[q]) + _dot_nt(lo, sel_ref[q])
        acc = part if acc is None else acc + part
    for t in range(S5_T):
        o_ref[pl.ds(t, rb, stride=S5_T), :] = acc[:, t * LANES:(t + 1) * LANES]


def _s5_scatter(yg, sel):
    g, nchp, tw = yg.shape
    gpt = LANES // S5_GROUP
    rb = nchp // 4
    return pl.pallas_call(
        _s5_scatter_kernel,
        grid=(S5_WIDTH // LANES, 4),
        in_specs=[pl.BlockSpec((gpt, rb, tw), lambda j, r: (j, r, 0)),
                  pl.BlockSpec((gpt, S5_T * LANES, tw), lambda j, r: (0, 0, 0), pipeline_mode=pl.Buffered(1))],
        out_specs=pl.BlockSpec((rb * S5_T, LANES), lambda j, r: (r, j)),
        out_shape=jax.ShapeDtypeStruct((nchp * S5_T, S5_WIDTH), F32),
        compiler_params=_params(("parallel", "parallel")),
        name="s5_scatter",
    )(yg, sel)


def _s5_glu_kernel(y_ref, u_ref, d_ref, w_ref, b_ref, o_ref):
    y = jax.nn.gelu(y_ref[...] + d_ref[...] * u_ref[...])
    t = _dot(y.astype(BF16), w_ref[...]) + b_ref[...]
    o_ref[...] = (t[:, :S5_WIDTH] * jax.nn.sigmoid(t[:, S5_WIDTH:])).astype(BF16)


def _s5_glu(y, u, d, w, b):
    n = y.shape[0]
    row = lambda wd: pl.BlockSpec((TM, wd), lambda i: (i, 0))
    return pl.pallas_call(
        _s5_glu_kernel,
        grid=(n // TM,),
        in_specs=[row(S5_WIDTH), row(S5_WIDTH), pl.BlockSpec((1, S5_WIDTH), lambda i: (0, 0)),
                  pl.BlockSpec((S5_WIDTH, 2 * S5_WIDTH), lambda i: (0, 0)),
                  pl.BlockSpec((1, 2 * S5_WIDTH), lambda i: (0, 0))],
        out_specs=row(S5_WIDTH),
        out_shape=jax.ShapeDtypeStruct((n, S5_WIDTH), BF16),
        compiler_params=_params(("parallel",)),
        name="s5_glu",
    )(y, u, d, w, b)


BR_WIDTHS = (SSD_INNER, S5_WIDTH, ATT_WIDTH, ATT_WIDTH)


def _merge_kernel(h_ref, *refs):
    y_refs, wg_refs, wb_refs, o_ref = refs[:4], refs[4:8], refs[8:12], refs[12]
    h = h_ref[...]
    acc = None
    for y_ref, wg_ref, wb_ref in zip(y_refs, wg_refs, wb_refs):
        t = jax.nn.sigmoid(_dot(h, wg_ref[...])) * _dot(y_ref[...], wb_ref[...])
        acc = t if acc is None else acc + t
    o_ref[...] = acc.astype(BF16)


def _merge(hb, ys, w_gate, w_brs, tn=512):
    n = hb.shape[0]
    nj = D_MODEL // tn
    row = lambda w: pl.BlockSpec((TM, w), lambda i, j: (i, 0))
    return pl.pallas_call(
        _merge_kernel,
        grid=(n // TM, nj),
        in_specs=[row(D_MODEL)] + [row(w) for w in BR_WIDTHS]
                 + [pl.BlockSpec((D_MODEL, tn), lambda i, j, b=b: (0, b * nj + j)) for b in range(N_BRANCH)]
                 + [pl.BlockSpec((w, tn), lambda i, j: (0, j)) for w in BR_WIDTHS],
        out_specs=pl.BlockSpec((TM, tn), lambda i, j: (i, j)),
        out_shape=jax.ShapeDtypeStruct((n, D_MODEL), BF16),
        compiler_params=_params(("parallel", "arbitrary")),
        name="merge",
    )(hb, *ys, w_gate, w_gate, w_gate, w_gate, *w_brs)


def _out_kernel(x_ref, a_ref, mod_ref, w_ref, o_ref):
    o_ref[...] = x_ref[...] + mod_ref[5:6, :] * _dot(a_ref[...], w_ref[...])


def _out_proj(x, acc, mods_l, w_out):
    n = x.shape[0]
    row = lambda: pl.BlockSpec((TM, D_MODEL), lambda i: (i, 0))
    return pl.pallas_call(
        _out_kernel,
        grid=(n // TM,),
        in_specs=[row(), row(), _mod_spec(),
                  pl.BlockSpec((D_MODEL, D_MODEL), lambda i: (0, 0), pipeline_mode=pl.Buffered(1))],
        out_specs=row(),
        out_shape=jax.ShapeDtypeStruct((n, D_MODEL), F32),
        compiler_params=_params(("parallel",)),
        name="out_proj",
    )(x, acc, mods_l, w_out)


def _rope_tables(seq):
    rows = seq // GRID_W
    row = jnp.repeat(jnp.arange(rows, dtype=F32), GRID_W)
    col = jnp.tile(jnp.arange(GRID_W, dtype=F32), rows)
    n_freq = HEAD_DIM // 4
    inv = ROPE_THETA ** (-jnp.arange(n_freq, dtype=F32) / n_freq)
    ang = jnp.concatenate([row[:, None] * inv, col[:, None] * inv], axis=-1)
    cos, sin = jnp.cos(ang), jnp.sin(ang)
    reps = LANES // HEAD_DIM
    cos_t = jnp.tile(jnp.concatenate([cos, cos], axis=-1), (1, reps))
    sin_t = jnp.tile(jnp.concatenate([-sin, sin], axis=-1), (1, reps))
    cos_t = jnp.concatenate([jnp.ones((CTX_LEN, LANES), F32), cos_t], axis=0)
    sin_t = jnp.concatenate([jnp.zeros((CTX_LEN, LANES), F32), sin_t], axis=0)
    return cos_t, sin_t


def _ssd_expand(direction):
    e = np.zeros((DT_PAD, SSD_INNER), np.float32)
    for h in range(SSD_HEADS):
        e[direction * SSD_HEADS + h, h * SSD_HEAD_DIM:(h + 1) * SSD_HEAD_DIM] = 1.0
    return jnp.asarray(e, BF16)


def _to_stream(ctx_rows, lat_rows):
    pad = jnp.zeros((TM - CTX_LEN,) + ctx_rows.shape[1:], ctx_rows.dtype)
    return jnp.concatenate([ctx_rows, pad, lat_rows], axis=0)


def kernel(x, c, ctx, c_ctx, w_mod, b_mod, norm_g, ffn_in, ffn_out, w_in, ssd_conv_w, ssd_conv_b, ssd_a_log, ssd_dt_bias, ssd_d, ssd_norm_g, s5_a_re, s5_a_im, s5_log_dt, s5_b_re, s5_b_im, s5_c_re, s5_c_im, s5_d, s5_glu_w, s5_glu_b, swa_qk_g, swa_sink, glb_qk_g, w_br_ssd, w_br_s5, w_br_swa, w_br_glb, w_out):
    assert x.shape[0] == 1 and ctx.shape[1] == CTX_LEN
    depth = w_mod.shape[0]
    seq = x.shape[1]
    n_rows = TM + seq
    s_len = CTX_LEN + seq

    ffn_in_b = ffn_in.astype(BF16)
    ffn_out_b = ffn_out.astype(BF16)
    offs = np.cumsum((0, SSD_INNER, SSD_XBC, 2 * SSD_HEADS, S5_WIDTH, ATT_WIDTH, ATT_KV_WIDTH, ATT_KV_WIDTH,
                      ATT_WIDTH, ATT_KV_WIDTH, ATT_KV_WIDTH))
    seg = lambda a, b: w_in[:, :, offs[a]:offs[b]].astype(BF16)
    w_dt = jnp.pad(seg(2, 3), ((0, 0), (0, 0), (0, DT_PAD - 2 * SSD_HEADS)))
    w_proj = (seg(0, 1), seg(1, 2), w_dt, seg(3, 4), seg(4, 7), seg(7, 10))
    w_gate = w_in[:, :, offs[10]:].astype(BF16)
    w_brs = tuple(w.astype(BF16) for w in (w_br_ssd, w_br_s5, w_br_swa, w_br_glb))
    w_out_b = w_out.astype(BF16)
    glu_w_b = s5_glu_w.astype(BF16)

    mods = _mods(c, c_ctx, w_mod, b_mod)
    cos_t, sin_t = _rope_tables(seq)
    bd = jnp.asarray(np.kron(np.eye(LANES // HEAD_DIM), np.ones((HEAD_DIM, HEAD_DIM))) / HEAD_DIM, BF16)
    tbl_f, tbl_b = _ssd_tables(n_rows)
    tril = jnp.asarray(np.tril(np.ones((SSD_T, SSD_T), np.float32)), BF16)
    triu = jnp.asarray(np.triu(np.ones((SSD_T, SSD_T), np.float32)), BF16)
    e_dirs = (_ssd_expand(0), _ssd_expand(1))
    s5_sel = _s5_select()
    nk = s_len // GLB_TK

    xs = _to_stream(ctx[0], x[0])
    for i in range(depth):
        ml = mods[i]
        xs = _ffn(xs, ml, norm_g[i, 0:1], ffn_in_b[i, 0], ffn_out_b[i, 0], 0)
        hb, z, xbc, dt, u, qkv_swa, qkv_glb = _proj(xs, ml, norm_g[i, 1:2], [w[i] for w in w_proj])

        conv_w = jnp.pad(ssd_conv_w[i], ((0, 8 - SSD_CONV), (0, 0)))
        dt_bias = jnp.pad(ssd_dt_bias[i].reshape(1, -1), ((0, 0), (0, DT_PAD - 2 * SSD_HEADS)))
        xact, dtsp = _ssd_prep(xbc, dt, conv_w, ssd_conv_b[i].reshape(1, -1), dt_bias)
        a_neg = -jnp.exp(ssd_a_log[i].astype(F32))
        arow = lambda d: jnp.zeros((1, DT_PAD), F32).at[0, d * SSD_HEADS:(d + 1) * SSD_HEADS].set(a_neg[d])
        y_f = _ssd_scan(tbl_f, xact, dtsp, arow(0), tril, e_dirs[0], 0)
        d_exp = jnp.repeat(ssd_d[i].astype(F32), SSD_HEAD_DIM).reshape(1, -1)
        y_ssd = _ssd_scan(tbl_b, xact, dtsp, arow(1), triu, e_dirs[1], 1,
                          extra=(y_f, z, d_exp, ssd_norm_g[i].reshape(1, -1)))

        s5m, s5bs, s5csa, s5csb, lam_re, lam_im = _s5_setup(
            s5_a_re[i], s5_a_im[i], s5_log_dt[i], s5_b_re[i], s5_b_im[i], s5_c_re[i], s5_c_im[i])
        yg = _s5_scan(_s5_gather(u, s5_sel), s5m, s5bs, s5csa, s5csb, lam_re, lam_im)
        y_s5 = _s5_glu(_s5_scatter(yg, s5_sel), u, s5_d[i].reshape(1, -1), glu_w_b[i], s5_glu_b[i].reshape(1, -1))

        g_swa = jnp.concatenate([jnp.tile(swa_qk_g[i, 0], ATT_Q_HEADS), jnp.tile(swa_qk_g[i, 1], ATT_KV_HEADS)])[None]
        q, kt, v = _attn_prep(qkv_swa, g_swa, cos_t, sin_t, bd)
        y_swa = _swa_attn(q, kt, v, swa_sink[i], n_rows)
        y_swa = _ctx_attn(q, kt, v, swa_sink[i], y_swa, True)

        g_glb = jnp.concatenate([jnp.tile(glb_qk_g[i, 0], ATT_Q_HEADS), jnp.tile(glb_qk_g[i, 1], ATT_KV_HEADS)])[None]
        q, kt, v = _attn_prep(qkv_glb, g_glb, cos_t, sin_t, bd)
        kt_c = kt.reshape(ATT_KV_HEADS, HEAD_DIM, nk, GLB_TK).transpose(0, 2, 1, 3)
        y_glb = _glb_attn(q, kt_c, v.reshape(ATT_KV_HEADS, nk, GLB_TK, LANES), n_rows)
        y_glb = _ctx_attn(q, kt, v, swa_sink[i], y_glb, False)

        acc = _merge(hb, (y_ssd, y_s5, y_swa, y_glb), w_gate[i], [w[i] for w in w_brs])
        xs = _out_proj(xs, acc, ml, w_out_b[i])
        xs = _ffn(xs, ml, norm_g[i, 2:3], ffn_in_b[i, 1], ffn_out_b[i, 1], 2)
    return xs[TM:][None]
```

```python
import functools
import math

import numpy as np
import jax
import jax.numpy as jnp
from jax import lax
from jax.experimental import pallas as pl
from jax.experimental.pallas import tpu as pltpu

F32 = jnp.float32
BF16 = jnp.bfloat16

D_MODEL = 2048
GRID_W = 64
CTX_LEN = 256
N_MOD = 9
D_FF = 5632
SSD_HEADS = 12
SSD_HEAD_DIM = 64
SSD_INNER = SSD_HEADS * SSD_HEAD_DIM
SSD_GROUPS = 2
SSD_STATE = 128
SSD_BC = SSD_GROUPS * SSD_STATE
SSD_XBC = SSD_INNER + 2 * SSD_BC
SSD_CONV = 5
S5_WIDTH = 512
S5_GROUP = 16
S5_GROUPS = S5_WIDTH // S5_GROUP
S5_STATE = 64
HEAD_DIM = 64
ATT_Q_HEADS = 8
ATT_KV_HEADS = 2
GQA_GROUP = ATT_Q_HEADS // ATT_KV_HEADS
ATT_WIDTH = ATT_Q_HEADS * HEAD_DIM
ATT_KV_WIDTH = ATT_KV_HEADS * HEAD_DIM
ATT_SCALE = HEAD_DIM ** -0.5
SWA_WINDOW = 128
ROPE_THETA = 10000.0
NEG_INF = -1e30
N_BRANCH = 4
EPS = 1e-6

LANES = 128
VMEM_LIMIT = 56 * 1024 * 1024

TM = 512
SSD_T = 128
S5_T = 16
S5_GB = 4
PROJ_TM = 256
GLB_TQ = 256
SWA_TQ = 256
GLB_TK = 1280
DT_PAD = LANES


def _params(sem, vmem=VMEM_LIMIT):
    return pltpu.CompilerParams(dimension_semantics=sem, vmem_limit_bytes=vmem)


def _silu(x):
    return x * jax.nn.sigmoid(x)


def _norm_mod(x, g, shift, scale):
    ms = jnp.mean(x * x, axis=-1, keepdims=True)
    return (x * lax.rsqrt(ms + EPS)) * (g * (1.0 + scale)) + shift


def _dot(a, b):
    return jnp.dot(a, b, preferred_element_type=F32)


def _dot_nt(a, b):
    return lax.dot_general(a, b, (((1,), (1,)), ((), ())), preferred_element_type=F32)


def _split_rhs_dot(m, x, parts):
    acc = None
    r = x
    for _ in range(parts):
        hi = r.astype(BF16)
        t = _dot(m, hi)
        acc = t if acc is None else acc + t
        r = r - hi.astype(F32)
    return acc


def _split_lhs_dot(x, m, parts):
    acc = None
    r = x
    for _ in range(parts):
        hi = r.astype(BF16)
        t = _dot(hi, m)
        acc = t if acc is None else acc + t
        r = r - hi.astype(F32)
    return acc


def _mod_kernel(s_ref, w_ref, b_ref, o_ref):
    s = _silu(s_ref[...])
    o_ref[...] = _dot(s.astype(BF16), w_ref[...].astype(BF16)) + b_ref[...]


def _mods(c, c_ctx, w_mod, b_mod):
    depth = w_mod.shape[0]
    s = jnp.zeros((8, D_MODEL), F32).at[0].set(c_ctx).at[1].set(c[0])
    tn = 1024
    out = pl.pallas_call(
        _mod_kernel,
        grid=(depth, N_MOD * D_MODEL // tn),
        in_specs=[pl.BlockSpec((8, D_MODEL), lambda l, j: (0, 0)),
                  pl.BlockSpec((None, D_MODEL, tn), lambda l, j: (l, 0, j)),
                  pl.BlockSpec((None, 1, tn), lambda l, j: (l, 0, j))],
        out_specs=pl.BlockSpec((None, 8, tn), lambda l, j: (l, 0, j)),
        out_shape=jax.ShapeDtypeStruct((depth, 8, N_MOD * D_MODEL), F32),
        compiler_params=_params(("parallel", "parallel")),
        name="mods",
    )(s, w_mod, b_mod.reshape(depth, 1, N_MOD * D_MODEL))
    return out[:, :2].reshape(depth, 2, N_MOD, D_MODEL)


def _mod_spec():
    return pl.BlockSpec((None, N_MOD, D_MODEL), lambda i, *_: (jnp.minimum(i, 1), 0, 0))


def _ffn_kernel(x_ref, mod_ref, g_ref, wa_ref, wb_ref, wo_ref, o_ref, h_ref, acc_ref, *, sub, nj):
    j = pl.program_id(1)

    def step(first, last):
        if first:
            h = _norm_mod(x_ref[...], g_ref[...], mod_ref[3 * sub:3 * sub + 1, :],
                          mod_ref[3 * sub + 1:3 * sub + 2, :]).astype(BF16)
            h_ref[...] = h
        else:
            h = h_ref[...]
        t = _dot((_silu(_dot(h, wa_ref[...])) * _dot(h, wb_ref[...])).astype(BF16), wo_ref[...])
        acc = t if first else acc_ref[...] + t
        if last:
            o_ref[...] = x_ref[...] + (0.5 * mod_ref[3 * sub + 2:3 * sub + 3, :]) * acc
        else:
            acc_ref[...] = acc

    pl.when(j == 0)(lambda: step(True, False))
    pl.when(jnp.logical_and(j > 0, j < nj - 1))(lambda: step(False, False))
    pl.when(j == nj - 1)(lambda: step(False, True))


def _ffn(x, mods_l, g, w_in, w_out, sub, tf=512):
    n = x.shape[0]
    nj = D_FF // tf
    return pl.pallas_call(
        functools.partial(_ffn_kernel, sub=sub, nj=nj),
        grid=(n // TM, nj),
        in_specs=[pl.BlockSpec((TM, D_MODEL), lambda i, j: (i, 0)),
                  _mod_spec(),
                  pl.BlockSpec((1, D_MODEL), lambda i, j: (0, 0)),
                  pl.BlockSpec((D_MODEL, tf), lambda i, j: (0, j)),
                  pl.BlockSpec((D_MODEL, tf), lambda i, j: (0, j + nj)),
                  pl.BlockSpec((tf, D_MODEL), lambda i, j: (j, 0))],
        out_specs=pl.BlockSpec((TM, D_MODEL), lambda i, j: (i, 0)),
        out_shape=jax.ShapeDtypeStruct((n, D_MODEL), F32),
        scratch_shapes=[pltpu.VMEM((TM, D_MODEL), BF16), pltpu.VMEM((TM, D_MODEL), F32)],
        compiler_params=_params(("parallel", "arbitrary")),
        name="ffn",
    )(x, mods_l, g, w_in, w_in, w_out)


PROJ_WIDTHS = (SSD_INNER, SSD_XBC, DT_PAD, S5_WIDTH, ATT_WIDTH + 2 * ATT_KV_WIDTH, ATT_WIDTH + 2 * ATT_KV_WIDTH)
W_IN_BOUNDS = tuple(zip(
    np.cumsum((0, SSD_INNER, SSD_XBC, 2 * SSD_HEADS, S5_WIDTH, ATT_WIDTH + 2 * ATT_KV_WIDTH, ATT_WIDTH + 2 * ATT_KV_WIDTH)),
    np.cumsum((SSD_INNER, SSD_XBC, 2 * SSD_HEADS, S5_WIDTH, ATT_WIDTH + 2 * ATT_KV_WIDTH, ATT_WIDTH + 2 * ATT_KV_WIDTH,
               N_BRANCH * D_MODEL))))


def _split_w_in_kernel(w_ref, *o_refs):
    w = w_ref[...]
    for o_ref, (lo, hi) in zip(o_refs, W_IN_BOUNDS):
        piece = w[:, int(lo):int(hi)].astype(BF16)
        if o_ref.shape[1] > piece.shape[1]:
            piece = jnp.concatenate([piece, jnp.zeros((piece.shape[0], o_ref.shape[1] - piece.shape[1]), BF16)], axis=1)
        o_ref[...] = piece


def _split_w_in(w_in):
    depth, d, d_in = w_in.shape
    tr = 128
    widths = PROJ_WIDTHS + (N_BRANCH * D_MODEL,)
    return pl.pallas_call(
        _split_w_in_kernel,
        grid=(depth, d // tr),
        in_specs=[pl.BlockSpec((None, tr, d_in), lambda l, r: (l, r, 0))],
        out_specs=[pl.BlockSpec((None, tr, w), lambda l, r: (l, r, 0)) for w in widths],
        out_shape=[jax.ShapeDtypeStruct((depth, d, w), BF16) for w in widths],
        compiler_params=_params(("parallel", "parallel")),
        name="split_w_in",
    )(w_in)


def _proj_kernel(x_ref, mod_ref, g_ref, *refs):
    nw = len(PROJ_WIDTHS)
    w_refs, h_ref, o_refs = refs[:nw], refs[nw], refs[nw + 1:]
    h = _norm_mod(x_ref[...], g_ref[...], mod_ref[3:4, :], mod_ref[4:5, :]).astype(BF16)
    h_ref[...] = h
    for w_ref, o_ref in zip(w_refs, o_refs):
        o_ref[...] = _dot(h, w_ref[...])


def _proj(x, mods_l, g, ws):
    n = x.shape[0]
    tm = PROJ_TM
    row = lambda w: pl.BlockSpec((tm, w), lambda i: (i, 0))
    return pl.pallas_call(
        _proj_kernel,
        grid=(n // tm,),
        in_specs=[row(D_MODEL),
                  pl.BlockSpec((None, N_MOD, D_MODEL), lambda i: (jnp.minimum(i // (TM // tm), 1), 0, 0)),
                  pl.BlockSpec((1, D_MODEL), lambda i: (0, 0))]
                 + [pl.BlockSpec((D_MODEL, w), lambda i: (0, 0), pipeline_mode=pl.Buffered(1)) for w in PROJ_WIDTHS],
        out_specs=[row(D_MODEL)] + [row(w) for w in PROJ_WIDTHS],
        out_shape=[jax.ShapeDtypeStruct((n, D_MODEL), BF16)]
                  + [jax.ShapeDtypeStruct((n, w), F32) for w in PROJ_WIDTHS],
        compiler_params=_params(("parallel",)),
        name="proj",
    )(x, mods_l, g, *ws)


QK_W = ATT_WIDTH + ATT_KV_WIDTH


def _attn_prep_kernel(qkv_ref, g_ref, cos_ref, sin_ref, bd_ref, q_ref, kt_ref, v_ref):
    x = qkv_ref[...]
    qk = x[:, :QK_W]
    sq = qk * qk
    hi = sq.astype(BF16)
    lo = (sq - hi.astype(F32)).astype(BF16)
    bd = bd_ref[...]
    ms = jnp.concatenate([_dot(hi[:, t * LANES:(t + 1) * LANES], bd) + _dot(lo[:, t * LANES:(t + 1) * LANES], bd)
                          for t in range(QK_W // LANES)], axis=1)
    y = qk * lax.rsqrt(ms + EPS) * g_ref[...]
    cos = cos_ref[...]
    sin = sin_ref[...]
    lane = lax.broadcasted_iota(jnp.int32, (x.shape[0], LANES), 1)
    first_half = (lane % HEAD_DIM) < (HEAD_DIM // 2)
    tiles = []
    for t in range(QK_W // LANES):
        yt = y[:, t * LANES:(t + 1) * LANES]
        rot = jnp.where(first_half, pltpu.roll(yt, LANES - HEAD_DIM // 2, 1), pltpu.roll(yt, HEAD_DIM // 2, 1))
        tiles.append(yt * cos + rot * sin)
    for h in range(ATT_Q_HEADS):
        t = tiles[h // 2]
        q_ref[h] = (t[:, (h % 2) * HEAD_DIM:(h % 2 + 1) * HEAD_DIM] * ATT_SCALE).astype(BF16)
    kt = tiles[ATT_WIDTH // LANES].T
    kt_ref[0] = kt[:HEAD_DIM].astype(BF16)
    kt_ref[1] = kt[HEAD_DIM:].astype(BF16)
    v = x[:, QK_W:QK_W + LANES]
    one_col = jnp.where(lane == HEAD_DIM, 1.0, 0.0)
    v_ref[0] = jnp.where(lane < HEAD_DIM, v, one_col).astype(BF16)
    v_ref[1] = jnp.where(lane < HEAD_DIM, pltpu.roll(v, HEAD_DIM, 1), one_col).astype(BF16)


def _attn_prep(qkv, g640, cos_t, sin_t, bd):
    s = cos_t.shape[0]
    tm = PROJ_TM
    skip = TM // tm - CTX_LEN // tm
    src = lambda i: jnp.where(i < CTX_LEN // tm, i, i + skip)
    return pl.pallas_call(
        _attn_prep_kernel,
        grid=(s // tm,),
        in_specs=[pl.BlockSpec((tm, ATT_WIDTH + 2 * ATT_KV_WIDTH), lambda i: (src(i), 0)),
                  pl.BlockSpec((1, QK_W), lambda i: (0, 0)),
                  pl.BlockSpec((tm, LANES), lambda i: (i, 0)),
                  pl.BlockSpec((tm, LANES), lambda i: (i, 0)),
                  pl.BlockSpec((LANES, LANES), lambda i: (0, 0))],
        out_specs=[pl.BlockSpec((ATT_Q_HEADS, tm, HEAD_DIM), lambda i: (0, i, 0)),
                   pl.BlockSpec((ATT_KV_HEADS, HEAD_DIM, tm), lambda i: (0, 0, i)),
                   pl.BlockSpec((ATT_KV_HEADS, tm, LANES), lambda i: (0, i, 0))],
        out_shape=[jax.ShapeDtypeStruct((ATT_Q_HEADS, s, HEAD_DIM), BF16),
                   jax.ShapeDtypeStruct((ATT_KV_HEADS, HEAD_DIM, s), BF16),
                   jax.ShapeDtypeStruct((ATT_KV_HEADS, s, LANES), BF16)],
        compiler_params=_params(("parallel",)),
        name="attn_prep",
    )(qkv, g640, cos_t, sin_t, bd)


def _heads_out(acc, rows):
    outs = []
    for h in range(GQA_GROUP):
        a = acc[h * rows:(h + 1) * rows]
        outs.append(a[:, :HEAD_DIM] / a[:, HEAD_DIM:HEAD_DIM + 1])
    return jnp.concatenate(outs, axis=1)


def _glb_kernel(q_ref, kt_ref, v_ref, o_ref, *, nk):
    tq = q_ref.shape[1]
    m_rows = GQA_GROUP * tq
    q = q_ref[...].reshape(m_rows, HEAD_DIM)

    def body(c, carry):
        m, acc = carry
        s = _dot(q, kt_ref[c])
        m_new = jnp.maximum(m, jnp.max(s, axis=1, keepdims=True))
        p = jnp.exp(s - m_new)
        acc = jnp.exp(m - m_new) * acc + _dot(p.astype(BF16), v_ref[c])
        return m_new, acc

    m0 = jnp.full((m_rows, 1), NEG_INF, F32)
    acc0 = jnp.zeros((m_rows, LANES), F32)
    _, acc = lax.fori_loop(0, nk, body, (m0, acc0), unroll=True)
    o_ref[...] = _heads_out(acc, tq).astype(BF16)


def _glb_attn(q, kt, v, n_rows):
    s = q.shape[1]
    nk, tk = kt.shape[1], kt.shape[3]
    tq = GLB_TQ
    nq = (s - CTX_LEN) // tq
    half = GQA_GROUP * HEAD_DIM
    return pl.pallas_call(
        functools.partial(_glb_kernel, nk=nk),
        grid=(ATT_KV_HEADS, nq),
        in_specs=[pl.BlockSpec((GQA_GROUP, tq, HEAD_DIM), lambda kv, i: (kv, i + CTX_LEN // tq, 0)),
                  pl.BlockSpec((None, nk, HEAD_DIM, tk), lambda kv, i: (kv, 0, 0, 0)),
                  pl.BlockSpec((None, nk, tk, LANES), lambda kv, i: (kv, 0, 0, 0))],
        out_specs=pl.BlockSpec((tq, half), lambda kv, i: (i + TM // tq, kv)),
        out_shape=jax.ShapeDtypeStruct((n_rows, ATT_WIDTH), BF16),
        compiler_params=_params(("arbitrary", "arbitrary")),
        name="glb_attn",
    )(q, kt, v)


def _swa_bias():
    w, tq = SWA_WINDOW, SWA_TQ
    nkw = tq + 2 * w
    qi = np.arange(tq)[:, None]
    kj = np.arange(nkw)[None, :]
    ok = (kj >= qi) & (kj <= qi + 2 * w)
    oks = np.stack([ok, ok & (kj >= w), ok & (kj < nkw - w)])
    return jnp.asarray(np.where(oks, 0.0, NEG_INF).astype(np.float32))


def _swa_kernel(sink_ref, bias_ref, q_ref, *refs, nt):
    nkb = SWA_TQ // SWA_WINDOW + 2
    kt_refs, ktx_ref = refs[:nkb], refs[nkb]
    v_refs, vx_ref, o_ref = refs[nkb + 1:2 * nkb + 1], refs[2 * nkb + 1], refs[2 * nkb + 2]
    n = pl.program_id(0)
    tq = SWA_TQ
    rows = GQA_GROUP * tq
    bias = bias_ref[jnp.where(n == 0, 1, jnp.where(n == nt - 1, 2, 0))]
    bias = jnp.concatenate([bias] * GQA_GROUP, axis=0)
    outs = []
    for kv in range(ATT_KV_HEADS):
        q = q_ref[kv * GQA_GROUP:(kv + 1) * GQA_GROUP].reshape(rows, HEAD_DIM)
        kt = jnp.concatenate([r[kv] for r in kt_refs], axis=1)
        s = jnp.concatenate([_dot(q, kt) + bias, _dot(q, ktx_ref[kv])], axis=1)
        sink = jnp.concatenate([jnp.full((tq, 1), sink_ref[kv * GQA_GROUP + h], F32) for h in range(GQA_GROUP)], axis=0)
        m = jnp.maximum(jnp.max(s, axis=1, keepdims=True), sink)
        v = jnp.concatenate([r[kv] for r in v_refs] + [vx_ref[kv]], axis=0)
        acc = _dot(jnp.exp(s - m).astype(BF16), v)
        lane = lax.broadcasted_iota(jnp.int32, acc.shape, 1)
        acc = acc + jnp.where(lane == HEAD_DIM, jnp.exp(sink - m), 0.0)
        outs.append(_heads_out(acc, tq))
    o_ref[...] = jnp.concatenate(outs, axis=1).astype(BF16)


def _swa_attn(q, kt, v, sink, n_rows):
    s = q.shape[1]
    w = SWA_WINDOW
    tq = SWA_TQ
    nb = (s - CTX_LEN) // w
    nt = (s - CTX_LEN) // tq
    nkb = tq // w + 2
    c0 = CTX_LEN // w
    blk = lambda b: (lambda n: jnp.clip(n * (tq // w) - 1 + b, 0, nb - 1) + c0)
    kt_spec = lambda f: pl.BlockSpec((ATT_KV_HEADS, HEAD_DIM, w), lambda n: (0, 0, f(n)))
    v_spec = lambda f: pl.BlockSpec((ATT_KV_HEADS, w, LANES), lambda n: (0, f(n), 0))
    return pl.pallas_call(
        functools.partial(_swa_kernel, nt=nt),
        grid=(nt,),
        in_specs=[pl.BlockSpec(memory_space=pltpu.SMEM),
                  pl.BlockSpec((3, tq, nkb * w), lambda n: (0, 0, 0)),
                  pl.BlockSpec((ATT_Q_HEADS, tq, HEAD_DIM), lambda n: (0, n + CTX_LEN // tq, 0))]
                 + [kt_spec(blk(b)) for b in range(nkb)]
                 + [pl.BlockSpec((ATT_KV_HEADS, HEAD_DIM, CTX_LEN), lambda n: (0, 0, 0))]
                 + [v_spec(blk(b)) for b in range(nkb)]
                 + [pl.BlockSpec((ATT_KV_HEADS, CTX_LEN, LANES), lambda n: (0, 0, 0))],
        out_specs=pl.BlockSpec((tq, ATT_WIDTH), lambda n: (n + TM // tq, 0)),
        out_shape=jax.ShapeDtypeStruct((n_rows, ATT_WIDTH), BF16),
        compiler_params=_params(("parallel",)),
        name="swa_attn",
    )(sink, _swa_bias(), q, *([kt] * (nkb + 1)), *([v] * (nkb + 1)))


def _ctx_attn_kernel(sink_ref, q_ref, kt_ref, v_ref, prev_ref, o_ref, *, use_sink):
    del prev_ref
    rows = GQA_GROUP * CTX_LEN
    outs = []
    for kv in range(ATT_KV_HEADS):
        q = q_ref[kv * GQA_GROUP:(kv + 1) * GQA_GROUP].reshape(rows, HEAD_DIM)
        s = _dot(q, kt_ref[kv])
        m = jnp.max(s, axis=1, keepdims=True)
        if use_sink:
            sink = jnp.concatenate([jnp.full((CTX_LEN, 1), sink_ref[kv * GQA_GROUP + h], F32)
                                    for h in range(GQA_GROUP)], axis=0)
            m = jnp.maximum(m, sink)
        acc = _dot(jnp.exp(s - m).astype(BF16), v_ref[kv])
        if use_sink:
            lane = lax.broadcasted_iota(jnp.int32, acc.shape, 1)
            acc = acc + jnp.where(lane == HEAD_DIM, jnp.exp(sink - m), 0.0)
        outs.append(_heads_out(acc, CTX_LEN))
    o_ref[:CTX_LEN] = jnp.concatenate(outs, axis=1).astype(BF16)
    o_ref[CTX_LEN:] = jnp.zeros((TM - CTX_LEN, ATT_WIDTH), BF16)


def _ctx_attn(q, kt, v, sink, y_prev, use_sink):
    return pl.pallas_call(
        functools.partial(_ctx_attn_kernel, use_sink=use_sink),
        grid=(1,),
        in_specs=[pl.BlockSpec(memory_space=pltpu.SMEM),
                  pl.BlockSpec((ATT_Q_HEADS, CTX_LEN, HEAD_DIM), lambda i: (0, 0, 0)),
                  pl.BlockSpec((ATT_KV_HEADS, HEAD_DIM, CTX_LEN), lambda i: (0, 0, 0)),
                  pl.BlockSpec((ATT_KV_HEADS, CTX_LEN, LANES), lambda i: (0, 0, 0)),
                  pl.BlockSpec(memory_space=pl.ANY)],
        out_specs=pl.BlockSpec((TM, ATT_WIDTH), lambda i: (0, 0)),
        out_shape=jax.ShapeDtypeStruct(y_prev.shape, y_prev.dtype),
        input_output_aliases={4: 0},
        compiler_params=_params(("arbitrary",)),
        name="ctx_attn",
    )(sink, q, kt, v, y_prev)


def _ssd_prep_kernel(xp_ref, xc_ref, xn_ref, dt_ref, w_ref, b_ref, dtb_ref, xo_ref, dto_ref, *, nblk):
    b = pl.program_id(0)
    t = SSD_T
    lat0 = TM // t
    has_prev = jnp.logical_and(b != 0, b != lat0)
    has_next = jnp.logical_and(b != CTX_LEN // t - 1, b != nblk - 1)
    cur = xc_ref[...]
    prev = jnp.where(has_prev, xp_ref[...], 0.0)
    nxt = jnp.where(has_next, xn_ref[...], 0.0)
    row = lax.broadcasted_iota(jnp.int32, cur.shape, 0)
    halo = prev.shape[0]
    rep = lambda v: jnp.concatenate([v] * (t // halo), axis=0)
    w = w_ref[...]
    acc = cur * w[2:3, :] + b_ref[...]
    for k in (1, 2):
        down = jnp.where(row < k, rep(pltpu.roll(prev, k, 0)), pltpu.roll(cur, k, 0))
        up = jnp.where(row >= t - k, rep(pltpu.roll(nxt, halo - k, 0)), pltpu.roll(cur, t - k, 0))
        acc = acc + down * w[2 - k:3 - k, :] + up * w[2 + k:3 + k, :]
    xo_ref[...] = _silu(acc)
    d = dt_ref[...] + dtb_ref[...]
    dto_ref[...] = jnp.maximum(d, 0.0) + jnp.log1p(jnp.exp(-jnp.abs(d)))


def _ssd_prep(xbc, dt, conv_w, conv_b, dt_bias):
    n = xbc.shape[0]
    t = SSD_T
    nblk = n // t
    return pl.pallas_call(
        functools.partial(_ssd_prep_kernel, nblk=nblk),
        grid=(nblk,),
        in_specs=[pl.BlockSpec((8, SSD_XBC), lambda b: (jnp.maximum(b * (t // 8) - 1, 0), 0)),
                  pl.BlockSpec((t, SSD_XBC), lambda b: (b, 0)),
                  pl.BlockSpec((8, SSD_XBC), lambda b: (jnp.minimum((b + 1) * (t // 8), n // 8 - 1), 0)),
                  pl.BlockSpec((t, DT_PAD), lambda b: (b, 0)),
                  pl.BlockSpec((8, SSD_XBC), lambda b: (0, 0)),
                  pl.BlockSpec((1, SSD_XBC), lambda b: (0, 0)),
                  pl.BlockSpec((1, DT_PAD), lambda b: (0, 0))],
        out_specs=[pl.BlockSpec((t, SSD_XBC), lambda b: (b, 0)),
                   pl.BlockSpec((t, DT_PAD), lambda b: (b, 0))],
        out_shape=[jax.ShapeDtypeStruct((n, SSD_XBC), F32), jax.ShapeDtypeStruct((n, DT_PAD), F32)],
        compiler_params=_params(("parallel",)),
        name="ssd_prep",
    )(xbc, xbc, xbc, dt, conv_w, conv_b, dt_bias)


def _ssd_scan_kernel(tbl_ref, x_ref, dt_ref, arow_ref, tri_ref, e_ref, *rest, direction, final):
    if final:
        yf_ref, z_ref, dsk_ref, ng_ref, o_ref, st_ref = rest
    else:
        o_ref, st_ref = rest
    step = pl.program_id(0)
    flag = tbl_ref[1, step]
    t = SSD_T
    gw = SSD_INNER // SSD_GROUPS
    hpg = SSD_HEADS // SSD_GROUPS

    @pl.when(flag == 2)
    def _():
        o_ref[...] = jnp.zeros_like(o_ref)

    @pl.when(flag == 1)
    def _():
        st_ref[...] = jnp.zeros_like(st_ref)

    @pl.when(flag != 2)
    def _():
        x = x_ref[...]
        xs = x[:, :SSD_INNER]
        bm = x[:, SSD_INNER:SSD_INNER + SSD_BC]
        cm = x[:, SSD_INNER + SSD_BC:]
        dt = dt_ref[...]
        adt = dt * arow_ref[...]
        tri = tri_ref[...]
        keep = tri > 0
        cum = _split_rhs_dot(tri, adt, 3)
        tot = cum[t - 1:t, :] if direction == 0 else cum[0:1, :]
        e = e_ref[...]
        dt_e = _split_lhs_dot(dt, e, 2)
        expc_e = _split_lhs_dot(jnp.exp(cum), e, 2)
        decs_e = _split_lhs_dot(jnp.exp(tot - cum), e, 2)
        dch_e = _split_lhs_dot(jnp.broadcast_to(jnp.exp(tot), (8, DT_PAD)), e, 2)[0:1]
        cum_t = cum.T
        xdt = xs * dt_e
        ys = []
        for g in range(SSD_GROUPS):
            bg = bm[:, g * SSD_STATE:(g + 1) * SSD_STATE]
            cgb = cm[:, g * SSD_STATE:(g + 1) * SSD_STATE].astype(BF16)
            cb = _dot_nt(cgb, bg.astype(BF16))
            stg = st_ref[g]
            y_off = _dot(cgb, stg.astype(BF16)) * expc_e[:, g * gw:(g + 1) * gw]
            parts = []
            for j in range(hpg):
                h = g * hpg + j
                r = direction * SSD_HEADS + h
                diff = cum[:, r:r + 1] - cum_t[r:r + 1, :]
                lmat = jnp.exp(jnp.where(keep, diff, NEG_INF))
                wmat = (cb * lmat).astype(BF16)
                parts.append(_dot(wmat, xdt[:, h * SSD_HEAD_DIM:(h + 1) * SSD_HEAD_DIM].astype(BF16)))
            ys.append(jnp.concatenate(parts, axis=1) + y_off)
            xw = (xdt[:, g * gw:(g + 1) * gw] * decs_e[:, g * gw:(g + 1) * gw]).astype(BF16)
            st_ref[g] = stg * dch_e[:, g * gw:(g + 1) * gw] + _dot(bg.T.astype(BF16), xw)
        y = jnp.concatenate(ys, axis=1)
        if final:
            y = yf_ref[...] + y + dsk_ref[...] * xs
            y = y * _silu(z_ref[...])
            ms = jnp.mean(y * y, axis=-1, keepdims=True)
            o_ref[...] = (y * lax.rsqrt(ms + EPS) * ng_ref[...]).astype(BF16)
        else:
            o_ref[...] = y


def _ssd_scan(tbl, xact, dtsp, arow, tri, e, direction, extra=None):
    n = xact.shape[0]
    t = SSD_T
    final = extra is not None
    blk = lambda w: pl.BlockSpec((t, w), lambda s, tb: (tb[0, s], 0))
    const = lambda r, w: pl.BlockSpec((r, w), lambda s, tb: (0, 0))
    in_specs = [blk(SSD_XBC), blk(DT_PAD), const(1, DT_PAD), const(t, t), const(DT_PAD, SSD_INNER)]
    args = [xact, dtsp, arow, tri, e]
    if final:
        in_specs += [blk(SSD_INNER), blk(SSD_INNER), const(1, SSD_INNER), const(1, SSD_INNER)]
        args += list(extra)
    return pl.pallas_call(
        functools.partial(_ssd_scan_kernel, direction=direction, final=final),
        grid_spec=pltpu.PrefetchScalarGridSpec(
            num_scalar_prefetch=1, grid=(tbl.shape[1],), in_specs=in_specs,
            out_specs=blk(SSD_INNER),
            scratch_shapes=[pltpu.VMEM((SSD_GROUPS, SSD_STATE, SSD_INNER // SSD_GROUPS), F32)]),
        out_shape=jax.ShapeDtypeStruct((n, SSD_INNER), BF16 if final else F32),
        compiler_params=_params(("arbitrary",)),
        name="ssd_scan_bwd" if final else "ssd_scan_fwd",
    )(tbl, *args)


def _ssd_tables(n_rows):
    t = SSD_T
    ctx = list(range(CTX_LEN // t))
    pad = list(range(CTX_LEN // t, TM // t))
    lat = list(range(TM // t, n_rows // t))
    fwd = ctx + lat + pad
    bwd = ctx[::-1] + lat[::-1] + pad
    flags = [1] + [0] * (len(ctx) + len(lat) - 1) + [2] * len(pad)
    return (jnp.asarray(np.array([fwd, flags], np.int32)), jnp.asarray(np.array([bwd, flags], np.int32)))


def _dot_nt_split(a, b):
    a_hi = a.astype(BF16)
    b_hi = b.astype(BF16)
    a_lo = (a - a_hi.astype(F32)).astype(BF16)
    b_lo = (b - b_hi.astype(F32)).astype(BF16)
    return _dot_nt(a_hi, b_hi) + _dot_nt(a_hi, b_lo) + _dot_nt(a_lo, b_hi)


def _s5_setup_kernel(pa_ref, pb_ref, pc_ref, tile_ref, m_ref, bs_ref, csa_ref, csb_ref, lre_ref, lim_ref):
    t_len = S5_T
    hh = S5_GROUP
    a_re = pa_ref[0:1, :]
    a_im = pa_ref[1:2, :]
    dt = jnp.exp(pa_ref[2:3, :])
    mag = jnp.exp(a_re * dt)
    lr = mag * jnp.cos(a_im * dt)
    li = mag * jnp.sin(a_im * dt)
    den = a_re * a_re + a_im * a_im
    nr = lr - 1.0
    f_re = (nr * a_re + li * a_im) / den
    f_im = (li * a_re - nr * a_im) / den
    b_re, b_im = pb_ref[0:hh], pb_ref[hh:2 * hh]
    c_re, c_im = pc_ref[0:hh], pc_ref[hh:2 * hh]
    bb_re = f_re * b_re - f_im * b_im
    bb_im = f_re * b_im + f_im * b_re
    pw = [(jnp.ones_like(lr), jnp.zeros_like(lr))]
    for _ in range(t_len):
        pr, pi = pw[-1]
        pw.append((pr * lr - pi * li, pr * li + pi * lr))
    fwd = lax.broadcasted_iota(jnp.int32, lr.shape, 1) < S5_STATE
    pick = lambda kf, kb: (jnp.where(fwd, pw[kf][0], pw[kb][0]), jnp.where(fwd, pw[kf][1], pw[kb][1]))
    bs_re, bs_im, cs_re, cs_im = [], [], [], []
    for t in range(t_len):
        er, ei = pick(t_len - 1 - t, t)
        bs_re.append(er * bb_re - ei * bb_im)
        bs_im.append(er * bb_im + ei * bb_re)
        fr, fi = pick(t + 1, t_len - t)
        cs_re.append(c_re * fr - c_im * fi)
        cs_im.append(c_re * fi + c_im * fr)
    bs_full = jnp.concatenate([jnp.concatenate(bs_re, axis=0), jnp.concatenate(bs_im, axis=0)], axis=1)
    cs_full = jnp.concatenate([jnp.concatenate(cs_re, axis=0), -jnp.concatenate(cs_im, axis=0)], axis=1)
    wide = bs_full.shape
    fwd_w = (lax.broadcasted_iota(jnp.int32, wide, 1) % (2 * S5_STATE)) < S5_STATE
    bs_ref[...] = bs_full.astype(BF16)
    csa_ref[...] = jnp.where(fwd_w, cs_full, 0.0).astype(BF16)
    csb_ref[...] = jnp.where(fwd_w, 0.0, cs_full).astype(BF16)
    lre_ref[...] = pw[t_len][0]
    lim_ref[...] = pw[t_len][1]
    cc = jnp.concatenate([c_re, -c_im], axis=1)
    cc = jnp.concatenate([cc, jnp.zeros((LANES - hh, wide[1]), F32)], axis=0)
    fwd_c = (lax.broadcasted_iota(jnp.int32, cc.shape, 1) % (2 * S5_STATE)) < S5_STATE
    kr_f = _dot_nt_split(bs_full, jnp.where(fwd_c, cc, 0.0))
    k_b = _dot_nt_split(bs_full, jnp.where(fwd_c, 0.0, cc))
    tile = tile_ref[...]
    kr_w = _split_lhs_dot(kr_f, tile, 3)
    kb_w = _split_lhs_dot(k_b, tile, 3)
    lane_blk = lax.broadcasted_iota(jnp.int32, wide, 1) // hh
    m = jnp.zeros(wide, F32)
    for t in range(t_len):
        up, dn = kr_w, kb_w
        if t < t_len - 1:
            r = (t_len - 1 - t) * hh
            up = jnp.concatenate([kr_w[r:], jnp.zeros((r, wide[1]), F32)], axis=0)
        if t > 0:
            r = t * hh
            dn = jnp.concatenate([jnp.zeros((r, wide[1]), F32), kb_w[:wide[0] - r]], axis=0)
        m = m + jnp.where(lane_blk == t, up + dn, 0.0)
    m_ref[...] = m.astype(BF16)


def _s5_setup(a_re, a_im, log_dt, b_re, b_im, c_re, c_im):
    g = a_re.shape[1]
    tw = S5_T * S5_GROUP
    both = lambda v: jnp.concatenate([v[0], v[1]], axis=-1)
    ldt = jnp.broadcast_to(log_dt[:, :, None], (2, g, S5_STATE))
    pa = jnp.stack([both(a_re), both(a_im), both(ldt)], axis=1).astype(F32)
    pa = jnp.pad(pa, ((0, 0), (0, 5), (0, 0)))
    dup = lambda v: jnp.concatenate([v, v], axis=-1)
    pb = jnp.concatenate([dup(jnp.swapaxes(b_re, 1, 2)), dup(jnp.swapaxes(b_im, 1, 2))], axis=1).astype(F32)
    pc = jnp.concatenate([dup(c_re), dup(c_im)], axis=1).astype(F32)
    tile = jnp.asarray(np.tile(np.eye(LANES, S5_GROUP).astype(np.float32), (1, S5_T)), BF16)
    per_g = lambda r, c: pl.BlockSpec((None, r, c), lambda i: (i, 0, 0))
    mat = jax.ShapeDtypeStruct((g, tw, tw), BF16)
    lam = jax.ShapeDtypeStruct((g, 1, 2 * S5_STATE), F32)
    m, bs, csa, csb, lre, lim = pl.pallas_call(
        _s5_setup_kernel,
        grid=(g,),
        in_specs=[per_g(8, 2 * S5_STATE), per_g(2 * S5_GROUP, 2 * S5_STATE), per_g(2 * S5_GROUP, 2 * S5_STATE),
                  pl.BlockSpec((LANES, tw), lambda i: (0, 0))],
        out_specs=[per_g(tw, tw)] * 4 + [per_g(1, 2 * S5_STATE)] * 2,
        out_shape=[mat] * 4 + [lam] * 2,
        compiler_params=_params(("parallel",)),
        name="s5_setup",
    )(pa, pb, pc, tile)
    return m, bs, csa, csb, lre[:, 0], lim[:, 0]


def _s5_kernel(u_ref, m_ref, bs_ref, csa_ref, csb_ref, lre_ref, lim_ref, y_ref, v_ref, sa_ref, sb_ref, *, nctx, npad, nch):
    gb = S5_GB
    for j in range(gb):
        v = _dot(u_ref[j], bs_ref[j])
        v_ref.at[0][pl.ds(j, nch, stride=gb), :] = v[:, :LANES]
        v_ref.at[1][pl.ds(j, nch, stride=gb), :] = v[:, LANES:]
    lre = lre_ref[0]
    lim = lim_ref[0]
    is_fwd = lax.broadcasted_iota(jnp.int32, (gb, LANES), 1) < S5_STATE

    nlat = nch - nctx - npad

    def body(i, carry):
        sre, sim = carry
        tail = i - nlat
        rf = jnp.where(i < nctx, i, jnp.where(i < nctx + nlat, i + npad, tail)) * gb
        rb = jnp.where(i < nctx, nctx - 1 - i, jnp.where(i < nctx + nlat, nch + nctx - 1 - i, tail)) * gb
        sa_ref[0, pl.ds(rf, gb), :] = sre
        sa_ref[1, pl.ds(rf, gb), :] = sim
        sb_ref[0, pl.ds(rb, gb), :] = sre
        sb_ref[1, pl.ds(rb, gb), :] = sim
        vre = jnp.where(is_fwd, v_ref[0, pl.ds(rf, gb), :], v_ref[0, pl.ds(rb, gb), :])
        vim = jnp.where(is_fwd, v_ref[1, pl.ds(rf, gb), :], v_ref[1, pl.ds(rb, gb), :])
        return lre * sre - lim * sim + vre, lre * sim + lim * sre + vim

    zero = jnp.zeros((gb, LANES), F32)
    lax.fori_loop(0, nch, body, (zero, zero))
    for j in range(gb):
        rows = pl.ds(j, nch, stride=gb)
        sa = jnp.concatenate([sa_ref.at[0][rows, :], sa_ref.at[1][rows, :]], axis=1).astype(BF16)
        sb = jnp.concatenate([sb_ref.at[0][rows, :], sb_ref.at[1][rows, :]], axis=1).astype(BF16)
        y_ref[j] = _dot(u_ref[j], m_ref[j]) + _dot_nt(sa, csa_ref[j]) + _dot_nt(sb, csb_ref[j])


def _s5_scan(ug, m, bs, csa, csb, lam_re, lam_im):
    g, nch, tw = ug.shape
    gb = S5_GB
    sw = 4 * S5_STATE
    blk = lambda a, b: pl.BlockSpec((gb, a, b), lambda i: (i, 0, 0))
    lam_spec = pl.BlockSpec((1, gb, 2 * S5_STATE), lambda i: (i, 0, 0))
    return pl.pallas_call(
        functools.partial(_s5_kernel, nctx=CTX_LEN // S5_T, npad=(TM - CTX_LEN) // S5_T, nch=nch),
        grid=(g // gb,),
        in_specs=[blk(nch, tw), blk(tw, tw), blk(tw, sw), blk(tw, sw), blk(tw, sw), lam_spec, lam_spec],
        out_specs=blk(nch, tw),
        out_shape=jax.ShapeDtypeStruct((g, nch, tw), F32),
        scratch_shapes=[pltpu.VMEM((2, nch * gb, LANES), F32) for _ in range(3)],
        compiler_params=_params(("parallel",)),
        name="s5_scan",
    )(ug, m, bs, csa, csb, lam_re.reshape(g // gb, gb, -1), lam_im.reshape(g // gb, gb, -1))


def _s5_select():
    gpt = LANES // S5_GROUP
    sel = np.zeros((gpt, S5_T * LANES, S5_T * S5_GROUP), np.float32)
    for q in range(gpt):
        for t in range(S5_T):
            for h in range(S5_GROUP):
                sel[q, t * LANES + q * S5_GROUP + h, t * S5_GROUP + h] = 1.0
    return jnp.asarray(sel, BF16)


def _s5_gather_kernel(u_ref, sel_ref, o_ref):
    rb = o_ref.shape[1]
    u2 = jnp.concatenate([u_ref[pl.ds(t, rb, stride=S5_T), :] for t in range(S5_T)], axis=1).astype(BF16)
    for q in range(LANES // S5_GROUP):
        o_ref[q] = _dot(u2, sel_ref[q]).astype(BF16)


def _s5_gather(u, sel):
    n = u.shape[0]
    gpt = LANES // S5_GROUP
    rb = n // S5_T // 4
    tw = S5_T * S5_GROUP
    return pl.pallas_call(
        _s5_gather_kernel,
        grid=(S5_WIDTH // LANES, 4),
        in_specs=[pl.BlockSpec((rb * S5_T, LANES), lambda j, r: (r, j)),
                  pl.BlockSpec((gpt, S5_T * LANES, tw), lambda j, r: (0, 0, 0), pipeline_mode=pl.Buffered(1))],
        out_specs=pl.BlockSpec((gpt, rb, tw), lambda j, r: (j, r, 0)),
        out_shape=jax.ShapeDtypeStruct((S5_GROUPS, n // S5_T, tw), BF16),
        compiler_params=_params(("parallel", "parallel")),
        name="s5_gather",
    )(u, sel)


def _s5_scatter_kernel(y_ref, sel_ref, o_ref):
    rb = y_ref.shape[1]
    acc = None
    for q in range(LANES // S5_GROUP):
        y = y_ref[q]
        hi = y.astype(BF16)
        lo = (y - hi.astype(F32)).astype(BF16)
        part = _dot_nt(hi, sel_ref[q]) + _dot_nt(lo, sel_ref[q])
        acc = part if acc is None else acc + part
    for t in range(S5_T):
        o_ref[pl.ds(t, rb, stride=S5_T), :] = acc[:, t * LANES:(t + 1) * LANES]


def _s5_scatter(yg, sel):
    g, nchp, tw = yg.shape
    gpt = LANES // S5_GROUP
    rb = nchp // 4
    return pl.pallas_call(
        _s5_scatter_kernel,
        grid=(S5_WIDTH // LANES, 4),
        in_specs=[pl.BlockSpec((gpt, rb, tw), lambda j, r: (j, r, 0)),
                  pl.BlockSpec((gpt, S5_T * LANES, tw), lambda j, r: (0, 0, 0), pipeline_mode=pl.Buffered(1))],
        out_specs=pl.BlockSpec((rb * S5_T, LANES), lambda j, r: (r, j)),
        out_shape=jax.ShapeDtypeStruct((nchp * S5_T, S5_WIDTH), F32),
        compiler_params=_params(("parallel", "parallel")),
        name="s5_scatter",
    )(yg, sel)


def _s5_glu_kernel(y_ref, u_ref, d_ref, w_ref, b_ref, o_ref):
    y = jax.nn.gelu(y_ref[...] + d_ref[...] * u_ref[...])
    t = _dot(y.astype(BF16), w_ref[...]) + b_ref[...]
    o_ref[...] = (t[:, :S5_WIDTH] * jax.nn.sigmoid(t[:, S5_WIDTH:])).astype(BF16)


def _s5_glu(y, u, d, w, b):
    n = y.shape[0]
    row = lambda wd: pl.BlockSpec((TM, wd), lambda i: (i, 0))
    return pl.pallas_call(
        _s5_glu_kernel,
        grid=(n // TM,),
        in_specs=[row(S5_WIDTH), row(S5_WIDTH), pl.BlockSpec((1, S5_WIDTH), lambda i: (0, 0)),
                  pl.BlockSpec((S5_WIDTH, 2 * S5_WIDTH), lambda i: (0, 0)),
                  pl.BlockSpec((1, 2 * S5_WIDTH), lambda i: (0, 0))],
        out_specs=row(S5_WIDTH),
        out_shape=jax.ShapeDtypeStruct((n, S5_WIDTH), BF16),
        compiler_params=_params(("parallel",)),
        name="s5_glu",
    )(y, u, d, w, b)


BR_WIDTHS = (SSD_INNER, S5_WIDTH, ATT_WIDTH, ATT_WIDTH)


def _merge_kernel(h_ref, *refs):
    y_refs, wg_refs, wb_refs, o_ref = refs[:4], refs[4:8], refs[8:12], refs[12]
    h = h_ref[...]
    acc = None
    for y_ref, wg_ref, wb_ref in zip(y_refs, wg_refs, wb_refs):
        t = jax.nn.sigmoid(_dot(h, wg_ref[...])) * _dot(y_ref[...], wb_ref[...])
        acc = t if acc is None else acc + t
    o_ref[...] = acc.astype(BF16)


def _merge(hb, ys, w_gate, w_brs, tn=512):
    n = hb.shape[0]
    nj = D_MODEL // tn
    row = lambda w: pl.BlockSpec((TM, w), lambda i, j: (i, 0))
    return pl.pallas_call(
        _merge_kernel,
        grid=(n // TM, nj),
        in_specs=[row(D_MODEL)] + [row(w) for w in BR_WIDTHS]
                 + [pl.BlockSpec((D_MODEL, tn), lambda i, j, b=b: (0, b * nj + j)) for b in range(N_BRANCH)]
                 + [pl.BlockSpec((w, tn), lambda i, j: (0, j)) for w in BR_WIDTHS],
        out_specs=pl.BlockSpec((TM, tn), lambda i, j: (i, j)),
        out_shape=jax.ShapeDtypeStruct((n, D_MODEL), BF16),
        compiler_params=_params(("parallel", "arbitrary")),
        name="merge",
    )(hb, *ys, w_gate, w_gate, w_gate, w_gate, *w_brs)


def _out_kernel(x_ref, a_ref, mod_ref, w_ref, o_ref):
    o_ref[...] = x_ref[...] + mod_ref[5:6, :] * _dot(a_ref[...], w_ref[...])


def _out_proj(x, acc, mods_l, w_out):
    n = x.shape[0]
    row = lambda: pl.BlockSpec((TM, D_MODEL), lambda i: (i, 0))
    return pl.pallas_call(
        _out_kernel,
        grid=(n // TM,),
        in_specs=[row(), row(), _mod_spec(),
                  pl.BlockSpec((D_MODEL, D_MODEL), lambda i: (0, 0), pipeline_mode=pl.Buffered(1))],
        out_specs=row(),
        out_shape=jax.ShapeDtypeStruct((n, D_MODEL), F32),
        compiler_params=_params(("parallel",)),
        name="out_proj",
    )(x, acc, mods_l, w_out)


def _rope_tables(seq):
    rows = seq // GRID_W
    row = jnp.repeat(jnp.arange(rows, dtype=F32), GRID_W)
    col = jnp.tile(jnp.arange(GRID_W, dtype=F32), rows)
    n_freq = HEAD_DIM // 4
    inv = ROPE_THETA ** (-jnp.arange(n_freq, dtype=F32) / n_freq)
    ang = jnp.concatenate([row[:, None] * inv, col[:, None] * inv], axis=-1)
    cos, sin = jnp.cos(ang), jnp.sin(ang)
    reps = LANES // HEAD_DIM
    cos_t = jnp.tile(jnp.concatenate([cos, cos], axis=-1), (1, reps))
    sin_t = jnp.tile(jnp.concatenate([-sin, sin], axis=-1), (1, reps))
    cos_t = jnp.concatenate([jnp.ones((CTX_LEN, LANES), F32), cos_t], axis=0)
    sin_t = jnp.concatenate([jnp.zeros((CTX_LEN, LANES), F32), sin_t], axis=0)
    return cos_t, sin_t


def _ssd_expand(direction):
    e = np.zeros((DT_PAD, SSD_INNER), np.float32)
    for h in range(SSD_HEADS):
        e[direction * SSD_HEADS + h, h * SSD_HEAD_DIM:(h + 1) * SSD_HEAD_DIM] = 1.0
    return jnp.asarray(e, BF16)


def _to_stream(ctx_rows, lat_rows):
    pad = jnp.zeros((TM - CTX_LEN,) + ctx_rows.shape[1:], ctx_rows.dtype)
    return jnp.concatenate([ctx_rows, pad, lat_rows], axis=0)


def kernel(x, c, ctx, c_ctx, w_mod, b_mod, norm_g, ffn_in, ffn_out, w_in, ssd_conv_w, ssd_conv_b, ssd_a_log, ssd_dt_bias, ssd_d, ssd_norm_g, s5_a_re, s5_a_im, s5_log_dt, s5_b_re, s5_b_im, s5_c_re, s5_c_im, s5_d, s5_glu_w, s5_glu_b, swa_qk_g, swa_sink, glb_qk_g, w_br_ssd, w_br_s5, w_br_swa, w_br_glb, w_out):
    assert x.shape[0] == 1 and ctx.shape[1] == CTX_LEN
    depth = w_mod.shape[0]
    seq = x.shape[1]
    n_rows = TM + seq
    s_len = CTX_LEN + seq

    ffn_in_b = ffn_in.astype(BF16)
    ffn_out_b = ffn_out.astype(BF16)
    *w_proj, w_gate = _split_w_in(w_in)
    w_brs = tuple(w.astype(BF16) for w in (w_br_ssd, w_br_s5, w_br_swa, w_br_glb))
    w_out_b = w_out.astype(BF16)
    glu_w_b = s5_glu_w.astype(BF16)

    mods = _mods(c, c_ctx, w_mod, b_mod)
    cos_t, sin_t = _rope_tables(seq)
    bd = jnp.asarray(np.kron(np.eye(LANES // HEAD_DIM), np.ones((HEAD_DIM, HEAD_DIM))) / HEAD_DIM, BF16)
    tbl_f, tbl_b = _ssd_tables(n_rows)
    tril = jnp.asarray(np.tril(np.ones((SSD_T, SSD_T), np.float32)), BF16)
    triu = jnp.asarray(np.triu(np.ones((SSD_T, SSD_T), np.float32)), BF16)
    e_dirs = (_ssd_expand(0), _ssd_expand(1))
    s5_sel = _s5_select()
    nk = s_len // GLB_TK

    xs = _to_stream(ctx[0], x[0])
    for i in range(depth):
        ml = mods[i]
        xs = _ffn(xs, ml, norm_g[i, 0:1], ffn_in_b[i, 0], ffn_out_b[i, 0], 0)
        hb, z, xbc, dt, u, qkv_swa, qkv_glb = _proj(xs, ml, norm_g[i, 1:2], [w[i] for w in w_proj])

        conv_w = jnp.pad(ssd_conv_w[i], ((0, 8 - SSD_CONV), (0, 0)))
        dt_bias = jnp.pad(ssd_dt_bias[i].reshape(1, -1), ((0, 0), (0, DT_PAD - 2 * SSD_HEADS)))
        xact, dtsp = _ssd_prep(xbc, dt, conv_w, ssd_conv_b[i].reshape(1, -1), dt_bias)
        a_neg = -jnp.exp(ssd_a_log[i].astype(F32))
        arow = lambda d: jnp.zeros((1, DT_PAD), F32).at[0, d * SSD_HEADS:(d + 1) * SSD_HEADS].set(a_neg[d])
        y_f = _ssd_scan(tbl_f, xact, dtsp, arow(0), tril, e_dirs[0], 0)
        d_exp = jnp.repeat(ssd_d[i].astype(F32), SSD_HEAD_DIM).reshape(1, -1)
        y_ssd = _ssd_scan(tbl_b, xact, dtsp, arow(1), triu, e_dirs[1], 1,
                          extra=(y_f, z, d_exp, ssd_norm_g[i].reshape(1, -1)))

        s5m, s5bs, s5csa, s5csb, lam_re, lam_im = _s5_setup(
            s5_a_re[i], s5_a_im[i], s5_log_dt[i], s5_b_re[i], s5_b_im[i], s5_c_re[i], s5_c_im[i])
        yg = _s5_scan(_s5_gather(u, s5_sel), s5m, s5bs, s5csa, s5csb, lam_re, lam_im)
        y_s5 = _s5_glu(_s5_scatter(yg, s5_sel), u, s5_d[i].reshape(1, -1), glu_w_b[i], s5_glu_b[i].reshape(1, -1))

        g_swa = jnp.concatenate([jnp.tile(swa_qk_g[i, 0], ATT_Q_HEADS), jnp.tile(swa_qk_g[i, 1], ATT_KV_HEADS)])[None]
        q, kt, v = _attn_prep(qkv_swa, g_swa, cos_t, sin_t, bd)
        y_swa = _swa_attn(q, kt, v, swa_sink[i], n_rows)
        y_swa = _ctx_attn(q, kt, v, swa_sink[i], y_swa, True)

        g_glb = jnp.concatenate([jnp.tile(glb_qk_g[i, 0], ATT_Q_HEADS), jnp.tile(glb_qk_g[i, 1], ATT_KV_HEADS)])[None]
        q, kt, v = _attn_prep(qkv_glb, g_glb, cos_t, sin_t, bd)
        kt_c = kt.reshape(ATT_KV_HEADS, HEAD_DIM, nk, GLB_TK).transpose(0, 2, 1, 3)
        y_glb = _glb_attn(q, kt_c, v.reshape(ATT_KV_HEADS, nk, GLB_TK, LANES), n_rows)
        y_glb = _ctx_attn(q, kt, v, swa_sink[i], y_glb, False)

        acc = _merge(hb, (y_ssd, y_s5, y_swa, y_glb), w_gate[i], [w[i] for w in w_brs])
        xs = _out_proj(xs, acc, ml, w_out_b[i])
        xs = _ffn(xs, ml, norm_g[i, 2:3], ffn_in_b[i, 1], ffn_out_b[i, 1], 2)
    return xs[TM:][None]
```

```python
import functools
import math

import numpy as np
import jax
import jax.numpy as jnp
from jax import lax
from jax.experimental import pallas as pl
from jax.experimental.pallas import tpu as pltpu

F32 = jnp.float32
BF16 = jnp.bfloat16

D_MODEL = 2048
GRID_W = 64
CTX_LEN = 256
N_MOD = 9
D_FF = 5632
SSD_HEADS = 12
SSD_HEAD_DIM = 64
SSD_INNER = SSD_HEADS * SSD_HEAD_DIM
SSD_GROUPS = 2
SSD_STATE = 128
SSD_BC = SSD_GROUPS * SSD_STATE
SSD_XBC = SSD_INNER + 2 * SSD_BC
SSD_CONV = 5
S5_WIDTH = 512
S5_GROUP = 16
S5_GROUPS = S5_WIDTH // S5_GROUP
S5_STATE = 64
HEAD_DIM = 64
ATT_Q_HEADS = 8
ATT_KV_HEADS = 2
GQA_GROUP = ATT_Q_HEADS // ATT_KV_HEADS
ATT_WIDTH = ATT_Q_HEADS * HEAD_DIM
ATT_KV_WIDTH = ATT_KV_HEADS * HEAD_DIM
ATT_SCALE = HEAD_DIM ** -0.5
SWA_WINDOW = 128
ROPE_THETA = 10000.0
NEG_INF = -1e30
N_BRANCH = 4
EPS = 1e-6

LANES = 128
VMEM_LIMIT = 56 * 1024 * 1024

TM = 512
SSD_T = 128
S5_T = 16
S5_GB = 4
PROJ_TM = 256
GLB_TQ = 256
SWA_TQ = 256
GLB_TK = 1280
DT_PAD = LANES


def _params(sem, vmem=VMEM_LIMIT):
    return pltpu.CompilerParams(dimension_semantics=sem, vmem_limit_bytes=vmem)


def _silu(x):
    return x * jax.nn.sigmoid(x)


def _norm_mod(x, g, shift, scale):
    ms = jnp.mean(x * x, axis=-1, keepdims=True)
    return (x * lax.rsqrt(ms + EPS)) * (g * (1.0 + scale)) + shift


def _dot(a, b):
    return jnp.dot(a, b, preferred_element_type=F32)


def _dot_nt(a, b):
    return lax.dot_general(a, b, (((1,), (1,)), ((), ())), preferred_element_type=F32)


def _split_rhs_dot(m, x, parts):
    acc = None
    r = x
    for _ in range(parts):
        hi = r.astype(BF16)
        t = _dot(m, hi)
        acc = t if acc is None else acc + t
        r = r - hi.astype(F32)
    return acc


def _split_lhs_dot(x, m, parts):
    acc = None
    r = x
    for _ in range(parts):
        hi = r.astype(BF16)
        t = _dot(hi, m)
        acc = t if acc is None else acc + t
        r = r - hi.astype(F32)
    return acc


def _mod_kernel(s_ref, w_ref, b_ref, o_ref):
    s = _silu(s_ref[...])
    o_ref[...] = _dot(s.astype(BF16), w_ref[...].astype(BF16)) + b_ref[...]


def _mods(c, c_ctx, w_mod, b_mod):
    depth = w_mod.shape[0]
    s = jnp.zeros((8, D_MODEL), F32).at[0].set(c_ctx).at[1].set(c[0])
    tn = 1024
    out = pl.pallas_call(
        _mod_kernel,
        grid=(depth, N_MOD * D_MODEL // tn),
        in_specs=[pl.BlockSpec((8, D_MODEL), lambda l, j: (0, 0)),
                  pl.BlockSpec((None, D_MODEL, tn), lambda l, j: (l, 0, j)),
                  pl.BlockSpec((None, 1, tn), lambda l, j: (l, 0, j))],
        out_specs=pl.BlockSpec((None, 8, tn), lambda l, j: (l, 0, j)),
        out_shape=jax.ShapeDtypeStruct((depth, 8, N_MOD * D_MODEL), F32),
        compiler_params=_params(("parallel", "parallel")),
        name="mods",
    )(s, w_mod, b_mod.reshape(depth, 1, N_MOD * D_MODEL))
    return out[:, :2].reshape(depth, 2, N_MOD, D_MODEL)


def _mod_spec(layer):
    return pl.BlockSpec((None, None, N_MOD, D_MODEL), lambda i, *_: (layer, jnp.minimum(i, 1), 0, 0))


def _ffn_kernel(x_ref, mod_ref, g_ref, wa_ref, wb_ref, wo_ref, o_ref, h_ref, acc_ref, *, sub, nj):
    j = pl.program_id(1)

    def step(first, last):
        if first:
            h = _norm_mod(x_ref[...], g_ref[...], mod_ref[3 * sub:3 * sub + 1, :],
                          mod_ref[3 * sub + 1:3 * sub + 2, :]).astype(BF16)
            h_ref[...] = h
        else:
            h = h_ref[...]
        t = _dot((_silu(_dot(h, wa_ref[...])) * _dot(h, wb_ref[...])).astype(BF16), wo_ref[...])
        acc = t if first else acc_ref[...] + t
        if last:
            o_ref[...] = x_ref[...] + (0.5 * mod_ref[3 * sub + 2:3 * sub + 3, :]) * acc
        else:
            acc_ref[...] = acc

    pl.when(j == 0)(lambda: step(True, False))
    pl.when(jnp.logical_and(j > 0, j < nj - 1))(lambda: step(False, False))
    pl.when(j == nj - 1)(lambda: step(False, True))


def _ffn(x, mods, g, w_in, w_out, layer, sub, out_rows=None, tf=512):
    n = x.shape[0]
    nj = D_FF // tf
    half = sub // 2
    skip = 0 if out_rows is None else (n - out_rows) // TM
    return pl.pallas_call(
        functools.partial(_ffn_kernel, sub=sub, nj=nj),
        grid=(n // TM, nj),
        in_specs=[pl.BlockSpec((TM, D_MODEL), lambda i, j: (i, 0)),
                  _mod_spec(layer),
                  pl.BlockSpec((1, D_MODEL), lambda i, j: (0, 0)),
                  pl.BlockSpec((None, None, D_MODEL, tf), lambda i, j: (layer, half, 0, j)),
                  pl.BlockSpec((None, None, D_MODEL, tf), lambda i, j: (layer, half, 0, j + nj)),
                  pl.BlockSpec((None, None, tf, D_MODEL), lambda i, j: (layer, half, j, 0))],
        out_specs=pl.BlockSpec((TM, D_MODEL), lambda i, j: (jnp.maximum(i - skip, 0), 0)),
        out_shape=jax.ShapeDtypeStruct((n - skip * TM, D_MODEL), F32),
        scratch_shapes=[pltpu.VMEM((TM, D_MODEL), BF16), pltpu.VMEM((TM, D_MODEL), F32)],
        compiler_params=_params(("arbitrary", "arbitrary")),
        name="ffn",
    )(x, mods, g, w_in, w_in, w_out)


PROJ_WIDTHS = (SSD_INNER, SSD_XBC, DT_PAD, S5_WIDTH, ATT_WIDTH + 2 * ATT_KV_WIDTH, ATT_WIDTH + 2 * ATT_KV_WIDTH)
W_IN_BOUNDS = tuple(zip(
    np.cumsum((0, SSD_INNER, SSD_XBC, 2 * SSD_HEADS, S5_WIDTH, ATT_WIDTH + 2 * ATT_KV_WIDTH, ATT_WIDTH + 2 * ATT_KV_WIDTH)),
    np.cumsum((SSD_INNER, SSD_XBC, 2 * SSD_HEADS, S5_WIDTH, ATT_WIDTH + 2 * ATT_KV_WIDTH, ATT_WIDTH + 2 * ATT_KV_WIDTH,
               N_BRANCH * D_MODEL))))


def _split_w_in_kernel(w_ref, *o_refs):
    w = w_ref[...]
    for o_ref, (lo, hi) in zip(o_refs, W_IN_BOUNDS):
        piece = w[:, int(lo):int(hi)].astype(BF16)
        if o_ref.shape[1] > piece.shape[1]:
            piece = jnp.concatenate([piece, jnp.zeros((piece.shape[0], o_ref.shape[1] - piece.shape[1]), BF16)], axis=1)
        o_ref[...] = piece


def _split_w_in(w_in):
    depth, d, d_in = w_in.shape
    tr = 128
    widths = PROJ_WIDTHS + (N_BRANCH * D_MODEL,)
    return pl.pallas_call(
        _split_w_in_kernel,
        grid=(depth, d // tr),
        in_specs=[pl.BlockSpec((None, tr, d_in), lambda l, r: (l, r, 0))],
        out_specs=[pl.BlockSpec((None, tr, w), lambda l, r: (l, r, 0)) for w in widths],
        out_shape=[jax.ShapeDtypeStruct((depth, d, w), BF16) for w in widths],
        compiler_params=_params(("parallel", "parallel")),
        name="split_w_in",
    )(w_in)


def _proj_kernel(x_ref, mod_ref, g_ref, *refs):
    nw = len(PROJ_WIDTHS)
    w_refs, h_ref, o_refs = refs[:nw], refs[nw], refs[nw + 1:]
    h = _norm_mod(x_ref[...], g_ref[...], mod_ref[3:4, :], mod_ref[4:5, :]).astype(BF16)
    h_ref[...] = h
    for w_ref, o_ref in zip(w_refs, o_refs):
        o_ref[...] = _dot(h, w_ref[...])


def _proj(x, mods, g, ws, layer):
    n = x.shape[0]
    tm = PROJ_TM
    row = lambda w: pl.BlockSpec((tm, w), lambda i: (i, 0))
    return pl.pallas_call(
        _proj_kernel,
        grid=(n // tm,),
        in_specs=[row(D_MODEL),
                  pl.BlockSpec((None, None, N_MOD, D_MODEL), lambda i: (layer, jnp.minimum(i // (TM // tm), 1), 0, 0)),
                  pl.BlockSpec((1, D_MODEL), lambda i: (0, 0))]
                 + [pl.BlockSpec((None, D_MODEL, w), lambda i: (layer, 0, 0), pipeline_mode=pl.Buffered(1))
                    for w in PROJ_WIDTHS],
        out_specs=[row(D_MODEL)] + [row(w) for w in PROJ_WIDTHS],
        out_shape=[jax.ShapeDtypeStruct((n, D_MODEL), BF16)]
                  + [jax.ShapeDtypeStruct((n, w), F32) for w in PROJ_WIDTHS],
        compiler_params=_params(("parallel",)),
        name="proj",
    )(x, mods, g, *ws)


QK_W = ATT_WIDTH + ATT_KV_WIDTH


def _attn_prep_kernel(qkv_ref, g_ref, cos_ref, sin_ref, bd_ref, q_ref, kt_ref, v_ref):
    x = qkv_ref[...]
    qk = x[:, :QK_W]
    sq = qk * qk
    hi = sq.astype(BF16)
    lo = (sq - hi.astype(F32)).astype(BF16)
    bd = bd_ref[...]
    ms = jnp.concatenate([_dot(hi[:, t * LANES:(t + 1) * LANES], bd) + _dot(lo[:, t * LANES:(t + 1) * LANES], bd)
                          for t in range(QK_W // LANES)], axis=1)
    y = qk * lax.rsqrt(ms + EPS) * g_ref[...]
    cos = cos_ref[...]
    sin = sin_ref[...]
    lane = lax.broadcasted_iota(jnp.int32, (x.shape[0], LANES), 1)
    first_half = (lane % HEAD_DIM) < (HEAD_DIM // 2)
    tiles = []
    for t in range(QK_W // LANES):
        yt = y[:, t * LANES:(t + 1) * LANES]
        rot = jnp.where(first_half, pltpu.roll(yt, LANES - HEAD_DIM // 2, 1), pltpu.roll(yt, HEAD_DIM // 2, 1))
        tiles.append(yt * cos + rot * sin)
    for h in range(ATT_Q_HEADS):
        t = tiles[h // 2]
        q_ref[h] = (t[:, (h % 2) * HEAD_DIM:(h % 2 + 1) * HEAD_DIM] * ATT_SCALE).astype(BF16)
    kt = tiles[ATT_WIDTH // LANES].T
    kt_ref[0] = kt[:HEAD_DIM].astype(BF16)
    kt_ref[1] = kt[HEAD_DIM:].astype(BF16)
    v = x[:, QK_W:QK_W + LANES]
    one_col = jnp.where(lane == HEAD_DIM, 1.0, 0.0)
    v_ref[0] = jnp.where(lane < HEAD_DIM, v, one_col).astype(BF16)
    v_ref[1] = jnp.where(lane < HEAD_DIM, pltpu.roll(v, HEAD_DIM, 1), one_col).astype(BF16)


def _attn_prep(qkv, g640, cos_t, sin_t, bd):
    s = cos_t.shape[0]
    tm = PROJ_TM
    skip = TM // tm - CTX_LEN // tm
    src = lambda i: jnp.where(i < CTX_LEN // tm, i, i + skip)
    return pl.pallas_call(
        _attn_prep_kernel,
        grid=(s // tm,),
        in_specs=[pl.BlockSpec((tm, ATT_WIDTH + 2 * ATT_KV_WIDTH), lambda i: (src(i), 0)),
                  pl.BlockSpec((1, QK_W), lambda i: (0, 0)),
                  pl.BlockSpec((tm, LANES), lambda i: (i, 0)),
                  pl.BlockSpec((tm, LANES), lambda i: (i, 0)),
                  pl.BlockSpec((LANES, LANES), lambda i: (0, 0))],
        out_specs=[pl.BlockSpec((ATT_Q_HEADS, tm, HEAD_DIM), lambda i: (0, i, 0)),
                   pl.BlockSpec((ATT_KV_HEADS, HEAD_DIM, tm), lambda i: (0, 0, i)),
                   pl.BlockSpec((ATT_KV_HEADS, tm, LANES), lambda i: (0, i, 0))],
        out_shape=[jax.ShapeDtypeStruct((ATT_Q_HEADS, s, HEAD_DIM), BF16),
                   jax.ShapeDtypeStruct((ATT_KV_HEADS, HEAD_DIM, s), BF16),
                   jax.ShapeDtypeStruct((ATT_KV_HEADS, s, LANES), BF16)],
        compiler_params=_params(("parallel",)),
        name="attn_prep",
    )(qkv, g640, cos_t, sin_t, bd)


def _heads_out(acc, rows):
    outs = []
    for h in range(GQA_GROUP):
        a = acc[h * rows:(h + 1) * rows]
        outs.append(a[:, :HEAD_DIM] / a[:, HEAD_DIM:HEAD_DIM + 1])
    return jnp.concatenate(outs, axis=1)


def _glb_kernel(q_ref, kt_ref, v_ref, o_ref, *, nk):
    tq = q_ref.shape[1]
    m_rows = GQA_GROUP * tq
    q = q_ref[...].reshape(m_rows, HEAD_DIM)

    def body(c, carry):
        m, acc = carry
        s = _dot(q, kt_ref[c])
        m_new = jnp.maximum(m, jnp.max(s, axis=1, keepdims=True))
        p = jnp.exp(s - m_new)
        acc = jnp.exp(m - m_new) * acc + _dot(p.astype(BF16), v_ref[c])
        return m_new, acc

    m0 = jnp.full((m_rows, 1), NEG_INF, F32)
    acc0 = jnp.zeros((m_rows, LANES), F32)
    _, acc = lax.fori_loop(0, nk, body, (m0, acc0), unroll=True)
    o_ref[...] = _heads_out(acc, tq).astype(BF16)


def _glb_attn(q, kt, v, n_rows):
    s = q.shape[1]
    nk, tk = kt.shape[1], kt.shape[3]
    tq = GLB_TQ
    nq = (s - CTX_LEN) // tq
    half = GQA_GROUP * HEAD_DIM
    return pl.pallas_call(
        functools.partial(_glb_kernel, nk=nk),
        grid=(ATT_KV_HEADS, nq),
        in_specs=[pl.BlockSpec((GQA_GROUP, tq, HEAD_DIM), lambda kv, i: (kv, i + CTX_LEN // tq, 0)),
                  pl.BlockSpec((None, nk, HEAD_DIM, tk), lambda kv, i: (kv, 0, 0, 0)),
                  pl.BlockSpec((None, nk, tk, LANES), lambda kv, i: (kv, 0, 0, 0))],
        out_specs=pl.BlockSpec((tq, half), lambda kv, i: (i + TM // tq, kv)),
        out_shape=jax.ShapeDtypeStruct((n_rows, ATT_WIDTH), BF16),
        compiler_params=_params(("arbitrary", "arbitrary")),
        name="glb_attn",
    )(q, kt, v)


def _swa_bias():
    w, tq = SWA_WINDOW, SWA_TQ
    nkw = tq + 2 * w
    qi = np.arange(tq)[:, None]
    kj = np.arange(nkw)[None, :]
    ok = (kj >= qi) & (kj <= qi + 2 * w)
    oks = np.stack([ok, ok & (kj >= w), ok & (kj < nkw - w)])
    return jnp.asarray(np.where(oks, 0.0, NEG_INF).astype(np.float32))


def _swa_kernel(sink_ref, bias_ref, q_ref, *refs, nt):
    nkb = SWA_TQ // SWA_WINDOW + 2
    kt_refs, ktx_ref = refs[:nkb], refs[nkb]
    v_refs, vx_ref, o_ref = refs[nkb + 1:2 * nkb + 1], refs[2 * nkb + 1], refs[2 * nkb + 2]
    n = pl.program_id(0)
    tq = SWA_TQ
    rows = GQA_GROUP * tq
    bias = bias_ref[jnp.where(n == 0, 1, jnp.where(n == nt - 1, 2, 0))]
    bias = jnp.concatenate([bias] * GQA_GROUP, axis=0)
    outs = []
    for kv in range(ATT_KV_HEADS):
        q = q_ref[kv * GQA_GROUP:(kv + 1) * GQA_GROUP].reshape(rows, HEAD_DIM)
        kt = jnp.concatenate([r[kv] for r in kt_refs], axis=1)
        s = jnp.concatenate([_dot(q, kt) + bias, _dot(q, ktx_ref[kv])], axis=1)
        sink = jnp.concatenate([jnp.full((tq, 1), sink_ref[kv * GQA_GROUP + h], F32) for h in range(GQA_GROUP)], axis=0)
        m = jnp.maximum(jnp.max(s, axis=1, keepdims=True), sink)
        v = jnp.concatenate([r[kv] for r in v_refs] + [vx_ref[kv]], axis=0)
        acc = _dot(jnp.exp(s - m).astype(BF16), v)
        lane = lax.broadcasted_iota(jnp.int32, acc.shape, 1)
        acc = acc + jnp.where(lane == HEAD_DIM, jnp.exp(sink - m), 0.0)
        outs.append(_heads_out(acc, tq))
    o_ref[...] = jnp.concatenate(outs, axis=1).astype(BF16)


def _swa_attn(q, kt, v, sink, n_rows):
    s = q.shape[1]
    w = SWA_WINDOW
    tq = SWA_TQ
    nb = (s - CTX_LEN) // w
    nt = (s - CTX_LEN) // tq
    nkb = tq // w + 2
    c0 = CTX_LEN // w
    blk = lambda b: (lambda n: jnp.clip(n * (tq // w) - 1 + b, 0, nb - 1) + c0)
    kt_spec = lambda f: pl.BlockSpec((ATT_KV_HEADS, HEAD_DIM, w), lambda n: (0, 0, f(n)))
    v_spec = lambda f: pl.BlockSpec((ATT_KV_HEADS, w, LANES), lambda n: (0, f(n), 0))
    return pl.pallas_call(
        functools.partial(_swa_kernel, nt=nt),
        grid=(nt,),
        in_specs=[pl.BlockSpec(memory_space=pltpu.SMEM),
                  pl.BlockSpec((3, tq, nkb * w), lambda n: (0, 0, 0)),
                  pl.BlockSpec((ATT_Q_HEADS, tq, HEAD_DIM), lambda n: (0, n + CTX_LEN // tq, 0))]
                 + [kt_spec(blk(b)) for b in range(nkb)]
                 + [pl.BlockSpec((ATT_KV_HEADS, HEAD_DIM, CTX_LEN), lambda n: (0, 0, 0))]
                 + [v_spec(blk(b)) for b in range(nkb)]
                 + [pl.BlockSpec((ATT_KV_HEADS, CTX_LEN, LANES), lambda n: (0, 0, 0))],
        out_specs=pl.BlockSpec((tq, ATT_WIDTH), lambda n: (n + TM // tq, 0)),
        out_shape=jax.ShapeDtypeStruct((n_rows, ATT_WIDTH), BF16),
        compiler_params=_params(("parallel",)),
        name="swa_attn",
    )(sink, _swa_bias(), q, *([kt] * (nkb + 1)), *([v] * (nkb + 1)))


def _ctx_attn_kernel(sink_ref, q_ref, kt_ref, v_ref, prev_ref, o_ref, *, use_sink):
    del prev_ref
    rows = GQA_GROUP * CTX_LEN
    outs = []
    for kv in range(ATT_KV_HEADS):
        q = q_ref[kv * GQA_GROUP:(kv + 1) * GQA_GROUP].reshape(rows, HEAD_DIM)
        s = _dot(q, kt_ref[kv])
        m = jnp.max(s, axis=1, keepdims=True)
        if use_sink:
            sink = jnp.concatenate([jnp.full((CTX_LEN, 1), sink_ref[kv * GQA_GROUP + h], F32)
                                    for h in range(GQA_GROUP)], axis=0)
            m = jnp.maximum(m, sink)
        acc = _dot(jnp.exp(s - m).astype(BF16), v_ref[kv])
        if use_sink:
            lane = lax.broadcasted_iota(jnp.int32, acc.shape, 1)
            acc = acc + jnp.where(lane == HEAD_DIM, jnp.exp(sink - m), 0.0)
        outs.append(_heads_out(acc, CTX_LEN))
    o_ref[:CTX_LEN] = jnp.concatenate(outs, axis=1).astype(BF16)
    o_ref[CTX_LEN:] = jnp.zeros((TM - CTX_LEN, ATT_WIDTH), BF16)


def _ctx_attn(q, kt, v, sink, y_prev, use_sink):
    return pl.pallas_call(
        functools.partial(_ctx_attn_kernel, use_sink=use_sink),
        grid=(1,),
        in_specs=[pl.BlockSpec(memory_space=pltpu.SMEM),
                  pl.BlockSpec((ATT_Q_HEADS, CTX_LEN, HEAD_DIM), lambda i: (0, 0, 0)),
                  pl.BlockSpec((ATT_KV_HEADS, HEAD_DIM, CTX_LEN), lambda i: (0, 0, 0)),
                  pl.BlockSpec((ATT_KV_HEADS, CTX_LEN, LANES), lambda i: (0, 0, 0)),
                  pl.BlockSpec(memory_space=pl.ANY)],
        out_specs=pl.BlockSpec((TM, ATT_WIDTH), lambda i: (0, 0)),
        out_shape=jax.ShapeDtypeStruct(y_prev.shape, y_prev.dtype),
        input_output_aliases={4: 0},
        compiler_params=_params(("arbitrary",)),
        name="ctx_attn",
    )(sink, q, kt, v, y_prev)


def _ssd_prep_kernel(xp_ref, xc_ref, xn_ref, dt_ref, w_ref, b_ref, dtb_ref, xo_ref, dto_ref, *, nblk):
    b = pl.program_id(0)
    t = SSD_T
    lat0 = TM // t
    has_prev = jnp.logical_and(b != 0, b != lat0)
    has_next = jnp.logical_and(b != CTX_LEN // t - 1, b != nblk - 1)
    cur = xc_ref[...]
    prev = jnp.where(has_prev, xp_ref[...], 0.0)
    nxt = jnp.where(has_next, xn_ref[...], 0.0)
    row = lax.broadcasted_iota(jnp.int32, cur.shape, 0)
    halo = prev.shape[0]
    rep = lambda v: jnp.concatenate([v] * (t // halo), axis=0)
    w = w_ref[...]
    acc = cur * w[2:3, :] + b_ref[...]
    for k in (1, 2):
        down = jnp.where(row < k, rep(pltpu.roll(prev, k, 0)), pltpu.roll(cur, k, 0))
        up = jnp.where(row >= t - k, rep(pltpu.roll(nxt, halo - k, 0)), pltpu.roll(cur, t - k, 0))
        acc = acc + down * w[2 - k:3 - k, :] + up * w[2 + k:3 + k, :]
    xo_ref[...] = _silu(acc)
    d = dt_ref[...] + dtb_ref[...]
    dto_ref[...] = jnp.maximum(d, 0.0) + jnp.log1p(jnp.exp(-jnp.abs(d)))


def _ssd_prep(xbc, dt, conv_w, conv_b, dt_bias):
    n = xbc.shape[0]
    t = SSD_T
    nblk = n // t
    return pl.pallas_call(
        functools.partial(_ssd_prep_kernel, nblk=nblk),
        grid=(nblk,),
        in_specs=[pl.BlockSpec((8, SSD_XBC), lambda b: (jnp.maximum(b * (t // 8) - 1, 0), 0)),
                  pl.BlockSpec((t, SSD_XBC), lambda b: (b, 0)),
                  pl.BlockSpec((8, SSD_XBC), lambda b: (jnp.minimum((b + 1) * (t // 8), n // 8 - 1), 0)),
                  pl.BlockSpec((t, DT_PAD), lambda b: (b, 0)),
                  pl.BlockSpec((8, SSD_XBC), lambda b: (0, 0)),
                  pl.BlockSpec((1, SSD_XBC), lambda b: (0, 0)),
                  pl.BlockSpec((1, DT_PAD), lambda b: (0, 0))],
        out_specs=[pl.BlockSpec((t, SSD_XBC), lambda b: (b, 0)),
                   pl.BlockSpec((t, DT_PAD), lambda b: (b, 0))],
        out_shape=[jax.ShapeDtypeStruct((n, SSD_XBC), F32), jax.ShapeDtypeStruct((n, DT_PAD), F32)],
        compiler_params=_params(("parallel",)),
        name="ssd_prep",
    )(xbc, xbc, xbc, dt, conv_w, conv_b, dt_bias)


def _ssd_scan_kernel(tbl_ref, x_ref, dt_ref, arow_ref, tri_ref, e_ref, *rest, direction, final):
    if final:
        yf_ref, z_ref, dsk_ref, ng_ref, o_ref, st_ref = rest
    else:
        o_ref, st_ref = rest
    step = pl.program_id(0)
    flag = tbl_ref[1, step]
    t = SSD_T
    gw = SSD_INNER // SSD_GROUPS
    hpg = SSD_HEADS // SSD_GROUPS

    @pl.when(flag == 2)
    def _():
        o_ref[...] = jnp.zeros_like(o_ref)

    @pl.when(flag == 1)
    def _():
        st_ref[...] = jnp.zeros_like(st_ref)

    @pl.when(flag != 2)
    def _():
        x = x_ref[...]
        xs = x[:, :SSD_INNER]
        bm = x[:, SSD_INNER:SSD_INNER + SSD_BC]
        cm = x[:, SSD_INNER + SSD_BC:]
        dt = dt_ref[...]
        adt = dt * arow_ref[...]
        tri = tri_ref[...]
        keep = tri > 0
        cum = _split_rhs_dot(tri, adt, 3)
        tot = cum[t - 1:t, :] if direction == 0 else cum[0:1, :]
        e = e_ref[...]
        dt_e = _split_lhs_dot(dt, e, 2)
        expc_e = _split_lhs_dot(jnp.exp(cum), e, 2)
        decs_e = _split_lhs_dot(jnp.exp(tot - cum), e, 2)
        dch_e = _split_lhs_dot(jnp.broadcast_to(jnp.exp(tot), (8, DT_PAD)), e, 2)[0:1]
        cum_t = cum.T
        xdt = xs * dt_e
        ys = []
        for g in range(SSD_GROUPS):
            bg = bm[:, g * SSD_STATE:(g + 1) * SSD_STATE]
            cgb = cm[:, g * SSD_STATE:(g + 1) * SSD_STATE].astype(BF16)
            cb = _dot_nt(cgb, bg.astype(BF16))
            stg = st_ref[g]
            y_off = _dot(cgb, stg.astype(BF16)) * expc_e[:, g * gw:(g + 1) * gw]
            parts = []
            for j in range(hpg):
                h = g * hpg + j
                r = direction * SSD_HEADS + h
                diff = cum[:, r:r + 1] - cum_t[r:r + 1, :]
                lmat = jnp.exp(jnp.where(keep, diff, NEG_INF))
                wmat = (cb * lmat).astype(BF16)
                parts.append(_dot(wmat, xdt[:, h * SSD_HEAD_DIM:(h + 1) * SSD_HEAD_DIM].astype(BF16)))
            ys.append(jnp.concatenate(parts, axis=1) + y_off)
            xw = (xdt[:, g * gw:(g + 1) * gw] * decs_e[:, g * gw:(g + 1) * gw]).astype(BF16)
            st_ref[g] = stg * dch_e[:, g * gw:(g + 1) * gw] + _dot(bg.T.astype(BF16), xw)
        y = jnp.concatenate(ys, axis=1)
        if final:
            y = yf_ref[...] + y + dsk_ref[...] * xs
            y = y * _silu(z_ref[...])
            ms = jnp.mean(y * y, axis=-1, keepdims=True)
            o_ref[...] = (y * lax.rsqrt(ms + EPS) * ng_ref[...]).astype(BF16)
        else:
            o_ref[...] = y


def _ssd_scan(tbl, xact, dtsp, arow, tri, e, direction, extra=None):
    n = xact.shape[0]
    t = SSD_T
    final = extra is not None
    blk = lambda w: pl.BlockSpec((t, w), lambda s, tb: (tb[0, s], 0))
    const = lambda r, w: pl.BlockSpec((r, w), lambda s, tb: (0, 0))
    in_specs = [blk(SSD_XBC), blk(DT_PAD), const(1, DT_PAD), const(t, t), const(DT_PAD, SSD_INNER)]
    args = [xact, dtsp, arow, tri, e]
    if final:
        in_specs += [blk(SSD_INNER), blk(SSD_INNER), const(1, SSD_INNER), const(1, SSD_INNER)]
        args += list(extra)
    return pl.pallas_call(
        functools.partial(_ssd_scan_kernel, direction=direction, final=final),
        grid_spec=pltpu.PrefetchScalarGridSpec(
            num_scalar_prefetch=1, grid=(tbl.shape[1],), in_specs=in_specs,
            out_specs=blk(SSD_INNER),
            scratch_shapes=[pltpu.VMEM((SSD_GROUPS, SSD_STATE, SSD_INNER // SSD_GROUPS), F32)]),
        out_shape=jax.ShapeDtypeStruct((n, SSD_INNER), BF16 if final else F32),
        compiler_params=_params(("arbitrary",)),
        name="ssd_scan_bwd" if final else "ssd_scan_fwd",
    )(tbl, *args)


def _ssd_tables(n_rows):
    t = SSD_T
    ctx = list(range(CTX_LEN // t))
    pad = list(range(CTX_LEN // t, TM // t))
    lat = list(range(TM // t, n_rows // t))
    fwd = ctx + lat + pad
    bwd = ctx[::-1] + lat[::-1] + pad
    flags = [1] + [0] * (len(ctx) + len(lat) - 1) + [2] * len(pad)
    return (jnp.asarray(np.array([fwd, flags], np.int32)), jnp.asarray(np.array([bwd, flags], np.int32)))


def _dot_nt_split(a, b):
    a_hi = a.astype(BF16)
    b_hi = b.astype(BF16)
    a_lo = (a - a_hi.astype(F32)).astype(BF16)
    b_lo = (b - b_hi.astype(F32)).astype(BF16)
    return _dot_nt(a_hi, b_hi) + _dot_nt(a_hi, b_lo) + _dot_nt(a_lo, b_hi)


def _s5_setup_kernel(pa_ref, pb_ref, pc_ref, tile_ref, m_ref, bs_ref, csa_ref, csb_ref, lre_ref, lim_ref):
    t_len = S5_T
    hh = S5_GROUP
    a_re = pa_ref[0:1, :]
    a_im = pa_ref[1:2, :]
    dt = jnp.exp(pa_ref[2:3, :])
    mag = jnp.exp(a_re * dt)
    lr = mag * jnp.cos(a_im * dt)
    li = mag * jnp.sin(a_im * dt)
    den = a_re * a_re + a_im * a_im
    nr = lr - 1.0
    f_re = (nr * a_re + li * a_im) / den
    f_im = (li * a_re - nr * a_im) / den
    b_re, b_im = pb_ref[0:hh], pb_ref[hh:2 * hh]
    c_re, c_im = pc_ref[0:hh], pc_ref[hh:2 * hh]
    bb_re = f_re * b_re - f_im * b_im
    bb_im = f_re * b_im + f_im * b_re
    pw = [(jnp.ones_like(lr), jnp.zeros_like(lr))]
    for _ in range(t_len):
        pr, pi = pw[-1]
        pw.append((pr * lr - pi * li, pr * li + pi * lr))
    fwd = lax.broadcasted_iota(jnp.int32, lr.shape, 1) < S5_STATE
    pick = lambda kf, kb: (jnp.where(fwd, pw[kf][0], pw[kb][0]), jnp.where(fwd, pw[kf][1], pw[kb][1]))
    bs_re, bs_im, cs_re, cs_im = [], [], [], []
    for t in range(t_len):
        er, ei = pick(t_len - 1 - t, t)
        bs_re.append(er * bb_re - ei * bb_im)
        bs_im.append(er * bb_im + ei * bb_re)
        fr, fi = pick(t + 1, t_len - t)
        cs_re.append(c_re * fr - c_im * fi)
        cs_im.append(c_re * fi + c_im * fr)
    bs_full = jnp.concatenate([jnp.concatenate(bs_re, axis=0), jnp.concatenate(bs_im, axis=0)], axis=1)
    cs_full = jnp.concatenate([jnp.concatenate(cs_re, axis=0), -jnp.concatenate(cs_im, axis=0)], axis=1)
    wide = bs_full.shape
    fwd_w = (lax.broadcasted_iota(jnp.int32, wide, 1) % (2 * S5_STATE)) < S5_STATE
    bs_ref[...] = bs_full.astype(BF16)
    csa_ref[...] = jnp.where(fwd_w, cs_full, 0.0).astype(BF16)
    csb_ref[...] = jnp.where(fwd_w, 0.0, cs_full).astype(BF16)
    lre_ref[...] = pw[t_len][0]
    lim_ref[...] = pw[t_len][1]
    cc = jnp.concatenate([c_re, -c_im], axis=1)
    cc = jnp.concatenate([cc, jnp.zeros((LANES - hh, wide[1]), F32)], axis=0)
    fwd_c = (lax.broadcasted_iota(jnp.int32, cc.shape, 1) % (2 * S5_STATE)) < S5_STATE
    kr_f = _dot_nt_split(bs_full, jnp.where(fwd_c, cc, 0.0))
    k_b = _dot_nt_split(bs_full, jnp.where(fwd_c, 0.0, cc))
    tile = tile_ref[...]
    kr_w = _split_lhs_dot(kr_f, tile, 3)
    kb_w = _split_lhs_dot(k_b, tile, 3)
    lane_blk = lax.broadcasted_iota(jnp.int32, wide, 1) // hh
    m = jnp.zeros(wide, F32)
    for t in range(t_len):
        up, dn = kr_w, kb_w
        if t < t_len - 1:
            r = (t_len - 1 - t) * hh
            up = jnp.concatenate([kr_w[r:], jnp.zeros((r, wide[1]), F32)], axis=0)
        if t > 0:
            r = t * hh
            dn = jnp.concatenate([jnp.zeros((r, wide[1]), F32), kb_w[:wide[0] - r]], axis=0)
        m = m + jnp.where(lane_blk == t, up + dn, 0.0)
    m_ref[...] = m.astype(BF16)


def _s5_setup(a_re, a_im, log_dt, b_re, b_im, c_re, c_im):
    g = a_re.shape[1]
    tw = S5_T * S5_GROUP
    both = lambda v: jnp.concatenate([v[0], v[1]], axis=-1)
    ldt = jnp.broadcast_to(log_dt[:, :, None], (2, g, S5_STATE))
    pa = jnp.stack([both(a_re), both(a_im), both(ldt)], axis=1).astype(F32)
    pa = jnp.pad(pa, ((0, 0), (0, 5), (0, 0)))
    dup = lambda v: jnp.concatenate([v, v], axis=-1)
    pb = jnp.concatenate([dup(jnp.swapaxes(b_re, 1, 2)), dup(jnp.swapaxes(b_im, 1, 2))], axis=1).astype(F32)
    pc = jnp.concatenate([dup(c_re), dup(c_im)], axis=1).astype(F32)
    tile = jnp.asarray(np.tile(np.eye(LANES, S5_GROUP).astype(np.float32), (1, S5_T)), BF16)
    per_g = lambda r, c: pl.BlockSpec((None, r, c), lambda i: (i, 0, 0))
    mat = jax.ShapeDtypeStruct((g, tw, tw), BF16)
    lam = jax.ShapeDtypeStruct((g, 1, 2 * S5_STATE), F32)
    m, bs, csa, csb, lre, lim = pl.pallas_call(
        _s5_setup_kernel,
        grid=(g,),
        in_specs=[per_g(8, 2 * S5_STATE), per_g(2 * S5_GROUP, 2 * S5_STATE), per_g(2 * S5_GROUP, 2 * S5_STATE),
                  pl.BlockSpec((LANES, tw), lambda i: (0, 0))],
        out_specs=[per_g(tw, tw)] * 4 + [per_g(1, 2 * S5_STATE)] * 2,
        out_shape=[mat] * 4 + [lam] * 2,
        compiler_params=_params(("parallel",)),
        name="s5_setup",
    )(pa, pb, pc, tile)
    return m, bs, csa, csb, lre[:, 0], lim[:, 0]


def _s5_kernel(u_ref, m_ref, bs_ref, csa_ref, csb_ref, lre_ref, lim_ref, y_ref, v_ref, sa_ref, sb_ref, *, nctx, npad, nch):
    gb = S5_GB
    for j in range(gb):
        v = _dot(u_ref[j], bs_ref[j])
        v_ref.at[0][pl.ds(j, nch, stride=gb), :] = v[:, :LANES]
        v_ref.at[1][pl.ds(j, nch, stride=gb), :] = v[:, LANES:]
    lre = lre_ref[0]
    lim = lim_ref[0]
    is_fwd = lax.broadcasted_iota(jnp.int32, (gb, LANES), 1) < S5_STATE

    nlat = nch - nctx - npad

    def body(i, carry):
        sre, sim = carry
        tail = i - nlat
        rf = jnp.where(i < nctx, i, jnp.where(i < nctx + nlat, i + npad, tail)) * gb
        rb = jnp.where(i < nctx, nctx - 1 - i, jnp.where(i < nctx + nlat, nch + nctx - 1 - i, tail)) * gb
        sa_ref[0, pl.ds(rf, gb), :] = sre
        sa_ref[1, pl.ds(rf, gb), :] = sim
        sb_ref[0, pl.ds(rb, gb), :] = sre
        sb_ref[1, pl.ds(rb, gb), :] = sim
        vre = jnp.where(is_fwd, v_ref[0, pl.ds(rf, gb), :], v_ref[0, pl.ds(rb, gb), :])
        vim = jnp.where(is_fwd, v_ref[1, pl.ds(rf, gb), :], v_ref[1, pl.ds(rb, gb), :])
        return lre * sre - lim * sim + vre, lre * sim + lim * sre + vim

    zero = jnp.zeros((gb, LANES), F32)
    lax.fori_loop(0, nch, body, (zero, zero))
    for j in range(gb):
        rows = pl.ds(j, nch, stride=gb)
        sa = jnp.concatenate([sa_ref.at[0][rows, :], sa_ref.at[1][rows, :]], axis=1).astype(BF16)
        sb = jnp.concatenate([sb_ref.at[0][rows, :], sb_ref.at[1][rows, :]], axis=1).astype(BF16)
        y_ref[j] = _dot(u_ref[j], m_ref[j]) + _dot_nt(sa, csa_ref[j]) + _dot_nt(sb, csb_ref[j])


def _s5_scan(ug, m, bs, csa, csb, lam_re, lam_im):
    g, nch, tw = ug.shape
    gb = S5_GB
    sw = 4 * S5_STATE
    blk = lambda a, b: pl.BlockSpec((gb, a, b), lambda i: (i, 0, 0))
    lam_spec = pl.BlockSpec((1, gb, 2 * S5_STATE), lambda i: (i, 0, 0))
    return pl.pallas_call(
        functools.partial(_s5_kernel, nctx=CTX_LEN // S5_T, npad=(TM - CTX_LEN) // S5_T, nch=nch),
        grid=(g // gb,),
        in_specs=[blk(nch, tw), blk(tw, tw), blk(tw, sw), blk(tw, sw), blk(tw, sw), lam_spec, lam_spec],
        out_specs=blk(nch, tw),
        out_shape=jax.ShapeDtypeStruct((g, nch, tw), F32),
        scratch_shapes=[pltpu.VMEM((2, nch * gb, LANES), F32) for _ in range(3)],
        compiler_params=_params(("parallel",)),
        name="s5_scan",
    )(ug, m, bs, csa, csb, lam_re.reshape(g // gb, gb, -1), lam_im.reshape(g // gb, gb, -1))


def _s5_select():
    gpt = LANES // S5_GROUP
    sel = np.zeros((gpt, S5_T * LANES, S5_T * S5_GROUP), np.float32)
    for q in range(gpt):
        for t in range(S5_T):
            for h in range(S5_GROUP):
                sel[q, t * LANES + q * S5_GROUP + h, t * S5_GROUP + h] = 1.0
    return jnp.asarray(sel, BF16)


def _s5_gather_kernel(u_ref, sel_ref, o_ref):
    rb = o_ref.shape[1]
    u2 = jnp.concatenate([u_ref[pl.ds(t, rb, stride=S5_T), :] for t in range(S5_T)], axis=1).astype(BF16)
    for q in range(LANES // S5_GROUP):
        o_ref[q] = _dot(u2, sel_ref[q]).astype(BF16)


def _s5_gather(u, sel):
    n = u.shape[0]
    gpt = LANES // S5_GROUP
    rb = n // S5_T // 4
    tw = S5_T * S5_GROUP
    return pl.pallas_call(
        _s5_gather_kernel,
        grid=(S5_WIDTH // LANES, 4),
        in_specs=[pl.BlockSpec((rb * S5_T, LANES), lambda j, r: (r, j)),
                  pl.BlockSpec((gpt, S5_T * LANES, tw), lambda j, r: (0, 0, 0), pipeline_mode=pl.Buffered(1))],
        out_specs=pl.BlockSpec((gpt, rb, tw), lambda j, r: (j, r, 0)),
        out_shape=jax.ShapeDtypeStruct((S5_GROUPS, n // S5_T, tw), BF16),
        compiler_params=_params(("parallel", "parallel")),
        name="s5_gather",
    )(u, sel)


def _s5_scatter_kernel(y_ref, sel_ref, o_ref):
    rb = y_ref.shape[1]
    acc = None
    for q in range(LANES // S5_GROUP):
        y = y_ref[q]
        hi = y.astype(BF16)
        lo = (y - hi.astype(F32)).astype(BF16)
        part = _dot_nt(hi, sel_ref[q]) + _dot_nt(lo, sel_ref[q])
        acc = part if acc is None else acc + part
    for t in range(S5_T):
        o_ref[pl.ds(t, rb, stride=S5_T), :] = acc[:, t * LANES:(t + 1) * LANES]


def _s5_scatter(yg, sel):
    g, nchp, tw = yg.shape
    gpt = LANES // S5_GROUP
    rb = nchp // 4
    return pl.pallas_call(
        _s5_scatter_kernel,
        grid=(S5_WIDTH // LANES, 4),
        in_specs=[pl.BlockSpec((gpt, rb, tw), lambda j, r: (j, r, 0)),
                  pl.BlockSpec((gpt, S5_T * LANES, tw), lambda j, r: (0, 0, 0), pipeline_mode=pl.Buffered(1))],
        out_specs=pl.BlockSpec((rb * S5_T, LANES), lambda j, r: (r, j)),
        out_shape=jax.ShapeDtypeStruct((nchp * S5_T, S5_WIDTH), F32),
        compiler_params=_params(("parallel", "parallel")),
        name="s5_scatter",
    )(yg, sel)


def _s5_glu_kernel(y_ref, u_ref, d_ref, w_ref, b_ref, o_ref):
    y = jax.nn.gelu(y_ref[...] + d_ref[...] * u_ref[...])
    t = _dot(y.astype(BF16), w_ref[...]) + b_ref[...]
    o_ref[...] = (t[:, :S5_WIDTH] * jax.nn.sigmoid(t[:, S5_WIDTH:])).astype(BF16)


def _s5_glu(y, u, d, w, b, layer):
    n = y.shape[0]
    row = lambda wd: pl.BlockSpec((TM, wd), lambda i: (i, 0))
    return pl.pallas_call(
        _s5_glu_kernel,
        grid=(n // TM,),
        in_specs=[row(S5_WIDTH), row(S5_WIDTH), pl.BlockSpec((1, S5_WIDTH), lambda i: (0, 0)),
                  pl.BlockSpec((None, S5_WIDTH, 2 * S5_WIDTH), lambda i: (layer, 0, 0)),
                  pl.BlockSpec((1, 2 * S5_WIDTH), lambda i: (0, 0))],
        out_specs=row(S5_WIDTH),
        out_shape=jax.ShapeDtypeStruct((n, S5_WIDTH), BF16),
        compiler_params=_params(("parallel",)),
        name="s5_glu",
    )(y, u, d, w, b)


BR_WIDTHS = (SSD_INNER, S5_WIDTH, ATT_WIDTH, ATT_WIDTH)


def _merge_kernel(h_ref, *refs):
    y_refs, wg_refs, wb_refs, o_ref = refs[:4], refs[4:8], refs[8:12], refs[12]
    h = h_ref[...]
    acc = None
    for y_ref, wg_ref, wb_ref in zip(y_refs, wg_refs, wb_refs):
        t = jax.nn.sigmoid(_dot(h, wg_ref[...])) * _dot(y_ref[...], wb_ref[...])
        acc = t if acc is None else acc + t
    o_ref[...] = acc.astype(BF16)


def _merge(hb, ys, w_gate, w_brs, layer, tn=512):
    n = hb.shape[0]
    nj = D_MODEL // tn
    row = lambda w: pl.BlockSpec((TM, w), lambda i, j: (i, 0))
    return pl.pallas_call(
        _merge_kernel,
        grid=(n // TM, nj),
        in_specs=[row(D_MODEL)] + [row(w) for w in BR_WIDTHS]
                 + [pl.BlockSpec((None, D_MODEL, tn), lambda i, j, b=b: (layer, 0, b * nj + j)) for b in range(N_BRANCH)]
                 + [pl.BlockSpec((None, w, tn), lambda i, j: (layer, 0, j)) for w in BR_WIDTHS],
        out_specs=pl.BlockSpec((TM, tn), lambda i, j: (i, j)),
        out_shape=jax.ShapeDtypeStruct((n, D_MODEL), BF16),
        compiler_params=_params(("parallel", "arbitrary")),
        name="merge",
    )(hb, *ys, w_gate, w_gate, w_gate, w_gate, *w_brs)


def _out_kernel(x_ref, a_ref, mod_ref, w_ref, o_ref):
    o_ref[...] = x_ref[...] + mod_ref[5:6, :] * _dot(a_ref[...], w_ref[...])


def _out_proj(x, acc, mods, w_out, layer):
    n = x.shape[0]
    row = lambda: pl.BlockSpec((TM, D_MODEL), lambda i: (i, 0))
    return pl.pallas_call(
        _out_kernel,
        grid=(n // TM,),
        in_specs=[row(), row(), _mod_spec(layer),
                  pl.BlockSpec((None, D_MODEL, D_MODEL), lambda i: (layer, 0, 0), pipeline_mode=pl.Buffered(1))],
        out_specs=row(),
        out_shape=jax.ShapeDtypeStruct((n, D_MODEL), F32),
        compiler_params=_params(("parallel",)),
        name="out_proj",
    )(x, acc, mods, w_out)


def _rope_tables(seq):
    rows = seq // GRID_W
    row = jnp.repeat(jnp.arange(rows, dtype=F32), GRID_W)
    col = jnp.tile(jnp.arange(GRID_W, dtype=F32), rows)
    n_freq = HEAD_DIM // 4
    inv = ROPE_THETA ** (-jnp.arange(n_freq, dtype=F32) / n_freq)
    ang = jnp.concatenate([row[:, None] * inv, col[:, None] * inv], axis=-1)
    cos, sin = jnp.cos(ang), jnp.sin(ang)
    reps = LANES // HEAD_DIM
    cos_t = jnp.tile(jnp.concatenate([cos, cos], axis=-1), (1, reps))
    sin_t = jnp.tile(jnp.concatenate([-sin, sin], axis=-1), (1, reps))
    cos_t = jnp.concatenate([jnp.ones((CTX_LEN, LANES), F32), cos_t], axis=0)
    sin_t = jnp.concatenate([jnp.zeros((CTX_LEN, LANES), F32), sin_t], axis=0)
    return cos_t, sin_t


def _ssd_expand(direction):
    e = np.zeros((DT_PAD, SSD_INNER), np.float32)
    for h in range(SSD_HEADS):
        e[direction * SSD_HEADS + h, h * SSD_HEAD_DIM:(h + 1) * SSD_HEAD_DIM] = 1.0
    return jnp.asarray(e, BF16)


def _to_stream(ctx_rows, lat_rows):
    pad = jnp.zeros((TM - CTX_LEN,) + ctx_rows.shape[1:], ctx_rows.dtype)
    return jnp.concatenate([ctx_rows, pad, lat_rows], axis=0)


def kernel(x, c, ctx, c_ctx, w_mod, b_mod, norm_g, ffn_in, ffn_out, w_in, ssd_conv_w, ssd_conv_b, ssd_a_log, ssd_dt_bias, ssd_d, ssd_norm_g, s5_a_re, s5_a_im, s5_log_dt, s5_b_re, s5_b_im, s5_c_re, s5_c_im, s5_d, s5_glu_w, s5_glu_b, swa_qk_g, swa_sink, glb_qk_g, w_br_ssd, w_br_s5, w_br_swa, w_br_glb, w_out):
    assert x.shape[0] == 1 and ctx.shape[1] == CTX_LEN
    depth = w_mod.shape[0]
    seq = x.shape[1]
    n_rows = TM + seq
    s_len = CTX_LEN + seq

    ffn_in_b = ffn_in.astype(BF16)
    ffn_out_b = ffn_out.astype(BF16)
    *w_proj, w_gate = _split_w_in(w_in)
    w_brs = tuple(w.astype(BF16) for w in (w_br_ssd, w_br_s5, w_br_swa, w_br_glb))
    w_out_b = w_out.astype(BF16)
    glu_w_b = s5_glu_w.astype(BF16)

    mods = _mods(c, c_ctx, w_mod, b_mod)
    cos_t, sin_t = _rope_tables(seq)
    bd = jnp.asarray(np.kron(np.eye(LANES // HEAD_DIM), np.ones((HEAD_DIM, HEAD_DIM))) / HEAD_DIM, BF16)
    tbl_f, tbl_b = _ssd_tables(n_rows)
    tril = jnp.asarray(np.tril(np.ones((SSD_T, SSD_T), np.float32)), BF16)
    triu = jnp.asarray(np.triu(np.ones((SSD_T, SSD_T), np.float32)), BF16)
    e_dirs = (_ssd_expand(0), _ssd_expand(1))
    s5_sel = _s5_select()
    nk = s_len // GLB_TK

    xs = _to_stream(ctx[0], x[0])
    for i in range(depth):
        xs = _ffn(xs, mods, norm_g[i, 0:1], ffn_in_b, ffn_out_b, i, 0)
        hb, z, xbc, dt, u, qkv_swa, qkv_glb = _proj(xs, mods, norm_g[i, 1:2], w_proj, i)

        conv_w = jnp.pad(ssd_conv_w[i], ((0, 8 - SSD_CONV), (0, 0)))
        dt_bias = jnp.pad(ssd_dt_bias[i].reshape(1, -1), ((0, 0), (0, DT_PAD - 2 * SSD_HEADS)))
        xact, dtsp = _ssd_prep(xbc, dt, conv_w, ssd_conv_b[i].reshape(1, -1), dt_bias)
        a_neg = -jnp.exp(ssd_a_log[i].astype(F32))
        arow = lambda d: jnp.zeros((1, DT_PAD), F32).at[0, d * SSD_HEADS:(d + 1) * SSD_HEADS].set(a_neg[d])
        y_f = _ssd_scan(tbl_f, xact, dtsp, arow(0), tril, e_dirs[0], 0)
        d_exp = jnp.repeat(ssd_d[i].astype(F32), SSD_HEAD_DIM).reshape(1, -1)
        y_ssd = _ssd_scan(tbl_b, xact, dtsp, arow(1), triu, e_dirs[1], 1,
                          extra=(y_f, z, d_exp, ssd_norm_g[i].reshape(1, -1)))

        s5m, s5bs, s5csa, s5csb, lam_re, lam_im = _s5_setup(
            s5_a_re[i], s5_a_im[i], s5_log_dt[i], s5_b_re[i], s5_b_im[i], s5_c_re[i], s5_c_im[i])
        yg = _s5_scan(_s5_gather(u, s5_sel), s5m, s5bs, s5csa, s5csb, lam_re, lam_im)
        y_s5 = _s5_glu(_s5_scatter(yg, s5_sel), u, s5_d[i].reshape(1, -1), glu_w_b, s5_glu_b[i].reshape(1, -1), i)

        g_swa = jnp.concatenate([jnp.tile(swa_qk_g[i, 0], ATT_Q_HEADS), jnp.tile(swa_qk_g[i, 1], ATT_KV_HEADS)])[None]
        q, kt, v = _attn_prep(qkv_swa, g_swa, cos_t, sin_t, bd)
        y_swa = _swa_attn(q, kt, v, swa_sink[i], n_rows)
        y_swa = _ctx_attn(q, kt, v, swa_sink[i], y_swa, True)

        g_glb = jnp.concatenate([jnp.tile(glb_qk_g[i, 0], ATT_Q_HEADS), jnp.tile(glb_qk_g[i, 1], ATT_KV_HEADS)])[None]
        q, kt, v = _attn_prep(qkv_glb, g_glb, cos_t, sin_t, bd)
        kt_c = kt.reshape(ATT_KV_HEADS, HEAD_DIM, nk, GLB_TK).transpose(0, 2, 1, 3)
        y_glb = _glb_attn(q, kt_c, v.reshape(ATT_KV_HEADS, nk, GLB_TK, LANES), n_rows)
        y_glb = _ctx_attn(q, kt, v, swa_sink[i], y_glb, False)

        acc = _merge(hb, (y_ssd, y_s5, y_swa, y_glb), w_gate, w_brs, i)
        xs = _out_proj(xs, acc, mods, w_out_b, i)
        xs = _ffn(xs, mods, norm_g[i, 2:3], ffn_in_b, ffn_out_b, i, 2, out_rows=seq if i == depth - 1 else None)
    return xs[None]
```

```python
import functools
import math

import numpy as np
import jax
import jax.numpy as jnp
from jax import lax
from jax.experimental import pallas as pl
from jax.experimental.pallas import tpu as pltpu

F32 = jnp.float32
BF16 = jnp.bfloat16

D_MODEL = 2048
GRID_W = 64
CTX_LEN = 256
N_MOD = 9
D_FF = 5632
SSD_HEADS = 12
SSD_HEAD_DIM = 64
SSD_INNER = SSD_HEADS * SSD_HEAD_DIM
SSD_GROUPS = 2
SSD_STATE = 128
SSD_BC = SSD_GROUPS * SSD_STATE
SSD_XBC = SSD_INNER + 2 * SSD_BC
SSD_CONV = 5
S5_WIDTH = 512
S5_GROUP = 16
S5_GROUPS = S5_WIDTH // S5_GROUP
S5_STATE = 64
HEAD_DIM = 64
ATT_Q_HEADS = 8
ATT_KV_HEADS = 2
GQA_GROUP = ATT_Q_HEADS // ATT_KV_HEADS
ATT_WIDTH = ATT_Q_HEADS * HEAD_DIM
ATT_KV_WIDTH = ATT_KV_HEADS * HEAD_DIM
ATT_SCALE = HEAD_DIM ** -0.5
SWA_WINDOW = 128
ROPE_THETA = 10000.0
NEG_INF = -1e30
N_BRANCH = 4
EPS = 1e-6

LANES = 128
VMEM_LIMIT = 56 * 1024 * 1024

TM = 512
SSD_T = 128
S5_T = 16
S5_GB = 4
PROJ_TM = 256
GLB_TQ = 256
SWA_TQ = 256
SWA_NSUB = 1
GLB_TK = 1280
DT_PAD = LANES


def _params(sem, vmem=VMEM_LIMIT):
    return pltpu.CompilerParams(dimension_semantics=sem, vmem_limit_bytes=vmem)


def _silu(x):
    return x * jax.nn.sigmoid(x)


def _norm_mod(x, g, shift, scale):
    ms = jnp.mean(x * x, axis=-1, keepdims=True)
    return (x * lax.rsqrt(ms + EPS)) * (g * (1.0 + scale)) + shift


def _dot(a, b):
    return jnp.dot(a, b, preferred_element_type=F32)


def _dot_nt(a, b):
    return lax.dot_general(a, b, (((1,), (1,)), ((), ())), preferred_element_type=F32)


def _split_rhs_dot(m, x, parts):
    acc = None
    r = x
    for _ in range(parts):
        hi = r.astype(BF16)
        t = _dot(m, hi)
        acc = t if acc is None else acc + t
        r = r - hi.astype(F32)
    return acc


def _split_lhs_dot(x, m, parts):
    acc = None
    r = x
    for _ in range(parts):
        hi = r.astype(BF16)
        t = _dot(hi, m)
        acc = t if acc is None else acc + t
        r = r - hi.astype(F32)
    return acc


def _mod_kernel(s_ref, w_ref, b_ref, o_ref):
    s = _silu(s_ref[...])
    o_ref[...] = _dot(s.astype(BF16), w_ref[...].astype(BF16)) + b_ref[...]


def _mods(c, c_ctx, w_mod, b_mod):
    depth = w_mod.shape[0]
    s = jnp.zeros((8, D_MODEL), F32).at[0].set(c_ctx).at[1].set(c[0])
    tn = 1024
    out = pl.pallas_call(
        _mod_kernel,
        grid=(depth, N_MOD * D_MODEL // tn),
        in_specs=[pl.BlockSpec((8, D_MODEL), lambda l, j: (0, 0)),
                  pl.BlockSpec((None, D_MODEL, tn), lambda l, j: (l, 0, j)),
                  pl.BlockSpec((None, 1, tn), lambda l, j: (l, 0, j))],
        out_specs=pl.BlockSpec((None, 8, tn), lambda l, j: (l, 0, j)),
        out_shape=jax.ShapeDtypeStruct((depth, 8, N_MOD * D_MODEL), F32),
        compiler_params=_params(("parallel", "parallel")),
        name="mods",
    )(s, w_mod, b_mod.reshape(depth, 1, N_MOD * D_MODEL))
    return out[:, :2].reshape(depth, 2, N_MOD, D_MODEL)


def _mod_spec(layer):
    return pl.BlockSpec((None, None, N_MOD, D_MODEL), lambda i, *_: (layer, jnp.minimum(i, 1), 0, 0))


def _ffn_kernel(x_ref, mod_ref, g_ref, wa_ref, wb_ref, wo_ref, o_ref, h_ref, acc_ref, *, sub, nj):
    j = pl.program_id(1)

    def step(first, last):
        if first:
            h = _norm_mod(x_ref[...], g_ref[...], mod_ref[3 * sub:3 * sub + 1, :],
                          mod_ref[3 * sub + 1:3 * sub + 2, :]).astype(BF16)
            h_ref[...] = h
        else:
            h = h_ref[...]
        t = _dot((_silu(_dot(h, wa_ref[...])) * _dot(h, wb_ref[...])).astype(BF16), wo_ref[...])
        acc = t if first else acc_ref[...] + t
        if last:
            o_ref[...] = x_ref[...] + (0.5 * mod_ref[3 * sub + 2:3 * sub + 3, :]) * acc
        else:
            acc_ref[...] = acc

    pl.when(j == 0)(lambda: step(True, False))
    pl.when(jnp.logical_and(j > 0, j < nj - 1))(lambda: step(False, False))
    pl.when(j == nj - 1)(lambda: step(False, True))


def _ffn(x, mods, g, w_in, w_out, layer, sub, out_rows=None, tf=512):
    n = x.shape[0]
    nj = D_FF // tf
    half = sub // 2
    skip = 0 if out_rows is None else (n - out_rows) // TM
    return pl.pallas_call(
        functools.partial(_ffn_kernel, sub=sub, nj=nj),
        grid=(n // TM, nj),
        in_specs=[pl.BlockSpec((TM, D_MODEL), lambda i, j: (i, 0)),
                  _mod_spec(layer),
                  pl.BlockSpec((1, D_MODEL), lambda i, j: (0, 0)),
                  pl.BlockSpec((None, None, D_MODEL, tf), lambda i, j: (layer, half, 0, j)),
                  pl.BlockSpec((None, None, D_MODEL, tf), lambda i, j: (layer, half, 0, j + nj)),
                  pl.BlockSpec((None, None, tf, D_MODEL), lambda i, j: (layer, half, j, 0))],
        out_specs=pl.BlockSpec((TM, D_MODEL), lambda i, j: (jnp.maximum(i - skip, 0), 0)),
        out_shape=jax.ShapeDtypeStruct((n - skip * TM, D_MODEL), F32),
        scratch_shapes=[pltpu.VMEM((TM, D_MODEL), BF16), pltpu.VMEM((TM, D_MODEL), F32)],
        compiler_params=_params(("arbitrary", "arbitrary")),
        name="ffn",
    )(x, mods, g, w_in, w_in, w_out)


PROJ_WIDTHS = (SSD_INNER, SSD_XBC, DT_PAD, S5_WIDTH, ATT_WIDTH + 2 * ATT_KV_WIDTH, ATT_WIDTH + 2 * ATT_KV_WIDTH)
W_IN_BOUNDS = tuple(zip(
    np.cumsum((0, SSD_INNER, SSD_XBC, 2 * SSD_HEADS, S5_WIDTH, ATT_WIDTH + 2 * ATT_KV_WIDTH, ATT_WIDTH + 2 * ATT_KV_WIDTH)),
    np.cumsum((SSD_INNER, SSD_XBC, 2 * SSD_HEADS, S5_WIDTH, ATT_WIDTH + 2 * ATT_KV_WIDTH, ATT_WIDTH + 2 * ATT_KV_WIDTH,
               N_BRANCH * D_MODEL))))


def _split_w_in_kernel(w_ref, *o_refs):
    w = w_ref[...]
    for o_ref, (lo, hi) in zip(o_refs, W_IN_BOUNDS):
        piece = w[:, int(lo):int(hi)].astype(BF16)
        if o_ref.shape[1] > piece.shape[1]:
            piece = jnp.concatenate([piece, jnp.zeros((piece.shape[0], o_ref.shape[1] - piece.shape[1]), BF16)], axis=1)
        o_ref[...] = piece


def _split_w_in(w_in):
    depth, d, d_in = w_in.shape
    tr = 128
    widths = PROJ_WIDTHS + (N_BRANCH * D_MODEL,)
    return pl.pallas_call(
        _split_w_in_kernel,
        grid=(depth, d // tr),
        in_specs=[pl.BlockSpec((None, tr, d_in), lambda l, r: (l, r, 0))],
        out_specs=[pl.BlockSpec((None, tr, w), lambda l, r: (l, r, 0)) for w in widths],
        out_shape=[jax.ShapeDtypeStruct((depth, d, w), BF16) for w in widths],
        compiler_params=_params(("parallel", "parallel")),
        name="split_w_in",
    )(w_in)


def _proj_kernel(x_ref, mod_ref, g_ref, *refs):
    nw = len(PROJ_WIDTHS)
    w_refs, h_ref, o_refs = refs[:nw], refs[nw], refs[nw + 1:]
    h = _norm_mod(x_ref[...], g_ref[...], mod_ref[3:4, :], mod_ref[4:5, :]).astype(BF16)
    h_ref[...] = h
    for w_ref, o_ref in zip(w_refs, o_refs):
        o_ref[...] = _dot(h, w_ref[...])


def _proj(x, mods, g, ws, layer):
    n = x.shape[0]
    tm = PROJ_TM
    row = lambda w: pl.BlockSpec((tm, w), lambda i: (i, 0))
    return pl.pallas_call(
        _proj_kernel,
        grid=(n // tm,),
        in_specs=[row(D_MODEL),
                  pl.BlockSpec((None, None, N_MOD, D_MODEL), lambda i: (layer, jnp.minimum(i // (TM // tm), 1), 0, 0)),
                  pl.BlockSpec((1, D_MODEL), lambda i: (0, 0))]
                 + [pl.BlockSpec((None, D_MODEL, w), lambda i: (layer, 0, 0), pipeline_mode=pl.Buffered(1))
                    for w in PROJ_WIDTHS],
        out_specs=[row(D_MODEL)] + [row(w) for w in PROJ_WIDTHS],
        out_shape=[jax.ShapeDtypeStruct((n, D_MODEL), BF16)]
                  + [jax.ShapeDtypeStruct((n, w), F32) for w in PROJ_WIDTHS],
        compiler_params=_params(("parallel",)),
        name="proj",
    )(x, mods, g, *ws)


QK_W = ATT_WIDTH + ATT_KV_WIDTH


def _attn_prep_kernel(qkv_ref, g_ref, cos_ref, sin_ref, bd_ref, q_ref, kt_ref, v_ref):
    x = qkv_ref[...]
    qk = x[:, :QK_W]
    sq = qk * qk
    hi = sq.astype(BF16)
    lo = (sq - hi.astype(F32)).astype(BF16)
    bd = bd_ref[...]
    ms = jnp.concatenate([_dot(hi[:, t * LANES:(t + 1) * LANES], bd) + _dot(lo[:, t * LANES:(t + 1) * LANES], bd)
                          for t in range(QK_W // LANES)], axis=1)
    y = qk * lax.rsqrt(ms + EPS) * g_ref[...]
    cos = cos_ref[...]
    sin = sin_ref[...]
    lane = lax.broadcasted_iota(jnp.int32, (x.shape[0], LANES), 1)
    first_half = (lane % HEAD_DIM) < (HEAD_DIM // 2)
    tiles = []
    for t in range(QK_W // LANES):
        yt = y[:, t * LANES:(t + 1) * LANES]
        rot = jnp.where(first_half, pltpu.roll(yt, LANES - HEAD_DIM // 2, 1), pltpu.roll(yt, HEAD_DIM // 2, 1))
        tiles.append(yt * cos + rot * sin)
    for h in range(ATT_Q_HEADS):
        t = tiles[h // 2]
        q_ref[h] = (t[:, (h % 2) * HEAD_DIM:(h % 2 + 1) * HEAD_DIM] * ATT_SCALE).astype(BF16)
    kt = tiles[ATT_WIDTH // LANES].T
    kt_ref[0] = kt[:HEAD_DIM].astype(BF16)
    kt_ref[1] = kt[HEAD_DIM:].astype(BF16)
    v = x[:, QK_W:QK_W + LANES]
    one_col = jnp.where(lane == HEAD_DIM, 1.0, 0.0)
    v_ref[0] = jnp.where(lane < HEAD_DIM, v, one_col).astype(BF16)
    v_ref[1] = jnp.where(lane < HEAD_DIM, pltpu.roll(v, HEAD_DIM, 1), one_col).astype(BF16)


def _attn_prep(qkv, g640, cos_t, sin_t, bd):
    s = cos_t.shape[0]
    tm = PROJ_TM
    skip = TM // tm - CTX_LEN // tm
    src = lambda i: jnp.where(i < CTX_LEN // tm, i, i + skip)
    return pl.pallas_call(
        _attn_prep_kernel,
        grid=(s // tm,),
        in_specs=[pl.BlockSpec((tm, ATT_WIDTH + 2 * ATT_KV_WIDTH), lambda i: (src(i), 0)),
                  pl.BlockSpec((1, QK_W), lambda i: (0, 0)),
                  pl.BlockSpec((tm, LANES), lambda i: (i, 0)),
                  pl.BlockSpec((tm, LANES), lambda i: (i, 0)),
                  pl.BlockSpec((LANES, LANES), lambda i: (0, 0))],
        out_specs=[pl.BlockSpec((ATT_Q_HEADS, tm, HEAD_DIM), lambda i: (0, i, 0)),
                   pl.BlockSpec((ATT_KV_HEADS, HEAD_DIM, tm), lambda i: (0, 0, i)),
                   pl.BlockSpec((ATT_KV_HEADS, tm, LANES), lambda i: (0, i, 0))],
        out_shape=[jax.ShapeDtypeStruct((ATT_Q_HEADS, s, HEAD_DIM), BF16),
                   jax.ShapeDtypeStruct((ATT_KV_HEADS, HEAD_DIM, s), BF16),
                   jax.ShapeDtypeStruct((ATT_KV_HEADS, s, LANES), BF16)],
        compiler_params=_params(("parallel",)),
        name="attn_prep",
    )(qkv, g640, cos_t, sin_t, bd)


def _heads_out(acc, rows):
    outs = []
    for h in range(GQA_GROUP):
        a = acc[h * rows:(h + 1) * rows]
        outs.append(a[:, :HEAD_DIM] / a[:, HEAD_DIM:HEAD_DIM + 1])
    return jnp.concatenate(outs, axis=1)


def _glb_kernel(q_ref, kt_ref, v_ref, o_ref, *, nk):
    tq = q_ref.shape[1]
    m_rows = GQA_GROUP * tq
    q = q_ref[...].reshape(m_rows, HEAD_DIM)

    def body(c, carry):
        m, acc = carry
        s = _dot(q, kt_ref[c])
        m_new = jnp.maximum(m, jnp.max(s, axis=1, keepdims=True))
        p = jnp.exp(s - m_new)
        acc = jnp.exp(m - m_new) * acc + _dot(p.astype(BF16), v_ref[c])
        return m_new, acc

    m0 = jnp.full((m_rows, 1), NEG_INF, F32)
    acc0 = jnp.zeros((m_rows, LANES), F32)
    _, acc = lax.fori_loop(0, nk, body, (m0, acc0), unroll=True)
    o_ref[...] = _heads_out(acc, tq).astype(BF16)


def _glb_attn(q, kt, v, n_rows):
    s = q.shape[1]
    nk, tk = kt.shape[1], kt.shape[3]
    tq = GLB_TQ
    nq = (s - CTX_LEN) // tq
    half = GQA_GROUP * HEAD_DIM
    return pl.pallas_call(
        functools.partial(_glb_kernel, nk=nk),
        grid=(ATT_KV_HEADS, nq),
        in_specs=[pl.BlockSpec((GQA_GROUP, tq, HEAD_DIM), lambda kv, i: (kv, i + CTX_LEN // tq, 0)),
                  pl.BlockSpec((None, nk, HEAD_DIM, tk), lambda kv, i: (kv, 0, 0, 0)),
                  pl.BlockSpec((None, nk, tk, LANES), lambda kv, i: (kv, 0, 0, 0))],
        out_specs=pl.BlockSpec((tq, half), lambda kv, i: (i + TM // tq, kv)),
        out_shape=jax.ShapeDtypeStruct((n_rows, ATT_WIDTH), BF16),
        compiler_params=_params(("arbitrary", "arbitrary")),
        name="glb_attn",
    )(q, kt, v)


def _swa_bias():
    w, tq = SWA_WINDOW, SWA_TQ
    nkw = tq + 2 * w
    qi = np.arange(tq)[:, None]
    kj = np.arange(nkw)[None, :]
    ok = (kj >= qi) & (kj <= qi + 2 * w)
    oks = np.stack([ok, ok & (kj >= w), ok & (kj < nkw - w)])
    return jnp.asarray(np.where(oks, 0.0, NEG_INF).astype(np.float32))


def _swa_kernel(sink_ref, bias_ref, *refs, nt):
    bps = SWA_TQ // SWA_WINDOW
    nkb = SWA_NSUB * bps + 2
    q_refs, refs = refs[:SWA_NSUB], refs[SWA_NSUB:]
    kt_refs, ktx_ref = refs[:nkb], refs[nkb]
    v_refs, vx_ref, o_ref = refs[nkb + 1:2 * nkb + 1], refs[2 * nkb + 1], refs[2 * nkb + 2]
    n = pl.program_id(0)
    tq = SWA_TQ
    rows = GQA_GROUP * tq
    for u in range(SWA_NSUB):
        first = jnp.logical_and(n == 0, u == 0)
        last = jnp.logical_and(n == nt - 1, u == SWA_NSUB - 1)
        bias = bias_ref[jnp.where(first, 1, jnp.where(last, 2, 0))]
        bias = jnp.concatenate([bias] * GQA_GROUP, axis=0)
        outs = []
        for kv in range(ATT_KV_HEADS):
            q = q_refs[u][kv * GQA_GROUP:(kv + 1) * GQA_GROUP].reshape(rows, HEAD_DIM)
            kt = jnp.concatenate([r[kv] for r in kt_refs[u * bps:u * bps + bps + 2]], axis=1)
            s = jnp.concatenate([_dot(q, kt) + bias, _dot(q, ktx_ref[kv])], axis=1)
            sink = jnp.concatenate([jnp.full((tq, 1), sink_ref[kv * GQA_GROUP + h], F32) for h in range(GQA_GROUP)],
                                   axis=0)
            m = jnp.maximum(jnp.max(s, axis=1, keepdims=True), sink)
            v = jnp.concatenate([r[kv] for r in v_refs[u * bps:u * bps + bps + 2]] + [vx_ref[kv]], axis=0)
            acc = _dot(jnp.exp(s - m).astype(BF16), v)
            lane = lax.broadcasted_iota(jnp.int32, acc.shape, 1)
            acc = acc + jnp.where(lane == HEAD_DIM, jnp.exp(sink - m), 0.0)
            outs.append(_heads_out(acc, tq))
        o_ref[u * tq:(u + 1) * tq, :] = jnp.concatenate(outs, axis=1).astype(BF16)


def _swa_attn(q, kt, v, sink, n_rows):
    s = q.shape[1]
    w = SWA_WINDOW
    ts = SWA_NSUB * SWA_TQ
    nb = (s - CTX_LEN) // w
    nt = (s - CTX_LEN) // ts
    nkb = ts // w + 2
    c0 = CTX_LEN // w
    blk = lambda b: (lambda n: jnp.clip(n * (ts // w) - 1 + b, 0, nb - 1) + c0)
    kt_spec = lambda f: pl.BlockSpec((ATT_KV_HEADS, HEAD_DIM, w), lambda n: (0, 0, f(n)))
    v_spec = lambda f: pl.BlockSpec((ATT_KV_HEADS, w, LANES), lambda n: (0, f(n), 0))
    return pl.pallas_call(
        functools.partial(_swa_kernel, nt=nt),
        grid=(nt,),
        in_specs=[pl.BlockSpec(memory_space=pltpu.SMEM),
                  pl.BlockSpec((3, SWA_TQ, SWA_TQ + 2 * w), lambda n: (0, 0, 0))]
                 + [pl.BlockSpec((ATT_Q_HEADS, SWA_TQ, HEAD_DIM),
                                 lambda n, u=u: (0, n * SWA_NSUB + u + CTX_LEN // SWA_TQ, 0)) for u in range(SWA_NSUB)]
                 + [kt_spec(blk(b)) for b in range(nkb)]
                 + [pl.BlockSpec((ATT_KV_HEADS, HEAD_DIM, CTX_LEN), lambda n: (0, 0, 0))]
                 + [v_spec(blk(b)) for b in range(nkb)]
                 + [pl.BlockSpec((ATT_KV_HEADS, CTX_LEN, LANES), lambda n: (0, 0, 0))],
        out_specs=pl.BlockSpec((ts, ATT_WIDTH), lambda n: (n + TM // ts, 0)),
        out_shape=jax.ShapeDtypeStruct((n_rows, ATT_WIDTH), BF16),
        compiler_params=_params(("parallel",)),
        name="swa_attn",
    )(sink, _swa_bias(), *([q] * SWA_NSUB), *([kt] * (nkb + 1)), *([v] * (nkb + 1)))


def _ctx_attn_kernel(sink_ref, q_ref, kt_ref, v_ref, prev_ref, o_ref, *, use_sink):
    del prev_ref
    rows = GQA_GROUP * CTX_LEN
    outs = []
    for kv in range(ATT_KV_HEADS):
        q = q_ref[kv * GQA_GROUP:(kv + 1) * GQA_GROUP].reshape(rows, HEAD_DIM)
        s = _dot(q, kt_ref[kv])
        m = jnp.max(s, axis=1, keepdims=True)
        if use_sink:
            sink = jnp.concatenate([jnp.full((CTX_LEN, 1), sink_ref[kv * GQA_GROUP + h], F32)
                                    for h in range(GQA_GROUP)], axis=0)
            m = jnp.maximum(m, sink)
        acc = _dot(jnp.exp(s - m).astype(BF16), v_ref[kv])
        if use_sink:
            lane = lax.broadcasted_iota(jnp.int32, acc.shape, 1)
            acc = acc + jnp.where(lane == HEAD_DIM, jnp.exp(sink - m), 0.0)
        outs.append(_heads_out(acc, CTX_LEN))
    o_ref[:CTX_LEN] = jnp.concatenate(outs, axis=1).astype(BF16)
    o_ref[CTX_LEN:] = jnp.zeros((TM - CTX_LEN, ATT_WIDTH), BF16)


def _ctx_attn(q, kt, v, sink, y_prev, use_sink):
    return pl.pallas_call(
        functools.partial(_ctx_attn_kernel, use_sink=use_sink),
        grid=(1,),
        in_specs=[pl.BlockSpec(memory_space=pltpu.SMEM),
                  pl.BlockSpec((ATT_Q_HEADS, CTX_LEN, HEAD_DIM), lambda i: (0, 0, 0)),
                  pl.BlockSpec((ATT_KV_HEADS, HEAD_DIM, CTX_LEN), lambda i: (0, 0, 0)),
                  pl.BlockSpec((ATT_KV_HEADS, CTX_LEN, LANES), lambda i: (0, 0, 0)),
                  pl.BlockSpec(memory_space=pl.ANY)],
        out_specs=pl.BlockSpec((TM, ATT_WIDTH), lambda i: (0, 0)),
        out_shape=jax.ShapeDtypeStruct(y_prev.shape, y_prev.dtype),
        input_output_aliases={4: 0},
        compiler_params=_params(("arbitrary",)),
        name="ctx_attn",
    )(sink, q, kt, v, y_prev)


def _ssd_activate(b, nblk, xp, xc, xn, dt, w, bias, dtb):
    t = SSD_T
    has_prev = jnp.logical_and(b != 0, b != TM // t)
    has_next = jnp.logical_and(b != CTX_LEN // t - 1, b != nblk - 1)
    prev = jnp.where(has_prev, xp, 0.0)
    nxt = jnp.where(has_next, xn, 0.0)
    row = lax.broadcasted_iota(jnp.int32, xc.shape, 0)
    halo = prev.shape[0]
    rep = lambda v: jnp.concatenate([v] * (t // halo), axis=0)
    acc = xc * w[2:3, :] + bias
    for k in (1, 2):
        down = jnp.where(row < k, rep(pltpu.roll(prev, k, 0)), pltpu.roll(xc, k, 0))
        up = jnp.where(row >= t - k, rep(pltpu.roll(nxt, halo - k, 0)), pltpu.roll(xc, t - k, 0))
        acc = acc + down * w[2 - k:3 - k, :] + up * w[2 + k:3 + k, :]
    d = dt + dtb
    return _silu(acc), jnp.maximum(d, 0.0) + jnp.log1p(jnp.exp(-jnp.abs(d)))


def _ssd_scan_kernel(tbl_ref, xp_ref, xc_ref, xn_ref, dt_ref, w_ref, b_ref, dtb_ref, arow_ref, tri_ref, e_ref, *rest,
                     direction, final, nblk):
    if final:
        yf_ref, z_ref, dsk_ref, ng_ref, o_ref, st_ref = rest
    else:
        o_ref, st_ref = rest
    step = pl.program_id(0)
    flag = tbl_ref[1, step]
    t = SSD_T
    gw = SSD_INNER // SSD_GROUPS
    hpg = SSD_HEADS // SSD_GROUPS

    @pl.when(flag == 2)
    def _():
        o_ref[...] = jnp.zeros_like(o_ref)

    @pl.when(flag == 1)
    def _():
        st_ref[...] = jnp.zeros_like(st_ref)

    @pl.when(flag != 2)
    def _():
        x, dt = _ssd_activate(tbl_ref[0, step], nblk, xp_ref[...], xc_ref[...], xn_ref[...], dt_ref[...],
                              w_ref[...], b_ref[...], dtb_ref[...])
        xs = x[:, :SSD_INNER]
        bm = x[:, SSD_INNER:SSD_INNER + SSD_BC]
        cm = x[:, SSD_INNER + SSD_BC:]
        adt = dt * arow_ref[...]
        tri = tri_ref[...]
        keep = tri > 0
        cum = _split_rhs_dot(tri, adt, 3)
        tot = cum[t - 1:t, :] if direction == 0 else cum[0:1, :]
        e = e_ref[...]
        dt_e = _split_lhs_dot(dt, e, 2)
        expc_e = _split_lhs_dot(jnp.exp(cum), e, 2)
        decs_e = _split_lhs_dot(jnp.exp(tot - cum), e, 2)
        dch_e = _split_lhs_dot(jnp.broadcast_to(jnp.exp(tot), (8, DT_PAD)), e, 2)[0:1]
        cum_t = cum.T
        xdt = xs * dt_e
        ys = []
        for g in range(SSD_GROUPS):
            bg = bm[:, g * SSD_STATE:(g + 1) * SSD_STATE]
            cgb = cm[:, g * SSD_STATE:(g + 1) * SSD_STATE].astype(BF16)
            cb = _dot_nt(cgb, bg.astype(BF16))
            stg = st_ref[g]
            y_off = _dot(cgb, stg.astype(BF16)) * expc_e[:, g * gw:(g + 1) * gw]
            parts = []
            for j in range(hpg):
                h = g * hpg + j
                r = direction * SSD_HEADS + h
                diff = cum[:, r:r + 1] - cum_t[r:r + 1, :]
                lmat = jnp.exp(jnp.where(keep, diff, NEG_INF))
                wmat = (cb * lmat).astype(BF16)
                parts.append(_dot(wmat, xdt[:, h * SSD_HEAD_DIM:(h + 1) * SSD_HEAD_DIM].astype(BF16)))
            ys.append(jnp.concatenate(parts, axis=1) + y_off)
            xw = (xdt[:, g * gw:(g + 1) * gw] * decs_e[:, g * gw:(g + 1) * gw]).astype(BF16)
            st_ref[g] = stg * dch_e[:, g * gw:(g + 1) * gw] + _dot(bg.T.astype(BF16), xw)
        y = jnp.concatenate(ys, axis=1)
        if final:
            y = yf_ref[...] + y + dsk_ref[...] * xs
            y = y * _silu(z_ref[...])
            ms = jnp.mean(y * y, axis=-1, keepdims=True)
            o_ref[...] = (y * lax.rsqrt(ms + EPS) * ng_ref[...]).astype(BF16)
        else:
            o_ref[...] = y


def _ssd_scan(tbl, xbc, dt, conv, arow, tri, e, direction, extra=None):
    n = xbc.shape[0]
    t = SSD_T
    final = extra is not None
    blk = lambda w: pl.BlockSpec((t, w), lambda s, tb: (tb[0, s], 0))
    const = lambda r, w: pl.BlockSpec((r, w), lambda s, tb: (0, 0))
    halo_p = pl.BlockSpec((8, SSD_XBC), lambda s, tb: (jnp.maximum(tb[0, s] * (t // 8) - 1, 0), 0))
    halo_n = pl.BlockSpec((8, SSD_XBC), lambda s, tb: (jnp.minimum((tb[0, s] + 1) * (t // 8), n // 8 - 1), 0))
    in_specs = [halo_p, blk(SSD_XBC), halo_n, blk(DT_PAD), const(8, SSD_XBC), const(1, SSD_XBC), const(1, DT_PAD),
                const(1, DT_PAD), const(t, t), const(DT_PAD, SSD_INNER)]
    args = [xbc, xbc, xbc, dt, *conv, arow, tri, e]
    if final:
        in_specs += [blk(SSD_INNER), blk(SSD_INNER), const(1, SSD_INNER), const(1, SSD_INNER)]
        args += list(extra)
    return pl.pallas_call(
        functools.partial(_ssd_scan_kernel, direction=direction, final=final, nblk=n // t),
        grid_spec=pltpu.PrefetchScalarGridSpec(
            num_scalar_prefetch=1, grid=(tbl.shape[1],), in_specs=in_specs,
            out_specs=blk(SSD_INNER),
            scratch_shapes=[pltpu.VMEM((SSD_GROUPS, SSD_STATE, SSD_INNER // SSD_GROUPS), F32)]),
        out_shape=jax.ShapeDtypeStruct((n, SSD_INNER), BF16 if final else F32),
        compiler_params=_params(("arbitrary",)),
        name="ssd_scan_bwd" if final else "ssd_scan_fwd",
    )(tbl, *args)


def _ssd_tables(n_rows):
    t = SSD_T
    ctx = list(range(CTX_LEN // t))
    pad = list(range(CTX_LEN // t, TM // t))
    lat = list(range(TM // t, n_rows // t))
    fwd = ctx + lat + pad
    bwd = ctx[::-1] + lat[::-1] + pad
    flags = [1] + [0] * (len(ctx) + len(lat) - 1) + [2] * len(pad)
    return (jnp.asarray(np.array([fwd, flags], np.int32)), jnp.asarray(np.array([bwd, flags], np.int32)))


def _dot_nt_split(a, b):
    a_hi = a.astype(BF16)
    b_hi = b.astype(BF16)
    a_lo = (a - a_hi.astype(F32)).astype(BF16)
    b_lo = (b - b_hi.astype(F32)).astype(BF16)
    return _dot_nt(a_hi, b_hi) + _dot_nt(a_hi, b_lo) + _dot_nt(a_lo, b_hi)


def _s5_setup_kernel(pa_ref, pb_ref, pc_ref, tile_ref, m_ref, bs_ref, csa_ref, csb_ref, lre_ref, lim_ref):
    t_len = S5_T
    hh = S5_GROUP
    a_re = pa_ref[0:1, :]
    a_im = pa_ref[1:2, :]
    dt = jnp.exp(pa_ref[2:3, :])
    mag = jnp.exp(a_re * dt)
    lr = mag * jnp.cos(a_im * dt)
    li = mag * jnp.sin(a_im * dt)
    den = a_re * a_re + a_im * a_im
    nr = lr - 1.0
    f_re = (nr * a_re + li * a_im) / den
    f_im = (li * a_re - nr * a_im) / den
    b_re, b_im = pb_ref[0:hh], pb_ref[hh:2 * hh]
    c_re, c_im = pc_ref[0:hh], pc_ref[hh:2 * hh]
    bb_re = f_re * b_re - f_im * b_im
    bb_im = f_re * b_im + f_im * b_re
    pw = [(jnp.ones_like(lr), jnp.zeros_like(lr))]
    for _ in range(t_len):
        pr, pi = pw[-1]
        pw.append((pr * lr - pi * li, pr * li + pi * lr))
    fwd = lax.broadcasted_iota(jnp.int32, lr.shape, 1) < S5_STATE
    pick = lambda kf, kb: (jnp.where(fwd, pw[kf][0], pw[kb][0]), jnp.where(fwd, pw[kf][1], pw[kb][1]))
    bs_re, bs_im, cs_re, cs_im = [], [], [], []
    for t in range(t_len):
        er, ei = pick(t_len - 1 - t, t)
        bs_re.append(er * bb_re - ei * bb_im)
        bs_im.append(er * bb_im + ei * bb_re)
        fr, fi = pick(t + 1, t_len - t)
        cs_re.append(c_re * fr - c_im * fi)
        cs_im.append(c_re * fi + c_im * fr)
    bs_full = jnp.concatenate([jnp.concatenate(bs_re, axis=0), jnp.concatenate(bs_im, axis=0)], axis=1)
    cs_full = jnp.concatenate([jnp.concatenate(cs_re, axis=0), -jnp.concatenate(cs_im, axis=0)], axis=1)
    wide = bs_full.shape
    fwd_w = (lax.broadcasted_iota(jnp.int32, wide, 1) % (2 * S5_STATE)) < S5_STATE
    bs_ref[...] = bs_full.astype(BF16)
    csa_ref[...] = jnp.where(fwd_w, cs_full, 0.0).astype(BF16)
    csb_ref[...] = jnp.where(fwd_w, 0.0, cs_full).astype(BF16)
    lre_ref[...] = pw[t_len][0]
    lim_ref[...] = pw[t_len][1]
    cc = jnp.concatenate([c_re, -c_im], axis=1)
    cc = jnp.concatenate([cc, jnp.zeros((LANES - hh, wide[1]), F32)], axis=0)
    fwd_c = (lax.broadcasted_iota(jnp.int32, cc.shape, 1) % (2 * S5_STATE)) < S5_STATE
    kr_f = _dot_nt_split(bs_full, jnp.where(fwd_c, cc, 0.0))
    k_b = _dot_nt_split(bs_full, jnp.where(fwd_c, 0.0, cc))
    tile = tile_ref[...]
    kr_w = _split_lhs_dot(kr_f, tile, 3)
    kb_w = _split_lhs_dot(k_b, tile, 3)
    lane_blk = lax.broadcasted_iota(jnp.int32, wide, 1) // hh
    m = jnp.zeros(wide, F32)
    for t in range(t_len):
        up, dn = kr_w, kb_w
        if t < t_len - 1:
            r = (t_len - 1 - t) * hh
            up = jnp.concatenate([kr_w[r:], jnp.zeros((r, wide[1]), F32)], axis=0)
        if t > 0:
            r = t * hh
            dn = jnp.concatenate([jnp.zeros((r, wide[1]), F32), kb_w[:wide[0] - r]], axis=0)
        m = m + jnp.where(lane_blk == t, up + dn, 0.0)
    m_ref[...] = m.astype(BF16)


def _s5_setup(a_re, a_im, log_dt, b_re, b_im, c_re, c_im):
    g = a_re.shape[1]
    tw = S5_T * S5_GROUP
    both = lambda v: jnp.concatenate([v[0], v[1]], axis=-1)
    ldt = jnp.broadcast_to(log_dt[:, :, None], (2, g, S5_STATE))
    pa = jnp.stack([both(a_re), both(a_im), both(ldt)], axis=1).astype(F32)
    pa = jnp.pad(pa, ((0, 0), (0, 5), (0, 0)))
    dup = lambda v: jnp.concatenate([v, v], axis=-1)
    pb = jnp.concatenate([dup(jnp.swapaxes(b_re, 1, 2)), dup(jnp.swapaxes(b_im, 1, 2))], axis=1).astype(F32)
    pc = jnp.concatenate([dup(c_re), dup(c_im)], axis=1).astype(F32)
    tile = jnp.asarray(np.tile(np.eye(LANES, S5_GROUP).astype(np.float32), (1, S5_T)), BF16)
    per_g = lambda r, c: pl.BlockSpec((None, r, c), lambda i: (i, 0, 0))
    mat = jax.ShapeDtypeStruct((g, tw, tw), BF16)
    lam = jax.ShapeDtypeStruct((g, 1, 2 * S5_STATE), F32)
    m, bs, csa, csb, lre, lim = pl.pallas_call(
        _s5_setup_kernel,
        grid=(g,),
        in_specs=[per_g(8, 2 * S5_STATE), per_g(2 * S5_GROUP, 2 * S5_STATE), per_g(2 * S5_GROUP, 2 * S5_STATE),
                  pl.BlockSpec((LANES, tw), lambda i: (0, 0))],
        out_specs=[per_g(tw, tw)] * 4 + [per_g(1, 2 * S5_STATE)] * 2,
        out_shape=[mat] * 4 + [lam] * 2,
        compiler_params=_params(("parallel",)),
        name="s5_setup",
    )(pa, pb, pc, tile)
    return m, bs, csa, csb, lre[:, 0], lim[:, 0]


def _s5_kernel(u_ref, m_ref, bs_ref, csa_ref, csb_ref, lre_ref, lim_ref, y_ref, v_ref, sa_ref, sb_ref, *, nctx, npad, nch):
    gb = S5_GB
    for j in range(gb):
        v = _dot(u_ref[j], bs_ref[j])
        v_ref.at[0][pl.ds(j, nch, stride=gb), :] = v[:, :LANES]
        v_ref.at[1][pl.ds(j, nch, stride=gb), :] = v[:, LANES:]
    lre = lre_ref[0]
    lim = lim_ref[0]
    is_fwd = lax.broadcasted_iota(jnp.int32, (gb, LANES), 1) < S5_STATE

    nlat = nch - nctx - npad

    def body(i, carry):
        sre, sim = carry
        tail = i - nlat
        rf = jnp.where(i < nctx, i, jnp.where(i < nctx + nlat, i + npad, tail)) * gb
        rb = jnp.where(i < nctx, nctx - 1 - i, jnp.where(i < nctx + nlat, nch + nctx - 1 - i, tail)) * gb
        sa_ref[0, pl.ds(rf, gb), :] = sre
        sa_ref[1, pl.ds(rf, gb), :] = sim
        sb_ref[0, pl.ds(rb, gb), :] = sre
        sb_ref[1, pl.ds(rb, gb), :] = sim
        vre = jnp.where(is_fwd, v_ref[0, pl.ds(rf, gb), :], v_ref[0, pl.ds(rb, gb), :])
        vim = jnp.where(is_fwd, v_ref[1, pl.ds(rf, gb), :], v_ref[1, pl.ds(rb, gb), :])
        return lre * sre - lim * sim + vre, lre * sim + lim * sre + vim

    zero = jnp.zeros((gb, LANES), F32)
    lax.fori_loop(0, nch, body, (zero, zero))
    for j in range(gb):
        rows = pl.ds(j, nch, stride=gb)
        sa = jnp.concatenate([sa_ref.at[0][rows, :], sa_ref.at[1][rows, :]], axis=1).astype(BF16)
        sb = jnp.concatenate([sb_ref.at[0][rows, :], sb_ref.at[1][rows, :]], axis=1).astype(BF16)
        y_ref[j] = _dot(u_ref[j], m_ref[j]) + _dot_nt(sa, csa_ref[j]) + _dot_nt(sb, csb_ref[j])


def _s5_scan(ug, m, bs, csa, csb, lam_re, lam_im):
    g, nch, tw = ug.shape
    gb = S5_GB
    sw = 4 * S5_STATE
    blk = lambda a, b: pl.BlockSpec((gb, a, b), lambda i: (i, 0, 0))
    lam_spec = pl.BlockSpec((1, gb, 2 * S5_STATE), lambda i: (i, 0, 0))
    return pl.pallas_call(
        functools.partial(_s5_kernel, nctx=CTX_LEN // S5_T, npad=(TM - CTX_LEN) // S5_T, nch=nch),
        grid=(g // gb,),
        in_specs=[blk(nch, tw), blk(tw, tw), blk(tw, sw), blk(tw, sw), blk(tw, sw), lam_spec, lam_spec],
        out_specs=blk(nch, tw),
        out_shape=jax.ShapeDtypeStruct((g, nch, tw), F32),
        scratch_shapes=[pltpu.VMEM((2, nch * gb, LANES), F32) for _ in range(3)],
        compiler_params=_params(("parallel",)),
        name="s5_scan",
    )(ug, m, bs, csa, csb, lam_re.reshape(g // gb, gb, -1), lam_im.reshape(g // gb, gb, -1))


def _s5_select():
    gpt = LANES // S5_GROUP
    sel = np.zeros((gpt, S5_T * LANES, S5_T * S5_GROUP), np.float32)
    for q in range(gpt):
        for t in range(S5_T):
            for h in range(S5_GROUP):
                sel[q, t * LANES + q * S5_GROUP + h, t * S5_GROUP + h] = 1.0
    return jnp.asarray(sel, BF16)


def _s5_gather_kernel(u_ref, sel_ref, o_ref):
    rb = o_ref.shape[1]
    u2 = jnp.concatenate([u_ref[pl.ds(t, rb, stride=S5_T), :] for t in range(S5_T)], axis=1).astype(BF16)
    for q in range(LANES // S5_GROUP):
        o_ref[q] = _dot(u2, sel_ref[q]).astype(BF16)


def _s5_gather(u, sel):
    n = u.shape[0]
    gpt = LANES // S5_GROUP
    rb = n // S5_T // 4
    tw = S5_T * S5_GROUP
    return pl.pallas_call(
        _s5_gather_kernel,
        grid=(S5_WIDTH // LANES, 4),
        in_specs=[pl.BlockSpec((rb * S5_T, LANES), lambda j, r: (r, j)),
                  pl.BlockSpec((gpt, S5_T * LANES, tw), lambda j, r: (0, 0, 0), pipeline_mode=pl.Buffered(1))],
        out_specs=pl.BlockSpec((gpt, rb, tw), lambda j, r: (j, r, 0)),
        out_shape=jax.ShapeDtypeStruct((S5_GROUPS, n // S5_T, tw), BF16),
        compiler_params=_params(("parallel", "parallel")),
        name="s5_gather",
    )(u, sel)


def _s5_scatter_kernel(y_ref, sel_ref, o_ref):
    rb = y_ref.shape[1]
    acc = None
    for q in range(LANES // S5_GROUP):
        y = y_ref[q]
        hi = y.astype(BF16)
        lo = (y - hi.astype(F32)).astype(BF16)
        part = _dot_nt(hi, sel_ref[q]) + _dot_nt(lo, sel_ref[q])
        acc = part if acc is None else acc + part
    for t in range(S5_T):
        o_ref[pl.ds(t, rb, stride=S5_T), :] = acc[:, t * LANES:(t + 1) * LANES]


def _s5_scatter(yg, sel):
    g, nchp, tw = yg.shape
    gpt = LANES // S5_GROUP
    rb = nchp // 4
    return pl.pallas_call(
        _s5_scatter_kernel,
        grid=(S5_WIDTH // LANES, 4),
        in_specs=[pl.BlockSpec((gpt, rb, tw), lambda j, r: (j, r, 0)),
                  pl.BlockSpec((gpt, S5_T * LANES, tw), lambda j, r: (0, 0, 0), pipeline_mode=pl.Buffered(1))],
        out_specs=pl.BlockSpec((rb * S5_T, LANES), lambda j, r: (r, j)),
        out_shape=jax.ShapeDtypeStruct((nchp * S5_T, S5_WIDTH), F32),
        compiler_params=_params(("parallel", "parallel")),
        name="s5_scatter",
    )(yg, sel)


def _s5_glu_kernel(y_ref, u_ref, d_ref, w_ref, b_ref, o_ref):
    y = jax.nn.gelu(y_ref[...] + d_ref[...] * u_ref[...])
    t = _dot(y.astype(BF16), w_ref[...]) + b_ref[...]
    o_ref[...] = (t[:, :S5_WIDTH] * jax.nn.sigmoid(t[:, S5_WIDTH:])).astype(BF16)


def _s5_glu(y, u, d, w, b, layer):
    n = y.shape[0]
    row = lambda wd: pl.BlockSpec((TM, wd), lambda i: (i, 0))
    return pl.pallas_call(
        _s5_glu_kernel,
        grid=(n // TM,),
        in_specs=[row(S5_WIDTH), row(S5_WIDTH), pl.BlockSpec((1, S5_WIDTH), lambda i: (0, 0)),
                  pl.BlockSpec((None, S5_WIDTH, 2 * S5_WIDTH), lambda i: (layer, 0, 0)),
                  pl.BlockSpec((1, 2 * S5_WIDTH), lambda i: (0, 0))],
        out_specs=row(S5_WIDTH),
        out_shape=jax.ShapeDtypeStruct((n, S5_WIDTH), BF16),
        compiler_params=_params(("parallel",)),
        name="s5_glu",
    )(y, u, d, w, b)


BR_WIDTHS = (SSD_INNER, S5_WIDTH, ATT_WIDTH, ATT_WIDTH)


def _merge_kernel(h_ref, *refs):
    y_refs, wg_refs, wb_refs, o_ref = refs[:4], refs[4:8], refs[8:12], refs[12]
    h = h_ref[...]
    acc = None
    for y_ref, wg_ref, wb_ref in zip(y_refs, wg_refs, wb_refs):
        t = jax.nn.sigmoid(_dot(h, wg_ref[...])) * _dot(y_ref[...], wb_ref[...])
        acc = t if acc is None else acc + t
    o_ref[...] = acc.astype(BF16)


def _merge(hb, ys, w_gate, w_brs, layer, tn=512):
    n = hb.shape[0]
    nj = D_MODEL // tn
    row = lambda w: pl.BlockSpec((TM, w), lambda i, j: (i, 0))
    return pl.pallas_call(
        _merge_kernel,
        grid=(n // TM, nj),
        in_specs=[row(D_MODEL)] + [row(w) for w in BR_WIDTHS]
                 + [pl.BlockSpec((None, D_MODEL, tn), lambda i, j, b=b: (layer, 0, b * nj + j)) for b in range(N_BRANCH)]
                 + [pl.BlockSpec((None, w, tn), lambda i, j: (layer, 0, j)) for w in BR_WIDTHS],
        out_specs=pl.BlockSpec((TM, tn), lambda i, j: (i, j)),
        out_shape=jax.ShapeDtypeStruct((n, D_MODEL), BF16),
        compiler_params=_params(("parallel", "arbitrary")),
        name="merge",
    )(hb, *ys, w_gate, w_gate, w_gate, w_gate, *w_brs)


def _out_kernel(x_ref, a_ref, mod_ref, w_ref, o_ref):
    o_ref[...] = x_ref[...] + mod_ref[5:6, :] * _dot(a_ref[...], w_ref[...])


def _out_proj(x, acc, mods, w_out, layer):
    n = x.shape[0]
    row = lambda: pl.BlockSpec((TM, D_MODEL), lambda i: (i, 0))
    return pl.pallas_call(
        _out_kernel,
        grid=(n // TM,),
        in_specs=[row(), row(), _mod_spec(layer),
                  pl.BlockSpec((None, D_MODEL, D_MODEL), lambda i: (layer, 0, 0), pipeline_mode=pl.Buffered(1))],
        out_specs=row(),
        out_shape=jax.ShapeDtypeStruct((n, D_MODEL), F32),
        compiler_params=_params(("parallel",)),
        name="out_proj",
    )(x, acc, mods, w_out)


def _rope_tables(seq):
    rows = seq // GRID_W
    row = jnp.repeat(jnp.arange(rows, dtype=F32), GRID_W)
    col = jnp.tile(jnp.arange(GRID_W, dtype=F32), rows)
    n_freq = HEAD_DIM // 4
    inv = ROPE_THETA ** (-jnp.arange(n_freq, dtype=F32) / n_freq)
    ang = jnp.concatenate([row[:, None] * inv, col[:, None] * inv], axis=-1)
    cos, sin = jnp.cos(ang), jnp.sin(ang)
    reps = LANES // HEAD_DIM
    cos_t = jnp.tile(jnp.concatenate([cos, cos], axis=-1), (1, reps))
    sin_t = jnp.tile(jnp.concatenate([-sin, sin], axis=-1), (1, reps))
    cos_t = jnp.concatenate([jnp.ones((CTX_LEN, LANES), F32), cos_t], axis=0)
    sin_t = jnp.concatenate([jnp.zeros((CTX_LEN, LANES), F32), sin_t], axis=0)
    return cos_t, sin_t


def _ssd_expand(direction):
    e = np.zeros((DT_PAD, SSD_INNER), np.float32)
    for h in range(SSD_HEADS):
        e[direction * SSD_HEADS + h, h * SSD_HEAD_DIM:(h + 1) * SSD_HEAD_DIM] = 1.0
    return jnp.asarray(e, BF16)


def _to_stream(ctx_rows, lat_rows):
    pad = jnp.zeros((TM - CTX_LEN,) + ctx_rows.shape[1:], ctx_rows.dtype)
    return jnp.concatenate([ctx_rows, pad, lat_rows], axis=0)


def kernel(x, c, ctx, c_ctx, w_mod, b_mod, norm_g, ffn_in, ffn_out, w_in, ssd_conv_w, ssd_conv_b, ssd_a_log, ssd_dt_bias, ssd_d, ssd_norm_g, s5_a_re, s5_a_im, s5_log_dt, s5_b_re, s5_b_im, s5_c_re, s5_c_im, s5_d, s5_glu_w, s5_glu_b, swa_qk_g, swa_sink, glb_qk_g, w_br_ssd, w_br_s5, w_br_swa, w_br_glb, w_out):
    assert x.shape[0] == 1 and ctx.shape[1] == CTX_LEN
    depth = w_mod.shape[0]
    seq = x.shape[1]
    n_rows = TM + seq
    s_len = CTX_LEN + seq

    ffn_in_b = ffn_in.astype(BF16)
    ffn_out_b = ffn_out.astype(BF16)
    *w_proj, w_gate = _split_w_in(w_in)
    w_brs = tuple(w.astype(BF16) for w in (w_br_ssd, w_br_s5, w_br_swa, w_br_glb))
    w_out_b = w_out.astype(BF16)
    glu_w_b = s5_glu_w.astype(BF16)

    mods = _mods(c, c_ctx, w_mod, b_mod)
    cos_t, sin_t = _rope_tables(seq)
    bd = jnp.asarray(np.kron(np.eye(LANES // HEAD_DIM), np.ones((HEAD_DIM, HEAD_DIM))) / HEAD_DIM, BF16)
    tbl_f, tbl_b = _ssd_tables(n_rows)
    tril = jnp.asarray(np.tril(np.ones((SSD_T, SSD_T), np.float32)), BF16)
    triu = jnp.asarray(np.triu(np.ones((SSD_T, SSD_T), np.float32)), BF16)
    e_dirs = (_ssd_expand(0), _ssd_expand(1))
    s5_sel = _s5_select()
    nk = s_len // GLB_TK

    xs = _to_stream(ctx[0], x[0])
    for i in range(depth):
        xs = _ffn(xs, mods, norm_g[i, 0:1], ffn_in_b, ffn_out_b, i, 0)
        hb, z, xbc, dt, u, qkv_swa, qkv_glb = _proj(xs, mods, norm_g[i, 1:2], w_proj, i)

        conv_w = jnp.pad(ssd_conv_w[i], ((0, 8 - SSD_CONV), (0, 0)))
        dt_bias = jnp.pad(ssd_dt_bias[i].reshape(1, -1), ((0, 0), (0, DT_PAD - 2 * SSD_HEADS)))
        conv = (conv_w, ssd_conv_b[i].reshape(1, -1), dt_bias)
        a_neg = -jnp.exp(ssd_a_log[i].astype(F32))
        arow = lambda d: jnp.zeros((1, DT_PAD), F32).at[0, d * SSD_HEADS:(d + 1) * SSD_HEADS].set(a_neg[d])
        y_f = _ssd_scan(tbl_f, xbc, dt, conv, arow(0), tril, e_dirs[0], 0)
        d_exp = jnp.repeat(ssd_d[i].astype(F32), SSD_HEAD_DIM).reshape(1, -1)
        y_ssd = _ssd_scan(tbl_b, xbc, dt, conv, arow(1), triu, e_dirs[1], 1,
                          extra=(y_f, z, d_exp, ssd_norm_g[i].reshape(1, -1)))

        s5m, s5bs, s5csa, s5csb, lam_re, lam_im = _s5_setup(
            s5_a_re[i], s5_a_im[i], s5_log_dt[i], s5_b_re[i], s5_b_im[i], s5_c_re[i], s5_c_im[i])
        yg = _s5_scan(_s5_gather(u, s5_sel), s5m, s5bs, s5csa, s5csb, lam_re, lam_im)
        y_s5 = _s5_glu(_s5_scatter(yg, s5_sel), u, s5_d[i].reshape(1, -1), glu_w_b, s5_glu_b[i].reshape(1, -1), i)

        g_swa = jnp.concatenate([jnp.tile(swa_qk_g[i, 0], ATT_Q_HEADS), jnp.tile(swa_qk_g[i, 1], ATT_KV_HEADS)])[None]
        q, kt, v = _attn_prep(qkv_swa, g_swa, cos_t, sin_t, bd)
        y_swa = _swa_attn(q, kt, v, swa_sink[i], n_rows)
        y_swa = _ctx_attn(q, kt, v, swa_sink[i], y_swa, True)

        g_glb = jnp.concatenate([jnp.tile(glb_qk_g[i, 0], ATT_Q_HEADS), jnp.tile(glb_qk_g[i, 1], ATT_KV_HEADS)])[None]
        q, kt, v = _attn_prep(qkv_glb, g_glb, cos_t, sin_t, bd)
        kt_c = kt.reshape(ATT_KV_HEADS, HEAD_DIM, nk, GLB_TK).transpose(0, 2, 1, 3)
        y_glb = _glb_attn(q, kt_c, v.reshape(ATT_KV_HEADS, nk, GLB_TK, LANES), n_rows)
        y_glb = _ctx_attn(q, kt, v, swa_sink[i], y_glb, False)

        acc = _merge(hb, (y_ssd, y_s5, y_swa, y_glb), w_gate, w_brs, i)
        xs = _out_proj(xs, acc, mods, w_out_b, i)
        xs = _ffn(xs, mods, norm_g[i, 2:3], ffn_in_b, ffn_out_b, i, 2, out_rows=seq if i == depth - 1 else None)
    return xs[None]
```

```python
import functools
import math

import numpy as np
import jax
import jax.numpy as jnp
from jax import lax
from jax.experimental import pallas as pl
from jax.experimental.pallas import tpu as pltpu

F32 = jnp.float32
BF16 = jnp.bfloat16

D_MODEL = 2048
GRID_W = 64
CTX_LEN = 256
N_MOD = 9
D_FF = 5632
SSD_HEADS = 12
SSD_HEAD_DIM = 64
SSD_INNER = SSD_HEADS * SSD_HEAD_DIM
SSD_GROUPS = 2
SSD_STATE = 128
SSD_BC = SSD_GROUPS * SSD_STATE
SSD_XBC = SSD_INNER + 2 * SSD_BC
SSD_CONV = 5
S5_WIDTH = 512
S5_GROUP = 16
S5_GROUPS = S5_WIDTH // S5_GROUP
S5_STATE = 64
HEAD_DIM = 64
ATT_Q_HEADS = 8
ATT_KV_HEADS = 2
GQA_GROUP = ATT_Q_HEADS // ATT_KV_HEADS
ATT_WIDTH = ATT_Q_HEADS * HEAD_DIM
ATT_KV_WIDTH = ATT_KV_HEADS * HEAD_DIM
ATT_SCALE = HEAD_DIM ** -0.5
SWA_WINDOW = 128
ROPE_THETA = 10000.0
NEG_INF = -1e30
N_BRANCH = 4
EPS = 1e-6

LANES = 128
VMEM_LIMIT = 56 * 1024 * 1024

TM = 512
FFN_TF = 512
SSD_T = 128
S5_T = 16
S5_GB = 4
PROJ_TM = 512
PREP_TM = 256
GLB_TQ = 256
SWA_TQ = 256
SWA_NSUB = 1
GLB_TK = 1280
DT_PAD = LANES


def _params(sem, vmem=VMEM_LIMIT):
    return pltpu.CompilerParams(dimension_semantics=sem, vmem_limit_bytes=vmem)


def _silu(x):
    return x * jax.nn.sigmoid(x)


def _norm_mod(x, g, shift, scale):
    ms = jnp.mean(x * x, axis=-1, keepdims=True)
    return (x * lax.rsqrt(ms + EPS)) * (g * (1.0 + scale)) + shift


def _dot(a, b):
    return jnp.dot(a, b, preferred_element_type=F32)


def _dot_nt(a, b):
    return lax.dot_general(a, b, (((1,), (1,)), ((), ())), preferred_element_type=F32)


def _split_rhs_dot(m, x, parts):
    acc = None
    r = x
    for _ in range(parts):
        hi = r.astype(BF16)
        t = _dot(m, hi)
        acc = t if acc is None else acc + t
        r = r - hi.astype(F32)
    return acc


def _split_lhs_dot(x, m, parts):
    acc = None
    r = x
    for _ in range(parts):
        hi = r.astype(BF16)
        t = _dot(hi, m)
        acc = t if acc is None else acc + t
        r = r - hi.astype(F32)
    return acc


def _mod_kernel(s_ref, w_ref, b_ref, o_ref):
    s = _silu(s_ref[...])
    o_ref[...] = _dot(s.astype(BF16), w_ref[...].astype(BF16)) + b_ref[...]


def _mods(c, c_ctx, w_mod, b_mod):
    depth = w_mod.shape[0]
    s = jnp.zeros((8, D_MODEL), F32).at[0].set(c_ctx).at[1].set(c[0])
    tn = 1024
    out = pl.pallas_call(
        _mod_kernel,
        grid=(depth, N_MOD * D_MODEL // tn),
        in_specs=[pl.BlockSpec((8, D_MODEL), lambda l, j: (0, 0)),
                  pl.BlockSpec((None, D_MODEL, tn), lambda l, j: (l, 0, j)),
                  pl.BlockSpec((None, 1, tn), lambda l, j: (l, 0, j))],
        out_specs=pl.BlockSpec((None, 8, tn), lambda l, j: (l, 0, j)),
        out_shape=jax.ShapeDtypeStruct((depth, 8, N_MOD * D_MODEL), F32),
        compiler_params=_params(("parallel", "parallel")),
        name="mods",
    )(s, w_mod, b_mod.reshape(depth, 1, N_MOD * D_MODEL))
    return out[:, :2].reshape(depth, 2, N_MOD, D_MODEL)


def _mod_spec(layer):
    return pl.BlockSpec((None, None, N_MOD, D_MODEL), lambda i, *_: (layer, jnp.minimum(i, 1), 0, 0))


def _ffn_kernel(x_ref, mod_ref, g_ref, wa_ref, wb_ref, wo_ref, o_ref, h_ref, acc_ref, *, sub, nj):
    j = pl.program_id(1)

    def step(first, last):
        if first:
            h = _norm_mod(x_ref[...], g_ref[...], mod_ref[3 * sub:3 * sub + 1, :],
                          mod_ref[3 * sub + 1:3 * sub + 2, :]).astype(BF16)
            h_ref[...] = h
        else:
            h = h_ref[...]
        t = _dot((_silu(_dot(h, wa_ref[...])) * _dot(h, wb_ref[...])).astype(BF16), wo_ref[...])
        acc = t if first else acc_ref[...] + t
        if last:
            o_ref[...] = x_ref[...] + (0.5 * mod_ref[3 * sub + 2:3 * sub + 3, :]) * acc
        else:
            acc_ref[...] = acc

    pl.when(j == 0)(lambda: step(True, False))
    pl.when(jnp.logical_and(j > 0, j < nj - 1))(lambda: step(False, False))
    pl.when(j == nj - 1)(lambda: step(False, True))


def _ffn(x, mods, g, w_in, w_out, layer, sub, out_rows=None, tf=FFN_TF):
    n = x.shape[0]
    nj = D_FF // tf
    half = sub // 2
    skip = 0 if out_rows is None else (n - out_rows) // TM
    return pl.pallas_call(
        functools.partial(_ffn_kernel, sub=sub, nj=nj),
        grid=(n // TM, nj),
        in_specs=[pl.BlockSpec((TM, D_MODEL), lambda i, j: (i, 0)),
                  _mod_spec(layer),
                  pl.BlockSpec((1, D_MODEL), lambda i, j: (0, 0)),
                  pl.BlockSpec((None, None, None, D_MODEL, tf), lambda i, j: (layer, half, j, 0, 0)),
                  pl.BlockSpec((None, None, None, D_MODEL, tf), lambda i, j: (layer, half, j + nj, 0, 0)),
                  pl.BlockSpec((None, None, tf, D_MODEL), lambda i, j: (layer, half, j, 0))],
        out_specs=pl.BlockSpec((TM, D_MODEL), lambda i, j: (jnp.maximum(i - skip, 0), 0)),
        out_shape=jax.ShapeDtypeStruct((n - skip * TM, D_MODEL), F32),
        scratch_shapes=[pltpu.VMEM((TM, D_MODEL), BF16), pltpu.VMEM((TM, D_MODEL), F32)],
        compiler_params=_params(("arbitrary", "arbitrary")),
        name="ffn",
    )(x, mods, g, w_in, w_in, w_out)


PROJ_WIDTHS = (SSD_INNER, SSD_XBC, DT_PAD, S5_WIDTH, ATT_WIDTH + 2 * ATT_KV_WIDTH, ATT_WIDTH + 2 * ATT_KV_WIDTH)
W_IN_BOUNDS = tuple(zip(
    np.cumsum((0, SSD_INNER, SSD_XBC, 2 * SSD_HEADS, S5_WIDTH, ATT_WIDTH + 2 * ATT_KV_WIDTH, ATT_WIDTH + 2 * ATT_KV_WIDTH)),
    np.cumsum((SSD_INNER, SSD_XBC, 2 * SSD_HEADS, S5_WIDTH, ATT_WIDTH + 2 * ATT_KV_WIDTH, ATT_WIDTH + 2 * ATT_KV_WIDTH,
               N_BRANCH * D_MODEL))))


def _split_w_in_kernel(w_ref, *o_refs):
    w = w_ref[...]
    for o_ref, (lo, hi) in zip(o_refs, W_IN_BOUNDS):
        piece = w[:, int(lo):int(hi)].astype(BF16)
        if o_ref.shape[1] > piece.shape[1]:
            piece = jnp.concatenate([piece, jnp.zeros((piece.shape[0], o_ref.shape[1] - piece.shape[1]), BF16)], axis=1)
        o_ref[...] = piece


def _split_w_in(w_in):
    depth, d, d_in = w_in.shape
    tr = 128
    widths = PROJ_WIDTHS + (N_BRANCH * D_MODEL,)
    return pl.pallas_call(
        _split_w_in_kernel,
        grid=(depth, d // tr),
        in_specs=[pl.BlockSpec((None, tr, d_in), lambda l, r: (l, r, 0))],
        out_specs=[pl.BlockSpec((None, tr, w), lambda l, r: (l, r, 0)) for w in widths],
        out_shape=[jax.ShapeDtypeStruct((depth, d, w), BF16) for w in widths],
        compiler_params=_params(("parallel", "parallel")),
        name="split_w_in",
    )(w_in)


def _proj_kernel(x_ref, mod_ref, g_ref, *refs):
    nw = len(PROJ_WIDTHS)
    w_refs, h_ref, o_refs = refs[:nw], refs[nw], refs[nw + 1:]
    h = _norm_mod(x_ref[...], g_ref[...], mod_ref[3:4, :], mod_ref[4:5, :]).astype(BF16)
    h_ref[...] = h
    for w_ref, o_ref in zip(w_refs, o_refs):
        o_ref[...] = _dot(h, w_ref[...])


def _proj(x, mods, g, ws, layer):
    n = x.shape[0]
    tm = PROJ_TM
    row = lambda w: pl.BlockSpec((tm, w), lambda i: (i, 0))
    return pl.pallas_call(
        _proj_kernel,
        grid=(n // tm,),
        in_specs=[row(D_MODEL),
                  pl.BlockSpec((None, None, N_MOD, D_MODEL), lambda i: (layer, jnp.minimum(i // (TM // tm), 1), 0, 0)),
                  pl.BlockSpec((1, D_MODEL), lambda i: (0, 0))]
                 + [pl.BlockSpec((None, D_MODEL, w), lambda i: (layer, 0, 0), pipeline_mode=pl.Buffered(1))
                    for w in PROJ_WIDTHS],
        out_specs=[row(D_MODEL)] + [row(w) for w in PROJ_WIDTHS],
        out_shape=[jax.ShapeDtypeStruct((n, D_MODEL), BF16)]
                  + [jax.ShapeDtypeStruct((n, w), F32) for w in PROJ_WIDTHS],
        compiler_params=_params(("parallel",)),
        name="proj",
    )(x, mods, g, *ws)


QK_W = ATT_WIDTH + ATT_KV_WIDTH


def _attn_prep_kernel(qkv_ref, g_ref, cos_ref, sin_ref, bd_ref, q_ref, kt_ref, v_ref):
    x = qkv_ref[...]
    qk = x[:, :QK_W]
    sq = qk * qk
    hi = sq.astype(BF16)
    lo = (sq - hi.astype(F32)).astype(BF16)
    bd = bd_ref[...]
    ms = jnp.concatenate([_dot(hi[:, t * LANES:(t + 1) * LANES], bd) + _dot(lo[:, t * LANES:(t + 1) * LANES], bd)
                          for t in range(QK_W // LANES)], axis=1)
    y = qk * lax.rsqrt(ms + EPS) * g_ref[...]
    cos = cos_ref[...]
    sin = sin_ref[...]
    lane = lax.broadcasted_iota(jnp.int32, (x.shape[0], LANES), 1)
    first_half = (lane % HEAD_DIM) < (HEAD_DIM // 2)
    tiles = []
    for t in range(QK_W // LANES):
        yt = y[:, t * LANES:(t + 1) * LANES]
        rot = jnp.where(first_half, pltpu.roll(yt, LANES - HEAD_DIM // 2, 1), pltpu.roll(yt, HEAD_DIM // 2, 1))
        tiles.append(yt * cos + rot * sin)
    for h in range(ATT_Q_HEADS):
        t = tiles[h // 2]
        q_ref[h] = (t[:, (h % 2) * HEAD_DIM:(h % 2 + 1) * HEAD_DIM] * ATT_SCALE).astype(BF16)
    kt = tiles[ATT_WIDTH // LANES].T
    kt_ref[0] = kt[:HEAD_DIM].astype(BF16)
    kt_ref[1] = kt[HEAD_DIM:].astype(BF16)
    v = x[:, QK_W:QK_W + LANES]
    one_col = jnp.where(lane == HEAD_DIM, 1.0, 0.0)
    v_ref[0] = jnp.where(lane < HEAD_DIM, v, one_col).astype(BF16)
    v_ref[1] = jnp.where(lane < HEAD_DIM, pltpu.roll(v, HEAD_DIM, 1), one_col).astype(BF16)


def _attn_prep(qkv, g640, cos_t, sin_t, bd):
    s = cos_t.shape[0]
    tm = PREP_TM
    skip = TM // tm - CTX_LEN // tm
    src = lambda i: jnp.where(i < CTX_LEN // tm, i, i + skip)
    return pl.pallas_call(
        _attn_prep_kernel,
        grid=(s // tm,),
        in_specs=[pl.BlockSpec((tm, ATT_WIDTH + 2 * ATT_KV_WIDTH), lambda i: (src(i), 0)),
                  pl.BlockSpec((1, QK_W), lambda i: (0, 0)),
                  pl.BlockSpec((tm, LANES), lambda i: (i, 0)),
                  pl.BlockSpec((tm, LANES), lambda i: (i, 0)),
                  pl.BlockSpec((LANES, LANES), lambda i: (0, 0))],
        out_specs=[pl.BlockSpec((ATT_Q_HEADS, tm, HEAD_DIM), lambda i: (0, i, 0)),
                   pl.BlockSpec((ATT_KV_HEADS, HEAD_DIM, tm), lambda i: (0, 0, i)),
                   pl.BlockSpec((ATT_KV_HEADS, tm, LANES), lambda i: (0, i, 0))],
        out_shape=[jax.ShapeDtypeStruct((ATT_Q_HEADS, s, HEAD_DIM), BF16),
                   jax.ShapeDtypeStruct((ATT_KV_HEADS, HEAD_DIM, s), BF16),
                   jax.ShapeDtypeStruct((ATT_KV_HEADS, s, LANES), BF16)],
        compiler_params=_params(("parallel",)),
        name="attn_prep",
    )(qkv, g640, cos_t, sin_t, bd)


def _heads_out(acc, rows):
    outs = []
    for h in range(GQA_GROUP):
        a = acc[h * rows:(h + 1) * rows]
        outs.append(a[:, :HEAD_DIM] / a[:, HEAD_DIM:HEAD_DIM + 1])
    return jnp.concatenate(outs, axis=1)


def _glb_kernel(q_ref, kt_ref, v_ref, o_ref, *, nk):
    tq = q_ref.shape[1]
    m_rows = GQA_GROUP * tq
    q = q_ref[...].reshape(m_rows, HEAD_DIM)

    def body(c, carry):
        m, acc = carry
        s = _dot(q, kt_ref[c])
        m_new = jnp.maximum(m, jnp.max(s, axis=1, keepdims=True))
        p = jnp.exp(s - m_new)
        acc = jnp.exp(m - m_new) * acc + _dot(p.astype(BF16), v_ref[c])
        return m_new, acc

    m0 = jnp.full((m_rows, 1), NEG_INF, F32)
    acc0 = jnp.zeros((m_rows, LANES), F32)
    _, acc = lax.fori_loop(0, nk, body, (m0, acc0), unroll=True)
    o_ref[...] = _heads_out(acc, tq).astype(BF16)


def _glb_attn(q, kt, v, n_rows):
    s = q.shape[1]
    nk, tk = kt.shape[1], kt.shape[3]
    tq = GLB_TQ
    nq = (s - CTX_LEN) // tq
    half = GQA_GROUP * HEAD_DIM
    return pl.pallas_call(
        functools.partial(_glb_kernel, nk=nk),
        grid=(ATT_KV_HEADS, nq),
        in_specs=[pl.BlockSpec((GQA_GROUP, tq, HEAD_DIM), lambda kv, i: (kv, i + CTX_LEN // tq, 0)),
                  pl.BlockSpec((None, nk, HEAD_DIM, tk), lambda kv, i: (kv, 0, 0, 0)),
                  pl.BlockSpec((None, nk, tk, LANES), lambda kv, i: (kv, 0, 0, 0))],
        out_specs=pl.BlockSpec((tq, half), lambda kv, i: (i + TM // tq, kv)),
        out_shape=jax.ShapeDtypeStruct((n_rows, ATT_WIDTH), BF16),
        compiler_params=_params(("arbitrary", "arbitrary")),
        name="glb_attn",
    )(q, kt, v)


def _swa_bias():
    w, tq = SWA_WINDOW, SWA_TQ
    nkw = tq + 2 * w
    qi = np.arange(tq)[:, None]
    kj = np.arange(nkw)[None, :]
    ok = (kj >= qi) & (kj <= qi + 2 * w)
    oks = np.stack([ok, ok & (kj >= w), ok & (kj < nkw - w)])
    return jnp.asarray(np.where(oks, 0.0, NEG_INF).astype(np.float32))


def _swa_kernel(sink_ref, bias_ref, *refs, nt):
    bps = SWA_TQ // SWA_WINDOW
    nkb = SWA_NSUB * bps + 2
    q_refs, refs = refs[:SWA_NSUB], refs[SWA_NSUB:]
    kt_refs, ktx_ref = refs[:nkb], refs[nkb]
    v_refs, vx_ref, o_ref = refs[nkb + 1:2 * nkb + 1], refs[2 * nkb + 1], refs[2 * nkb + 2]
    n = pl.program_id(0)
    tq = SWA_TQ
    rows = GQA_GROUP * tq
    for u in range(SWA_NSUB):
        first = jnp.logical_and(n == 0, u == 0)
        last = jnp.logical_and(n == nt - 1, u == SWA_NSUB - 1)
        bias = bias_ref[jnp.where(first, 1, jnp.where(last, 2, 0))]
        bias = jnp.concatenate([bias] * GQA_GROUP, axis=0)
        outs = []
        for kv in range(ATT_KV_HEADS):
            q = q_refs[u][kv * GQA_GROUP:(kv + 1) * GQA_GROUP].reshape(rows, HEAD_DIM)
            kt = jnp.concatenate([r[kv] for r in kt_refs[u * bps:u * bps + bps + 2]], axis=1)
            s = jnp.concatenate([_dot(q, kt) + bias, _dot(q, ktx_ref[kv])], axis=1)
            sink = jnp.concatenate([jnp.full((tq, 1), sink_ref[kv * GQA_GROUP + h], F32) for h in range(GQA_GROUP)],
                                   axis=0)
            m = jnp.maximum(jnp.max(s, axis=1, keepdims=True), sink)
            v = jnp.concatenate([r[kv] for r in v_refs[u * bps:u * bps + bps + 2]] + [vx_ref[kv]], axis=0)
            acc = _dot(jnp.exp(s - m).astype(BF16), v)
            lane = lax.broadcasted_iota(jnp.int32, acc.shape, 1)
            acc = acc + jnp.where(lane == HEAD_DIM, jnp.exp(sink - m), 0.0)
            outs.append(_heads_out(acc, tq))
        o_ref[u * tq:(u + 1) * tq, :] = jnp.concatenate(outs, axis=1).astype(BF16)


def _swa_attn(q, kt, v, sink, n_rows):
    s = q.shape[1]
    w = SWA_WINDOW
    ts = SWA_NSUB * SWA_TQ
    nb = (s - CTX_LEN) // w
    nt = (s - CTX_LEN) // ts
    nkb = ts // w + 2
    c0 = CTX_LEN // w
    blk = lambda b: (lambda n: jnp.clip(n * (ts // w) - 1 + b, 0, nb - 1) + c0)
    kt_spec = lambda f: pl.BlockSpec((ATT_KV_HEADS, HEAD_DIM, w), lambda n: (0, 0, f(n)))
    v_spec = lambda f: pl.BlockSpec((ATT_KV_HEADS, w, LANES), lambda n: (0, f(n), 0))
    return pl.pallas_call(
        functools.partial(_swa_kernel, nt=nt),
        grid=(nt,),
        in_specs=[pl.BlockSpec(memory_space=pltpu.SMEM),
                  pl.BlockSpec((3, SWA_TQ, SWA_TQ + 2 * w), lambda n: (0, 0, 0))]
                 + [pl.BlockSpec((ATT_Q_HEADS, SWA_TQ, HEAD_DIM),
                                 lambda n, u=u: (0, n * SWA_NSUB + u + CTX_LEN // SWA_TQ, 0)) for u in range(SWA_NSUB)]
                 + [kt_spec(blk(b)) for b in range(nkb)]
                 + [pl.BlockSpec((ATT_KV_HEADS, HEAD_DIM, CTX_LEN), lambda n: (0, 0, 0))]
                 + [v_spec(blk(b)) for b in range(nkb)]
                 + [pl.BlockSpec((ATT_KV_HEADS, CTX_LEN, LANES), lambda n: (0, 0, 0))],
        out_specs=pl.BlockSpec((ts, ATT_WIDTH), lambda n: (n + TM // ts, 0)),
        out_shape=jax.ShapeDtypeStruct((n_rows, ATT_WIDTH), BF16),
        compiler_params=_params(("parallel",)),
        name="swa_attn",
    )(sink, _swa_bias(), *([q] * SWA_NSUB), *([kt] * (nkb + 1)), *([v] * (nkb + 1)))


def _ctx_attn_kernel(sink_ref, q_ref, kt_ref, v_ref, prev_ref, o_ref, *, use_sink):
    del prev_ref
    rows = GQA_GROUP * CTX_LEN
    outs = []
    for kv in range(ATT_KV_HEADS):
        q = q_ref[kv * GQA_GROUP:(kv + 1) * GQA_GROUP].reshape(rows, HEAD_DIM)
        s = _dot(q, kt_ref[kv])
        m = jnp.max(s, axis=1, keepdims=True)
        if use_sink:
            sink = jnp.concatenate([jnp.full((CTX_LEN, 1), sink_ref[kv * GQA_GROUP + h], F32)
                                    for h in range(GQA_GROUP)], axis=0)
            m = jnp.maximum(m, sink)
        acc = _dot(jnp.exp(s - m).astype(BF16), v_ref[kv])
        if use_sink:
            lane = lax.broadcasted_iota(jnp.int32, acc.shape, 1)
            acc = acc + jnp.where(lane == HEAD_DIM, jnp.exp(sink - m), 0.0)
        outs.append(_heads_out(acc, CTX_LEN))
    o_ref[:CTX_LEN] = jnp.concatenate(outs, axis=1).astype(BF16)
    o_ref[CTX_LEN:] = jnp.zeros((TM - CTX_LEN, ATT_WIDTH), BF16)


def _ctx_attn(q, kt, v, sink, y_prev, use_sink):
    return pl.pallas_call(
        functools.partial(_ctx_attn_kernel, use_sink=use_sink),
        grid=(1,),
        in_specs=[pl.BlockSpec(memory_space=pltpu.SMEM),
                  pl.BlockSpec((ATT_Q_HEADS, CTX_LEN, HEAD_DIM), lambda i: (0, 0, 0)),
                  pl.BlockSpec((ATT_KV_HEADS, HEAD_DIM, CTX_LEN), lambda i: (0, 0, 0)),
                  pl.BlockSpec((ATT_KV_HEADS, CTX_LEN, LANES), lambda i: (0, 0, 0)),
                  pl.BlockSpec(memory_space=pl.ANY)],
        out_specs=pl.BlockSpec((TM, ATT_WIDTH), lambda i: (0, 0)),
        out_shape=jax.ShapeDtypeStruct(y_prev.shape, y_prev.dtype),
        input_output_aliases={4: 0},
        compiler_params=_params(("arbitrary",)),
        name="ctx_attn",
    )(sink, q, kt, v, y_prev)


def _ssd_activate(b, nblk, xp, xc, xn, dt, w, bias, dtb):
    t = SSD_T
    has_prev = jnp.logical_and(b != 0, b != TM // t)
    has_next = jnp.logical_and(b != CTX_LEN // t - 1, b != nblk - 1)
    prev = jnp.where(has_prev, xp, 0.0)
    nxt = jnp.where(has_next, xn, 0.0)
    row = lax.broadcasted_iota(jnp.int32, xc.shape, 0)
    halo = prev.shape[0]
    rep = lambda v: jnp.concatenate([v] * (t // halo), axis=0)
    acc = xc * w[2:3, :] + bias
    for k in (1, 2):
        down = jnp.where(row < k, rep(pltpu.roll(prev, k, 0)), pltpu.roll(xc, k, 0))
        up = jnp.where(row >= t - k, rep(pltpu.roll(nxt, halo - k, 0)), pltpu.roll(xc, t - k, 0))
        acc = acc + down * w[2 - k:3 - k, :] + up * w[2 + k:3 + k, :]
    d = dt + dtb
    return _silu(acc), jnp.maximum(d, 0.0) + jnp.log1p(jnp.exp(-jnp.abs(d)))


def _ssd_scan_kernel(tbl_ref, xp_ref, xc_ref, xn_ref, dt_ref, w_ref, b_ref, dtb_ref, arow_ref, tri_ref, e_ref, *rest,
                     direction, final, nblk):
    if final:
        yf_ref, z_ref, dsk_ref, ng_ref, o_ref, st_ref = rest
    else:
        o_ref, st_ref = rest
    step = pl.program_id(0)
    flag = tbl_ref[1, step]
    t = SSD_T
    gw = SSD_INNER // SSD_GROUPS
    hpg = SSD_HEADS // SSD_GROUPS

    @pl.when(flag == 2)
    def _():
        o_ref[...] = jnp.zeros_like(o_ref)

    @pl.when(flag == 1)
    def _():
        st_ref[...] = jnp.zeros_like(st_ref)

    @pl.when(flag != 2)
    def _():
        x, dt = _ssd_activate(tbl_ref[0, step], nblk, xp_ref[...], xc_ref[...], xn_ref[...], dt_ref[...],
                              w_ref[...], b_ref[...], dtb_ref[...])
        xs = x[:, :SSD_INNER]
        bm = x[:, SSD_INNER:SSD_INNER + SSD_BC]
        cm = x[:, SSD_INNER + SSD_BC:]
        adt = dt * arow_ref[...]
        tri = tri_ref[...]
        keep = tri > 0
        cum = _split_rhs_dot(tri, adt, 3)
        tot = cum[t - 1:t, :] if direction == 0 else cum[0:1, :]
        e = e_ref[...]
        dt_e = _split_lhs_dot(dt, e, 2)
        expc_e = _split_lhs_dot(jnp.exp(cum), e, 2)
        decs_e = _split_lhs_dot(jnp.exp(tot - cum), e, 2)
        dch_e = _split_lhs_dot(jnp.broadcast_to(jnp.exp(tot), (8, DT_PAD)), e, 2)[0:1]
        cum_t = cum.T
        xdt = xs * dt_e
        ys = []
        for g in range(SSD_GROUPS):
            bg = bm[:, g * SSD_STATE:(g + 1) * SSD_STATE]
            cgb = cm[:, g * SSD_STATE:(g + 1) * SSD_STATE].astype(BF16)
            cb = _dot_nt(cgb, bg.astype(BF16))
            stg = st_ref[g]
            y_off = _dot(cgb, stg.astype(BF16)) * expc_e[:, g * gw:(g + 1) * gw]
            parts = []
            for j in range(hpg):
                h = g * hpg + j
                r = direction * SSD_HEADS + h
                diff = cum[:, r:r + 1] - cum_t[r:r + 1, :]
                lmat = jnp.exp(jnp.where(keep, diff, NEG_INF))
                wmat = (cb * lmat).astype(BF16)
                parts.append(_dot(wmat, xdt[:, h * SSD_HEAD_DIM:(h + 1) * SSD_HEAD_DIM].astype(BF16)))
            ys.append(jnp.concatenate(parts, axis=1) + y_off)
            xw = (xdt[:, g * gw:(g + 1) * gw] * decs_e[:, g * gw:(g + 1) * gw]).astype(BF16)
            st_ref[g] = stg * dch_e[:, g * gw:(g + 1) * gw] + _dot(bg.T.astype(BF16), xw)
        y = jnp.concatenate(ys, axis=1)
        if final:
            y = yf_ref[...] + y + dsk_ref[...] * xs
            y = y * _silu(z_ref[...])
            ms = jnp.mean(y * y, axis=-1, keepdims=True)
            o_ref[...] = (y * lax.rsqrt(ms + EPS) * ng_ref[...]).astype(BF16)
        else:
            o_ref[...] = y


def _ssd_scan(tbl, xbc, dt, conv, arow, tri, e, direction, extra=None):
    n = xbc.shape[0]
    t = SSD_T
    final = extra is not None
    blk = lambda w: pl.BlockSpec((t, w), lambda s, tb: (tb[0, s], 0))
    const = lambda r, w: pl.BlockSpec((r, w), lambda s, tb: (0, 0))
    halo_p = pl.BlockSpec((8, SSD_XBC), lambda s, tb: (jnp.maximum(tb[0, s] * (t // 8) - 1, 0), 0))
    halo_n = pl.BlockSpec((8, SSD_XBC), lambda s, tb: (jnp.minimum((tb[0, s] + 1) * (t // 8), n // 8 - 1), 0))
    in_specs = [halo_p, blk(SSD_XBC), halo_n, blk(DT_PAD), const(8, SSD_XBC), const(1, SSD_XBC), const(1, DT_PAD),
                const(1, DT_PAD), const(t, t), const(DT_PAD, SSD_INNER)]
    args = [xbc, xbc, xbc, dt, *conv, arow, tri, e]
    if final:
        in_specs += [blk(SSD_INNER), blk(SSD_INNER), const(1, SSD_INNER), const(1, SSD_INNER)]
        args += list(extra)
    return pl.pallas_call(
        functools.partial(_ssd_scan_kernel, direction=direction, final=final, nblk=n // t),
        grid_spec=pltpu.PrefetchScalarGridSpec(
            num_scalar_prefetch=1, grid=(tbl.shape[1],), in_specs=in_specs,
            out_specs=blk(SSD_INNER),
            scratch_shapes=[pltpu.VMEM((SSD_GROUPS, SSD_STATE, SSD_INNER // SSD_GROUPS), F32)]),
        out_shape=jax.ShapeDtypeStruct((n, SSD_INNER), BF16 if final else F32),
        compiler_params=_params(("arbitrary",)),
        name="ssd_scan_bwd" if final else "ssd_scan_fwd",
    )(tbl, *args)


def _ssd_tables(n_rows):
    t = SSD_T
    ctx = list(range(CTX_LEN // t))
    pad = list(range(CTX_LEN // t, TM // t))
    lat = list(range(TM // t, n_rows // t))
    fwd = ctx + lat + pad
    bwd = ctx[::-1] + lat[::-1] + pad
    flags = [1] + [0] * (len(ctx) + len(lat) - 1) + [2] * len(pad)
    return (jnp.asarray(np.array([fwd, flags], np.int32)), jnp.asarray(np.array([bwd, flags], np.int32)))


def _dot_nt_split(a, b):
    a_hi = a.astype(BF16)
    b_hi = b.astype(BF16)
    a_lo = (a - a_hi.astype(F32)).astype(BF16)
    b_lo = (b - b_hi.astype(F32)).astype(BF16)
    return _dot_nt(a_hi, b_hi) + _dot_nt(a_hi, b_lo) + _dot_nt(a_lo, b_hi)


def _s5_setup_kernel(pa_ref, pb_ref, pc_ref, tile_ref, m_ref, bs_ref, csa_ref, csb_ref, lre_ref, lim_ref):
    t_len = S5_T
    hh = S5_GROUP
    a_re = pa_ref[0:1, :]
    a_im = pa_ref[1:2, :]
    dt = jnp.exp(pa_ref[2:3, :])
    mag = jnp.exp(a_re * dt)
    lr = mag * jnp.cos(a_im * dt)
    li = mag * jnp.sin(a_im * dt)
    den = a_re * a_re + a_im * a_im
    nr = lr - 1.0
    f_re = (nr * a_re + li * a_im) / den
    f_im = (li * a_re - nr * a_im) / den
    b_re, b_im = pb_ref[0:hh], pb_ref[hh:2 * hh]
    c_re, c_im = pc_ref[0:hh], pc_ref[hh:2 * hh]
    bb_re = f_re * b_re - f_im * b_im
    bb_im = f_re * b_im + f_im * b_re
    pw = [(jnp.ones_like(lr), jnp.zeros_like(lr))]
    for _ in range(t_len):
        pr, pi = pw[-1]
        pw.append((pr * lr - pi * li, pr * li + pi * lr))
    fwd = lax.broadcasted_iota(jnp.int32, lr.shape, 1) < S5_STATE
    pick = lambda kf, kb: (jnp.where(fwd, pw[kf][0], pw[kb][0]), jnp.where(fwd, pw[kf][1], pw[kb][1]))
    bs_re, bs_im, cs_re, cs_im = [], [], [], []
    for t in range(t_len):
        er, ei = pick(t_len - 1 - t, t)
        bs_re.append(er * bb_re - ei * bb_im)
        bs_im.append(er * bb_im + ei * bb_re)
        fr, fi = pick(t + 1, t_len - t)
        cs_re.append(c_re * fr - c_im * fi)
        cs_im.append(c_re * fi + c_im * fr)
    bs_full = jnp.concatenate([jnp.concatenate(bs_re, axis=0), jnp.concatenate(bs_im, axis=0)], axis=1)
    cs_full = jnp.concatenate([jnp.concatenate(cs_re, axis=0), -jnp.concatenate(cs_im, axis=0)], axis=1)
    wide = bs_full.shape
    fwd_w = (lax.broadcasted_iota(jnp.int32, wide, 1) % (2 * S5_STATE)) < S5_STATE
    bs_ref[...] = bs_full.astype(BF16)
    csa_ref[...] = jnp.where(fwd_w, cs_full, 0.0).astype(BF16)
    csb_ref[...] = jnp.where(fwd_w, 0.0, cs_full).astype(BF16)
    lre_ref[...] = pw[t_len][0]
    lim_ref[...] = pw[t_len][1]
    cc = jnp.concatenate([c_re, -c_im], axis=1)
    cc = jnp.concatenate([cc, jnp.zeros((LANES - hh, wide[1]), F32)], axis=0)
    fwd_c = (lax.broadcasted_iota(jnp.int32, cc.shape, 1) % (2 * S5_STATE)) < S5_STATE
    kr_f = _dot_nt_split(bs_full, jnp.where(fwd_c, cc, 0.0))
    k_b = _dot_nt_split(bs_full, jnp.where(fwd_c, 0.0, cc))
    tile = tile_ref[...]
    kr_w = _split_lhs_dot(kr_f, tile, 3)
    kb_w = _split_lhs_dot(k_b, tile, 3)
    lane_blk = lax.broadcasted_iota(jnp.int32, wide, 1) // hh
    m = jnp.zeros(wide, F32)
    for t in range(t_len):
        up, dn = kr_w, kb_w
        if t < t_len - 1:
            r = (t_len - 1 - t) * hh
            up = jnp.concatenate([kr_w[r:], jnp.zeros((r, wide[1]), F32)], axis=0)
        if t > 0:
            r = t * hh
            dn = jnp.concatenate([jnp.zeros((r, wide[1]), F32), kb_w[:wide[0] - r]], axis=0)
        m = m + jnp.where(lane_blk == t, up + dn, 0.0)
    m_ref[...] = m.astype(BF16)


def _s5_setup(a_re, a_im, log_dt, b_re, b_im, c_re, c_im):
    g = a_re.shape[1]
    tw = S5_T * S5_GROUP
    both = lambda v: jnp.concatenate([v[0], v[1]], axis=-1)
    ldt = jnp.broadcast_to(log_dt[:, :, None], (2, g, S5_STATE))
    pa = jnp.stack([both(a_re), both(a_im), both(ldt)], axis=1).astype(F32)
    pa = jnp.pad(pa, ((0, 0), (0, 5), (0, 0)))
    dup = lambda v: jnp.concatenate([v, v], axis=-1)
    pb = jnp.concatenate([dup(jnp.swapaxes(b_re, 1, 2)), dup(jnp.swapaxes(b_im, 1, 2))], axis=1).astype(F32)
    pc = jnp.concatenate([dup(c_re), dup(c_im)], axis=1).astype(F32)
    tile = jnp.asarray(np.tile(np.eye(LANES, S5_GROUP).astype(np.float32), (1, S5_T)), BF16)
    per_g = lambda r, c: pl.BlockSpec((None, r, c), lambda i: (i, 0, 0))
    mat = jax.ShapeDtypeStruct((g, tw, tw), BF16)
    lam = jax.ShapeDtypeStruct((g, 1, 2 * S5_STATE), F32)
    m, bs, csa, csb, lre, lim = pl.pallas_call(
        _s5_setup_kernel,
        grid=(g,),
        in_specs=[per_g(8, 2 * S5_STATE), per_g(2 * S5_GROUP, 2 * S5_STATE), per_g(2 * S5_GROUP, 2 * S5_STATE),
                  pl.BlockSpec((LANES, tw), lambda i: (0, 0))],
        out_specs=[per_g(tw, tw)] * 4 + [per_g(1, 2 * S5_STATE)] * 2,
        out_shape=[mat] * 4 + [lam] * 2,
        compiler_params=_params(("parallel",)),
        name="s5_setup",
    )(pa, pb, pc, tile)
    return m, bs, csa, csb, lre[:, 0], lim[:, 0]


def _s5_kernel(u_ref, m_ref, bs_ref, csa_ref, csb_ref, lre_ref, lim_ref, y_ref, v_ref, sa_ref, sb_ref, *, nctx, npad, nch):
    gb = S5_GB
    for j in range(gb):
        v = _dot(u_ref[j], bs_ref[j])
        v_ref.at[0][pl.ds(j, nch, stride=gb), :] = v[:, :LANES]
        v_ref.at[1][pl.ds(j, nch, stride=gb), :] = v[:, LANES:]
    lre = lre_ref[0]
    lim = lim_ref[0]
    is_fwd = lax.broadcasted_iota(jnp.int32, (gb, LANES), 1) < S5_STATE

    nlat = nch - nctx - npad

    def body(i, carry):
        sre, sim = carry
        tail = i - nlat
        rf = jnp.where(i < nctx, i, jnp.where(i < nctx + nlat, i + npad, tail)) * gb
        rb = jnp.where(i < nctx, nctx - 1 - i, jnp.where(i < nctx + nlat, nch + nctx - 1 - i, tail)) * gb
        sa_ref[0, pl.ds(rf, gb), :] = sre
        sa_ref[1, pl.ds(rf, gb), :] = sim
        sb_ref[0, pl.ds(rb, gb), :] = sre
        sb_ref[1, pl.ds(rb, gb), :] = sim
        vre = jnp.where(is_fwd, v_ref[0, pl.ds(rf, gb), :], v_ref[0, pl.ds(rb, gb), :])
        vim = jnp.where(is_fwd, v_ref[1, pl.ds(rf, gb), :], v_ref[1, pl.ds(rb, gb), :])
        return lre * sre - lim * sim + vre, lre * sim + lim * sre + vim

    zero = jnp.zeros((gb, LANES), F32)
    lax.fori_loop(0, nch, body, (zero, zero))
    for j in range(gb):
        rows = pl.ds(j, nch, stride=gb)
        sa = jnp.concatenate([sa_ref.at[0][rows, :], sa_ref.at[1][rows, :]], axis=1).astype(BF16)
        sb = jnp.concatenate([sb_ref.at[0][rows, :], sb_ref.at[1][rows, :]], axis=1).astype(BF16)
        y_ref[j] = _dot(u_ref[j], m_ref[j]) + _dot_nt(sa, csa_ref[j]) + _dot_nt(sb, csb_ref[j])


def _s5_scan(ug, m, bs, csa, csb, lam_re, lam_im):
    g, nch, tw = ug.shape
    gb = S5_GB
    sw = 4 * S5_STATE
    blk = lambda a, b: pl.BlockSpec((gb, a, b), lambda i: (i, 0, 0))
    lam_spec = pl.BlockSpec((1, gb, 2 * S5_STATE), lambda i: (i, 0, 0))
    return pl.pallas_call(
        functools.partial(_s5_kernel, nctx=CTX_LEN // S5_T, npad=(TM - CTX_LEN) // S5_T, nch=nch),
        grid=(g // gb,),
        in_specs=[blk(nch, tw), blk(tw, tw), blk(tw, sw), blk(tw, sw), blk(tw, sw), lam_spec, lam_spec],
        out_specs=blk(nch, tw),
        out_shape=jax.ShapeDtypeStruct((g, nch, tw), F32),
        scratch_shapes=[pltpu.VMEM((2, nch * gb, LANES), F32) for _ in range(3)],
        compiler_params=_params(("parallel",)),
        name="s5_scan",
    )(ug, m, bs, csa, csb, lam_re.reshape(g // gb, gb, -1), lam_im.reshape(g // gb, gb, -1))


def _lane_block_move(tiles, src_blk, dst_blk):
    gpt = LANES // S5_GROUP
    blk = lax.broadcasted_iota(jnp.int32, tiles[0].shape, 1) // S5_GROUP
    out = jnp.zeros(tiles[0].shape, tiles[0].dtype)
    for tile, a, b in zip(tiles, src_blk, dst_blk):
        moved = tile if a == b else pltpu.roll(tile, ((b - a) % gpt) * S5_GROUP, 1)
        out = jnp.where(blk == b, moved, out)
    return out


def _s5_gather_kernel(u_ref, o_ref):
    rb = o_ref.shape[1]
    gpt = LANES // S5_GROUP
    steps = [u_ref[pl.ds(t, rb, stride=S5_T), :] for t in range(S5_T)]
    for q in range(gpt):
        halves = [_lane_block_move(steps[k * gpt:(k + 1) * gpt], [q] * gpt, list(range(gpt)))
                  for k in range(S5_T // gpt)]
        o_ref[q] = jnp.concatenate(halves, axis=1).astype(BF16)


def _s5_gather(u):
    n = u.shape[0]
    gpt = LANES // S5_GROUP
    rb = n // S5_T // 4
    tw = S5_T * S5_GROUP
    return pl.pallas_call(
        _s5_gather_kernel,
        grid=(S5_WIDTH // LANES, 4),
        in_specs=[pl.BlockSpec((rb * S5_T, LANES), lambda j, r: (r, j))],
        out_specs=pl.BlockSpec((gpt, rb, tw), lambda j, r: (j, r, 0)),
        out_shape=jax.ShapeDtypeStruct((S5_GROUPS, n // S5_T, tw), BF16),
        compiler_params=_params(("parallel", "parallel")),
        name="s5_gather",
    )(u)


def _s5_scatter_kernel(y_ref, o_ref):
    rb = y_ref.shape[1]
    gpt = LANES // S5_GROUP
    ys = [y_ref[q] for q in range(gpt)]
    for t in range(S5_T):
        tiles = [y[:, (t // gpt) * LANES:(t // gpt + 1) * LANES] for y in ys]
        o_ref[pl.ds(t, rb, stride=S5_T), :] = _lane_block_move(tiles, [t % gpt] * gpt, list(range(gpt)))


def _s5_scatter(yg):
    g, nchp, tw = yg.shape
    gpt = LANES // S5_GROUP
    rb = nchp // 4
    return pl.pallas_call(
        _s5_scatter_kernel,
        grid=(S5_WIDTH // LANES, 4),
        in_specs=[pl.BlockSpec((gpt, rb, tw), lambda j, r: (j, r, 0))],
        out_specs=pl.BlockSpec((rb * S5_T, LANES), lambda j, r: (r, j)),
        out_shape=jax.ShapeDtypeStruct((nchp * S5_T, S5_WIDTH), F32),
        compiler_params=_params(("parallel", "parallel")),
        name="s5_scatter",
    )(yg)


def _s5_glu_kernel(y_ref, u_ref, d_ref, w_ref, b_ref, o_ref):
    y = jax.nn.gelu(y_ref[...] + d_ref[...] * u_ref[...])
    t = _dot(y.astype(BF16), w_ref[...]) + b_ref[...]
    o_ref[...] = (t[:, :S5_WIDTH] * jax.nn.sigmoid(t[:, S5_WIDTH:])).astype(BF16)


def _s5_glu(y, u, d, w, b, layer):
    n = y.shape[0]
    row = lambda wd: pl.BlockSpec((TM, wd), lambda i: (i, 0))
    return pl.pallas_call(
        _s5_glu_kernel,
        grid=(n // TM,),
        in_specs=[row(S5_WIDTH), row(S5_WIDTH), pl.BlockSpec((1, S5_WIDTH), lambda i: (0, 0)),
                  pl.BlockSpec((None, S5_WIDTH, 2 * S5_WIDTH), lambda i: (layer, 0, 0)),
                  pl.BlockSpec((1, 2 * S5_WIDTH), lambda i: (0, 0))],
        out_specs=row(S5_WIDTH),
        out_shape=jax.ShapeDtypeStruct((n, S5_WIDTH), BF16),
        compiler_params=_params(("parallel",)),
        name="s5_glu",
    )(y, u, d, w, b)


BR_WIDTHS = (SSD_INNER, S5_WIDTH, ATT_WIDTH, ATT_WIDTH)


def _merge_kernel(h_ref, *refs):
    y_refs, wg_refs, wb_refs, o_ref = refs[:4], refs[4:8], refs[8:12], refs[12]
    h = h_ref[...]
    acc = None
    for y_ref, wg_ref, wb_ref in zip(y_refs, wg_refs, wb_refs):
        t = jax.nn.sigmoid(_dot(h, wg_ref[...])) * _dot(y_ref[...], wb_ref[...])
        acc = t if acc is None else acc + t
    o_ref[...] = acc.astype(BF16)


def _merge(hb, ys, w_gate, w_brs, layer, tn=512):
    n = hb.shape[0]
    nj = D_MODEL // tn
    row = lambda w: pl.BlockSpec((TM, w), lambda i, j: (i, 0))
    return pl.pallas_call(
        _merge_kernel,
        grid=(n // TM, nj),
        in_specs=[row(D_MODEL)] + [row(w) for w in BR_WIDTHS]
                 + [pl.BlockSpec((None, D_MODEL, tn), lambda i, j, b=b: (layer, 0, b * nj + j)) for b in range(N_BRANCH)]
                 + [pl.BlockSpec((None, w, tn), lambda i, j: (layer, 0, j)) for w in BR_WIDTHS],
        out_specs=pl.BlockSpec((TM, tn), lambda i, j: (i, j)),
        out_shape=jax.ShapeDtypeStruct((n, D_MODEL), BF16),
        compiler_params=_params(("parallel", "arbitrary")),
        name="merge",
    )(hb, *ys, w_gate, w_gate, w_gate, w_gate, *w_brs)


def _out_kernel(x_ref, a_ref, mod_ref, w_ref, o_ref):
    o_ref[...] = x_ref[...] + mod_ref[5:6, :] * _dot(a_ref[...], w_ref[...])


def _out_proj(x, acc, mods, w_out, layer):
    n = x.shape[0]
    row = lambda: pl.BlockSpec((TM, D_MODEL), lambda i: (i, 0))
    return pl.pallas_call(
        _out_kernel,
        grid=(n // TM,),
        in_specs=[row(), row(), _mod_spec(layer),
                  pl.BlockSpec((None, D_MODEL, D_MODEL), lambda i: (layer, 0, 0), pipeline_mode=pl.Buffered(1))],
        out_specs=row(),
        out_shape=jax.ShapeDtypeStruct((n, D_MODEL), F32),
        compiler_params=_params(("parallel",)),
        name="out_proj",
    )(x, acc, mods, w_out)


def _rope_tables(seq):
    rows = seq // GRID_W
    row = jnp.repeat(jnp.arange(rows, dtype=F32), GRID_W)
    col = jnp.tile(jnp.arange(GRID_W, dtype=F32), rows)
    n_freq = HEAD_DIM // 4
    inv = ROPE_THETA ** (-jnp.arange(n_freq, dtype=F32) / n_freq)
    ang = jnp.concatenate([row[:, None] * inv, col[:, None] * inv], axis=-1)
    cos, sin = jnp.cos(ang), jnp.sin(ang)
    reps = LANES // HEAD_DIM
    cos_t = jnp.tile(jnp.concatenate([cos, cos], axis=-1), (1, reps))
    sin_t = jnp.tile(jnp.concatenate([-sin, sin], axis=-1), (1, reps))
    cos_t = jnp.concatenate([jnp.ones((CTX_LEN, LANES), F32), cos_t], axis=0)
    sin_t = jnp.concatenate([jnp.zeros((CTX_LEN, LANES), F32), sin_t], axis=0)
    return cos_t, sin_t


def _ssd_expand(direction):
    e = np.zeros((DT_PAD, SSD_INNER), np.float32)
    for h in range(SSD_HEADS):
        e[direction * SSD_HEADS + h, h * SSD_HEAD_DIM:(h + 1) * SSD_HEAD_DIM] = 1.0
    return jnp.asarray(e, BF16)


def _to_stream(ctx_rows, lat_rows):
    pad = jnp.zeros((TM - CTX_LEN,) + ctx_rows.shape[1:], ctx_rows.dtype)
    return jnp.concatenate([ctx_rows, pad, lat_rows], axis=0)


def kernel(x, c, ctx, c_ctx, w_mod, b_mod, norm_g, ffn_in, ffn_out, w_in, ssd_conv_w, ssd_conv_b, ssd_a_log, ssd_dt_bias, ssd_d, ssd_norm_g, s5_a_re, s5_a_im, s5_log_dt, s5_b_re, s5_b_im, s5_c_re, s5_c_im, s5_d, s5_glu_w, s5_glu_b, swa_qk_g, swa_sink, glb_qk_g, w_br_ssd, w_br_s5, w_br_swa, w_br_glb, w_out):
    assert x.shape[0] == 1 and ctx.shape[1] == CTX_LEN
    depth = w_mod.shape[0]
    seq = x.shape[1]
    n_rows = TM + seq
    s_len = CTX_LEN + seq

    ffn_in_b = ffn_in.astype(BF16).reshape(depth, 2, D_MODEL, 2 * D_FF // FFN_TF, FFN_TF).transpose(0, 1, 3, 2, 4)
    ffn_out_b = ffn_out.astype(BF16)
    *w_proj, w_gate = _split_w_in(w_in)
    w_brs = tuple(w.astype(BF16) for w in (w_br_ssd, w_br_s5, w_br_swa, w_br_glb))
    w_out_b = w_out.astype(BF16)
    glu_w_b = s5_glu_w.astype(BF16)

    mods = _mods(c, c_ctx, w_mod, b_mod)
    cos_t, sin_t = _rope_tables(seq)
    bd = jnp.asarray(np.kron(np.eye(LANES // HEAD_DIM), np.ones((HEAD_DIM, HEAD_DIM))) / HEAD_DIM, BF16)
    tbl_f, tbl_b = _ssd_tables(n_rows)
    tril = jnp.asarray(np.tril(np.ones((SSD_T, SSD_T), np.float32)), BF16)
    triu = jnp.asarray(np.triu(np.ones((SSD_T, SSD_T), np.float32)), BF16)
    e_dirs = (_ssd_expand(0), _ssd_expand(1))
    nk = s_len // GLB_TK

    xs = _to_stream(ctx[0], x[0])
    for i in range(depth):
        xs = _ffn(xs, mods, norm_g[i, 0:1], ffn_in_b, ffn_out_b, i, 0)
        hb, z, xbc, dt, u, qkv_swa, qkv_glb = _proj(xs, mods, norm_g[i, 1:2], w_proj, i)

        conv_w = jnp.pad(ssd_conv_w[i], ((0, 8 - SSD_CONV), (0, 0)))
        dt_bias = jnp.pad(ssd_dt_bias[i].reshape(1, -1), ((0, 0), (0, DT_PAD - 2 * SSD_HEADS)))
        conv = (conv_w, ssd_conv_b[i].reshape(1, -1), dt_bias)
        a_neg = -jnp.exp(ssd_a_log[i].astype(F32))
        arow = lambda d: jnp.zeros((1, DT_PAD), F32).at[0, d * SSD_HEADS:(d + 1) * SSD_HEADS].set(a_neg[d])
        y_f = _ssd_scan(tbl_f, xbc, dt, conv, arow(0), tril, e_dirs[0], 0)
        d_exp = jnp.repeat(ssd_d[i].astype(F32), SSD_HEAD_DIM).reshape(1, -1)
        y_ssd = _ssd_scan(tbl_b, xbc, dt, conv, arow(1), triu, e_dirs[1], 1,
                          extra=(y_f, z, d_exp, ssd_norm_g[i].reshape(1, -1)))

        s5m, s5bs, s5csa, s5csb, lam_re, lam_im = _s5_setup(
            s5_a_re[i], s5_a_im[i], s5_log_dt[i], s5_b_re[i], s5_b_im[i], s5_c_re[i], s5_c_im[i])
        yg = _s5_scan(_s5_gather(u), s5m, s5bs, s5csa, s5csb, lam_re, lam_im)
        y_s5 = _s5_glu(_s5_scatter(yg), u, s5_d[i].reshape(1, -1), glu_w_b, s5_glu_b[i].reshape(1, -1), i)

        g_swa = jnp.concatenate([jnp.tile(swa_qk_g[i, 0], ATT_Q_HEADS), jnp.tile(swa_qk_g[i, 1], ATT_KV_HEADS)])[None]
        q, kt, v = _attn_prep(qkv_swa, g_swa, cos_t, sin_t, bd)
        y_swa = _swa_attn(q, kt, v, swa_sink[i], n_rows)
        y_swa = _ctx_attn(q, kt, v, swa_sink[i], y_swa, True)

        g_glb = jnp.concatenate([jnp.tile(glb_qk_g[i, 0], ATT_Q_HEADS), jnp.tile(glb_qk_g[i, 1], ATT_KV_HEADS)])[None]
        q, kt, v = _attn_prep(qkv_glb, g_glb, cos_t, sin_t, bd)
        kt_c = kt.reshape(ATT_KV_HEADS, HEAD_DIM, nk, GLB_TK).transpose(0, 2, 1, 3)
        y_glb = _glb_attn(q, kt_c, v.reshape(ATT_KV_HEADS, nk, GLB_TK, LANES), n_rows)
        y_glb = _ctx_attn(q, kt, v, swa_sink[i], y_glb, False)

        acc = _merge(hb, (y_ssd, y_s5, y_swa, y_glb), w_gate, w_brs, i)
        xs = _out_proj(xs, acc, mods, w_out_b, i)
        xs = _ffn(xs, mods, norm_g[i, 2:3], ffn_in_b, ffn_out_b, i, 2, out_rows=seq if i == depth - 1 else None)
    return xs[None]
```

```python
import functools
import math

import numpy as np
import jax
import jax.numpy as jnp
from jax import lax
from jax.experimental import pallas as pl
from jax.experimental.pallas import tpu as pltpu

F32 = jnp.float32
BF16 = jnp.bfloat16

D_MODEL = 2048
GRID_W = 64
CTX_LEN = 256
N_MOD = 9
D_FF = 5632
SSD_HEADS = 12
SSD_HEAD_DIM = 64
SSD_INNER = SSD_HEADS * SSD_HEAD_DIM
SSD_GROUPS = 2
SSD_STATE = 128
SSD_BC = SSD_GROUPS * SSD_STATE
SSD_XBC = SSD_INNER + 2 * SSD_BC
SSD_CONV = 5
S5_WIDTH = 512
S5_GROUP = 16
S5_GROUPS = S5_WIDTH // S5_GROUP
S5_STATE = 64
HEAD_DIM = 64
ATT_Q_HEADS = 8
ATT_KV_HEADS = 2
GQA_GROUP = ATT_Q_HEADS // ATT_KV_HEADS
ATT_WIDTH = ATT_Q_HEADS * HEAD_DIM
ATT_KV_WIDTH = ATT_KV_HEADS * HEAD_DIM
ATT_SCALE = HEAD_DIM ** -0.5
SWA_WINDOW = 128
ROPE_THETA = 10000.0
NEG_INF = -1e30
N_BRANCH = 4
EPS = 1e-6

LANES = 128
VMEM_LIMIT = 56 * 1024 * 1024

TM = 512
FFN_TF = 512
SSD_T = 128
S5_T = 16
S5_GB = 4
PROJ_TM = 512
PREP_TM = 256
GLB_TQ = 256
SWA_TQ = 256
SWA_NSUB = 1
GLB_TK = 1280
DT_PAD = LANES


def _params(sem, vmem=VMEM_LIMIT):
    return pltpu.CompilerParams(dimension_semantics=sem, vmem_limit_bytes=vmem)


def _silu(x):
    return x * jax.nn.sigmoid(x)


def _norm_mod(x, g, shift, scale):
    ms = jnp.mean(x * x, axis=-1, keepdims=True)
    return (x * lax.rsqrt(ms + EPS)) * (g * (1.0 + scale)) + shift


def _dot(a, b):
    return jnp.dot(a, b, preferred_element_type=F32)


def _dot_nt(a, b):
    return lax.dot_general(a, b, (((1,), (1,)), ((), ())), preferred_element_type=F32)


def _split_rhs_dot(m, x, parts):
    acc = None
    r = x
    for _ in range(parts):
        hi = r.astype(BF16)
        t = _dot(m, hi)
        acc = t if acc is None else acc + t
        r = r - hi.astype(F32)
    return acc


def _split_lhs_dot(x, m, parts):
    acc = None
    r = x
    for _ in range(parts):
        hi = r.astype(BF16)
        t = _dot(hi, m)
        acc = t if acc is None else acc + t
        r = r - hi.astype(F32)
    return acc


def _mod_kernel(s_ref, w_ref, b_ref, o_ref):
    s = _silu(s_ref[...])
    o_ref[...] = _dot(s.astype(BF16), w_ref[...].astype(BF16)) + b_ref[...]


def _mods(c, c_ctx, w_mod, b_mod):
    depth = w_mod.shape[0]
    s = jnp.zeros((8, D_MODEL), F32).at[0].set(c_ctx).at[1].set(c[0])
    tn = 1024
    out = pl.pallas_call(
        _mod_kernel,
        grid=(depth, N_MOD * D_MODEL // tn),
        in_specs=[pl.BlockSpec((8, D_MODEL), lambda l, j: (0, 0)),
                  pl.BlockSpec((None, D_MODEL, tn), lambda l, j: (l, 0, j)),
                  pl.BlockSpec((None, 1, tn), lambda l, j: (l, 0, j))],
        out_specs=pl.BlockSpec((None, 8, tn), lambda l, j: (l, 0, j)),
        out_shape=jax.ShapeDtypeStruct((depth, 8, N_MOD * D_MODEL), F32),
        compiler_params=_params(("parallel", "parallel")),
        name="mods",
    )(s, w_mod, b_mod.reshape(depth, 1, N_MOD * D_MODEL))
    return out[:, :2].reshape(depth, 2, N_MOD, D_MODEL)


def _mod_spec(layer):
    return pl.BlockSpec((None, None, N_MOD, D_MODEL), lambda i, *_: (layer, jnp.minimum(i, 1), 0, 0))


def _ffn_kernel(x_ref, mod_ref, g_ref, wa_ref, wb_ref, wo_ref, o_ref, h_ref, acc_ref, *, sub, nj):
    j = pl.program_id(1)

    def step(first, last):
        if first:
            h = _norm_mod(x_ref[...], g_ref[...], mod_ref[3 * sub:3 * sub + 1, :],
                          mod_ref[3 * sub + 1:3 * sub + 2, :]).astype(BF16)
            h_ref[...] = h
        else:
            h = h_ref[...]
        t = _dot((_silu(_dot(h, wa_ref[...])) * _dot(h, wb_ref[...])).astype(BF16), wo_ref[...])
        acc = t if first else acc_ref[...] + t
        if last:
            o_ref[...] = x_ref[...] + (0.5 * mod_ref[3 * sub + 2:3 * sub + 3, :]) * acc
        else:
            acc_ref[...] = acc

    pl.when(j == 0)(lambda: step(True, False))
    pl.when(jnp.logical_and(j > 0, j < nj - 1))(lambda: step(False, False))
    pl.when(j == nj - 1)(lambda: step(False, True))


def _ffn(x, mods, g, w_in, w_out, layer, sub, out_rows=None, tf=FFN_TF):
    n = x.shape[0]
    nj = D_FF // tf
    half = sub // 2
    skip = 0 if out_rows is None else (n - out_rows) // TM
    return pl.pallas_call(
        functools.partial(_ffn_kernel, sub=sub, nj=nj),
        grid=(n // TM, nj),
        in_specs=[pl.BlockSpec((TM, D_MODEL), lambda i, j: (i, 0)),
                  _mod_spec(layer),
                  pl.BlockSpec((1, D_MODEL), lambda i, j: (0, 0)),
                  pl.BlockSpec((None, None, D_MODEL, tf), lambda i, j: (layer, half, 0, j)),
                  pl.BlockSpec((None, None, D_MODEL, tf), lambda i, j: (layer, half, 0, j + nj)),
                  pl.BlockSpec((None, None, tf, D_MODEL), lambda i, j: (layer, half, j, 0))],
        out_specs=pl.BlockSpec((TM, D_MODEL), lambda i, j: (jnp.maximum(i - skip, 0), 0)),
        out_shape=jax.ShapeDtypeStruct((n - skip * TM, D_MODEL), F32),
        scratch_shapes=[pltpu.VMEM((TM, D_MODEL), BF16), pltpu.VMEM((TM, D_MODEL), F32)],
        compiler_params=_params(("arbitrary", "arbitrary")),
        name="ffn",
    )(x, mods, g, w_in, w_in, w_out)


PROJ_WIDTHS = (SSD_INNER, SSD_XBC, DT_PAD, S5_WIDTH, ATT_WIDTH + 2 * ATT_KV_WIDTH, ATT_WIDTH + 2 * ATT_KV_WIDTH)
W_IN_BOUNDS = tuple(zip(
    np.cumsum((0, SSD_INNER, SSD_XBC, 2 * SSD_HEADS, S5_WIDTH, ATT_WIDTH + 2 * ATT_KV_WIDTH, ATT_WIDTH + 2 * ATT_KV_WIDTH)),
    np.cumsum((SSD_INNER, SSD_XBC, 2 * SSD_HEADS, S5_WIDTH, ATT_WIDTH + 2 * ATT_KV_WIDTH, ATT_WIDTH + 2 * ATT_KV_WIDTH,
               N_BRANCH * D_MODEL))))


def _split_w_in_kernel(w_ref, *o_refs):
    w = w_ref[...]
    for o_ref, (lo, hi) in zip(o_refs, W_IN_BOUNDS):
        piece = w[:, int(lo):int(hi)].astype(BF16)
        if o_ref.shape[1] > piece.shape[1]:
            piece = jnp.concatenate([piece, jnp.zeros((piece.shape[0], o_ref.shape[1] - piece.shape[1]), BF16)], axis=1)
        o_ref[...] = piece


def _split_w_in(w_in):
    depth, d, d_in = w_in.shape
    tr = 128
    widths = PROJ_WIDTHS + (N_BRANCH * D_MODEL,)
    return pl.pallas_call(
        _split_w_in_kernel,
        grid=(depth, d // tr),
        in_specs=[pl.BlockSpec((None, tr, d_in), lambda l, r: (l, r, 0))],
        out_specs=[pl.BlockSpec((None, tr, w), lambda l, r: (l, r, 0)) for w in widths],
        out_shape=[jax.ShapeDtypeStruct((depth, d, w), BF16) for w in widths],
        compiler_params=_params(("parallel", "parallel")),
        name="split_w_in",
    )(w_in)


def _proj_kernel(x_ref, mod_ref, g_ref, *refs):
    nw = len(PROJ_WIDTHS)
    w_refs, h_ref, o_refs = refs[:nw], refs[nw], refs[nw + 1:]
    h = _norm_mod(x_ref[...], g_ref[...], mod_ref[3:4, :], mod_ref[4:5, :]).astype(BF16)
    h_ref[...] = h
    for w_ref, o_ref in zip(w_refs, o_refs):
        o_ref[...] = _dot(h, w_ref[...])


def _proj(x, mods, g, ws, layer):
    n = x.shape[0]
    tm = PROJ_TM
    row = lambda w: pl.BlockSpec((tm, w), lambda i: (i, 0))
    return pl.pallas_call(
        _proj_kernel,
        grid=(n // tm,),
        in_specs=[row(D_MODEL),
                  pl.BlockSpec((None, None, N_MOD, D_MODEL), lambda i: (layer, jnp.minimum(i // (TM // tm), 1), 0, 0)),
                  pl.BlockSpec((1, D_MODEL), lambda i: (0, 0))]
                 + [pl.BlockSpec((None, D_MODEL, w), lambda i: (layer, 0, 0), pipeline_mode=pl.Buffered(1))
                    for w in PROJ_WIDTHS],
        out_specs=[row(D_MODEL)] + [row(w) for w in PROJ_WIDTHS],
        out_shape=[jax.ShapeDtypeStruct((n, D_MODEL), BF16)]
                  + [jax.ShapeDtypeStruct((n, w), F32) for w in PROJ_WIDTHS],
        compiler_params=_params(("parallel",)),
        name="proj",
    )(x, mods, g, *ws)


QK_W = ATT_WIDTH + ATT_KV_WIDTH


def _attn_prep_kernel(qkv_ref, g_ref, cos_ref, sin_ref, bd_ref, q_ref, kt_ref, v_ref):
    x = qkv_ref[...]
    qk = x[:, :QK_W]
    sq = qk * qk
    hi = sq.astype(BF16)
    lo = (sq - hi.astype(F32)).astype(BF16)
    bd = bd_ref[...]
    ms = jnp.concatenate([_dot(hi[:, t * LANES:(t + 1) * LANES], bd) + _dot(lo[:, t * LANES:(t + 1) * LANES], bd)
                          for t in range(QK_W // LANES)], axis=1)
    y = qk * lax.rsqrt(ms + EPS) * g_ref[...]
    cos = cos_ref[...]
    sin = sin_ref[...]
    lane = lax.broadcasted_iota(jnp.int32, (x.shape[0], LANES), 1)
    first_half = (lane % HEAD_DIM) < (HEAD_DIM // 2)
    tiles = []
    for t in range(QK_W // LANES):
        yt = y[:, t * LANES:(t + 1) * LANES]
        rot = jnp.where(first_half, pltpu.roll(yt, LANES - HEAD_DIM // 2, 1), pltpu.roll(yt, HEAD_DIM // 2, 1))
        tiles.append(yt * cos + rot * sin)
    for h in range(ATT_Q_HEADS):
        t = tiles[h // 2]
        q_ref[h] = (t[:, (h % 2) * HEAD_DIM:(h % 2 + 1) * HEAD_DIM] * ATT_SCALE).astype(BF16)
    kt = tiles[ATT_WIDTH // LANES].T
    kt_ref[0] = kt[:HEAD_DIM].astype(BF16)
    kt_ref[1] = kt[HEAD_DIM:].astype(BF16)
    v = x[:, QK_W:QK_W + LANES]
    one_col = jnp.where(lane == HEAD_DIM, 1.0, 0.0)
    v_ref[0] = jnp.where(lane < HEAD_DIM, v, one_col).astype(BF16)
    v_ref[1] = jnp.where(lane < HEAD_DIM, pltpu.roll(v, HEAD_DIM, 1), one_col).astype(BF16)


def _attn_prep(qkv, g640, cos_t, sin_t, bd):
    s = cos_t.shape[0]
    tm = PREP_TM
    skip = TM // tm - CTX_LEN // tm
    src = lambda i: jnp.where(i < CTX_LEN // tm, i, i + skip)
    return pl.pallas_call(
        _attn_prep_kernel,
        grid=(s // tm,),
        in_specs=[pl.BlockSpec((tm, ATT_WIDTH + 2 * ATT_KV_WIDTH), lambda i: (src(i), 0)),
                  pl.BlockSpec((1, QK_W), lambda i: (0, 0)),
                  pl.BlockSpec((tm, LANES), lambda i: (i, 0)),
                  pl.BlockSpec((tm, LANES), lambda i: (i, 0)),
                  pl.BlockSpec((LANES, LANES), lambda i: (0, 0))],
        out_specs=[pl.BlockSpec((ATT_Q_HEADS, tm, HEAD_DIM), lambda i: (0, i, 0)),
                   pl.BlockSpec((ATT_KV_HEADS, HEAD_DIM, tm), lambda i: (0, 0, i)),
                   pl.BlockSpec((ATT_KV_HEADS, tm, LANES), lambda i: (0, i, 0))],
        out_shape=[jax.ShapeDtypeStruct((ATT_Q_HEADS, s, HEAD_DIM), BF16),
                   jax.ShapeDtypeStruct((ATT_KV_HEADS, HEAD_DIM, s), BF16),
                   jax.ShapeDtypeStruct((ATT_KV_HEADS, s, LANES), BF16)],
        compiler_params=_params(("parallel",)),
        name="attn_prep",
    )(qkv, g640, cos_t, sin_t, bd)


def _heads_out(acc, rows):
    outs = []
    for h in range(GQA_GROUP):
        a = acc[h * rows:(h + 1) * rows]
        outs.append(a[:, :HEAD_DIM] / a[:, HEAD_DIM:HEAD_DIM + 1])
    return jnp.concatenate(outs, axis=1)


def _glb_kernel(q_ref, kt_ref, v_ref, o_ref, *, nk):
    tq = q_ref.shape[1]
    m_rows = GQA_GROUP * tq
    q = q_ref[...].reshape(m_rows, HEAD_DIM)

    def body(c, carry):
        m, acc = carry
        s = _dot(q, kt_ref[c])
        m_new = jnp.maximum(m, jnp.max(s, axis=1, keepdims=True))
        p = jnp.exp(s - m_new)
        acc = jnp.exp(m - m_new) * acc + _dot(p.astype(BF16), v_ref[c])
        return m_new, acc

    m0 = jnp.full((m_rows, 1), NEG_INF, F32)
    acc0 = jnp.zeros((m_rows, LANES), F32)
    _, acc = lax.fori_loop(0, nk, body, (m0, acc0), unroll=True)
    o_ref[...] = _heads_out(acc, tq).astype(BF16)


def _glb_attn(q, kt, v, n_rows):
    s = q.shape[1]
    nk, tk = kt.shape[1], kt.shape[3]
    tq = GLB_TQ
    nq = (s - CTX_LEN) // tq
    half = GQA_GROUP * HEAD_DIM
    return pl.pallas_call(
        functools.partial(_glb_kernel, nk=nk),
        grid=(ATT_KV_HEADS, nq),
        in_specs=[pl.BlockSpec((GQA_GROUP, tq, HEAD_DIM), lambda kv, i: (kv, i + CTX_LEN // tq, 0)),
                  pl.BlockSpec((None, nk, HEAD_DIM, tk), lambda kv, i: (kv, 0, 0, 0)),
                  pl.BlockSpec((None, nk, tk, LANES), lambda kv, i: (kv, 0, 0, 0))],
        out_specs=pl.BlockSpec((tq, half), lambda kv, i: (i + TM // tq, kv)),
        out_shape=jax.ShapeDtypeStruct((n_rows, ATT_WIDTH), BF16),
        compiler_params=_params(("arbitrary", "arbitrary")),
        name="glb_attn",
    )(q, kt, v)


def _swa_bias():
    w, tq = SWA_WINDOW, SWA_TQ
    nkw = tq + 2 * w
    qi = np.arange(tq)[:, None]
    kj = np.arange(nkw)[None, :]
    ok = (kj >= qi) & (kj <= qi + 2 * w)
    oks = np.stack([ok, ok & (kj >= w), ok & (kj < nkw - w)])
    return jnp.asarray(np.where(oks, 0.0, NEG_INF).astype(np.float32))


def _swa_kernel(sink_ref, bias_ref, *refs, nt):
    bps = SWA_TQ // SWA_WINDOW
    nkb = SWA_NSUB * bps + 2
    q_refs, refs = refs[:SWA_NSUB], refs[SWA_NSUB:]
    kt_refs, ktx_ref = refs[:nkb], refs[nkb]
    v_refs, vx_ref, o_ref = refs[nkb + 1:2 * nkb + 1], refs[2 * nkb + 1], refs[2 * nkb + 2]
    n = pl.program_id(0)
    tq = SWA_TQ
    rows = GQA_GROUP * tq
    for u in range(SWA_NSUB):
        first = jnp.logical_and(n == 0, u == 0)
        last = jnp.logical_and(n == nt - 1, u == SWA_NSUB - 1)
        bias = bias_ref[jnp.where(first, 1, jnp.where(last, 2, 0))]
        bias = jnp.concatenate([bias] * GQA_GROUP, axis=0)
        outs = []
        for kv in range(ATT_KV_HEADS):
            q = q_refs[u][kv * GQA_GROUP:(kv + 1) * GQA_GROUP].reshape(rows, HEAD_DIM)
            kt = jnp.concatenate([r[kv] for r in kt_refs[u * bps:u * bps + bps + 2]], axis=1)
            s = jnp.concatenate([_dot(q, kt) + bias, _dot(q, ktx_ref[kv])], axis=1)
            sink = jnp.concatenate([jnp.full((tq, 1), sink_ref[kv * GQA_GROUP + h], F32) for h in range(GQA_GROUP)],
                                   axis=0)
            m = jnp.maximum(jnp.max(s, axis=1, keepdims=True), sink)
            v = jnp.concatenate([r[kv] for r in v_refs[u * bps:u * bps + bps + 2]] + [vx_ref[kv]], axis=0)
            acc = _dot(jnp.exp(s - m).astype(BF16), v)
            lane = lax.broadcasted_iota(jnp.int32, acc.shape, 1)
            acc = acc + jnp.where(lane == HEAD_DIM, jnp.exp(sink - m), 0.0)
            outs.append(_heads_out(acc, tq))
        o_ref[u * tq:(u + 1) * tq, :] = jnp.concatenate(outs, axis=1).astype(BF16)


def _swa_attn(q, kt, v, sink, n_rows):
    s = q.shape[1]
    w = SWA_WINDOW
    ts = SWA_NSUB * SWA_TQ
    nb = (s - CTX_LEN) // w
    nt = (s - CTX_LEN) // ts
    nkb = ts // w + 2
    c0 = CTX_LEN // w
    blk = lambda b: (lambda n: jnp.clip(n * (ts // w) - 1 + b, 0, nb - 1) + c0)
    kt_spec = lambda f: pl.BlockSpec((ATT_KV_HEADS, HEAD_DIM, w), lambda n: (0, 0, f(n)))
    v_spec = lambda f: pl.BlockSpec((ATT_KV_HEADS, w, LANES), lambda n: (0, f(n), 0))
    return pl.pallas_call(
        functools.partial(_swa_kernel, nt=nt),
        grid=(nt,),
        in_specs=[pl.BlockSpec(memory_space=pltpu.SMEM),
                  pl.BlockSpec((3, SWA_TQ, SWA_TQ + 2 * w), lambda n: (0, 0, 0))]
                 + [pl.BlockSpec((ATT_Q_HEADS, SWA_TQ, HEAD_DIM),
                                 lambda n, u=u: (0, n * SWA_NSUB + u + CTX_LEN // SWA_TQ, 0)) for u in range(SWA_NSUB)]
                 + [kt_spec(blk(b)) for b in range(nkb)]
                 + [pl.BlockSpec((ATT_KV_HEADS, HEAD_DIM, CTX_LEN), lambda n: (0, 0, 0))]
                 + [v_spec(blk(b)) for b in range(nkb)]
                 + [pl.BlockSpec((ATT_KV_HEADS, CTX_LEN, LANES), lambda n: (0, 0, 0))],
        out_specs=pl.BlockSpec((ts, ATT_WIDTH), lambda n: (n + TM // ts, 0)),
        out_shape=jax.ShapeDtypeStruct((n_rows, ATT_WIDTH), BF16),
        compiler_params=_params(("parallel",)),
        name="swa_attn",
    )(sink, _swa_bias(), *([q] * SWA_NSUB), *([kt] * (nkb + 1)), *([v] * (nkb + 1)))


def _ctx_attn_kernel(sink_ref, q_ref, kt_ref, v_ref, prev_ref, o_ref, *, use_sink):
    del prev_ref
    rows = GQA_GROUP * CTX_LEN
    outs = []
    for kv in range(ATT_KV_HEADS):
        q = q_ref[kv * GQA_GROUP:(kv + 1) * GQA_GROUP].reshape(rows, HEAD_DIM)
        s = _dot(q, kt_ref[kv])
        m = jnp.max(s, axis=1, keepdims=True)
        if use_sink:
            sink = jnp.concatenate([jnp.full((CTX_LEN, 1), sink_ref[kv * GQA_GROUP + h], F32)
                                    for h in range(GQA_GROUP)], axis=0)
            m = jnp.maximum(m, sink)
        acc = _dot(jnp.exp(s - m).astype(BF16), v_ref[kv])
        if use_sink:
            lane = lax.broadcasted_iota(jnp.int32, acc.shape, 1)
            acc = acc + jnp.where(lane == HEAD_DIM, jnp.exp(sink - m), 0.0)
        outs.append(_heads_out(acc, CTX_LEN))
    o_ref[:CTX_LEN] = jnp.concatenate(outs, axis=1).astype(BF16)
    o_ref[CTX_LEN:] = jnp.zeros((TM - CTX_LEN, ATT_WIDTH), BF16)


def _ctx_attn(q, kt, v, sink, y_prev, use_sink):
    return pl.pallas_call(
        functools.partial(_ctx_attn_kernel, use_sink=use_sink),
        grid=(1,),
        in_specs=[pl.BlockSpec(memory_space=pltpu.SMEM),
                  pl.BlockSpec((ATT_Q_HEADS, CTX_LEN, HEAD_DIM), lambda i: (0, 0, 0)),
                  pl.BlockSpec((ATT_KV_HEADS, HEAD_DIM, CTX_LEN), lambda i: (0, 0, 0)),
                  pl.BlockSpec((ATT_KV_HEADS, CTX_LEN, LANES), lambda i: (0, 0, 0)),
                  pl.BlockSpec(memory_space=pl.ANY)],
        out_specs=pl.BlockSpec((TM, ATT_WIDTH), lambda i: (0, 0)),
        out_shape=jax.ShapeDtypeStruct(y_prev.shape, y_prev.dtype),
        input_output_aliases={4: 0},
        compiler_params=_params(("arbitrary",)),
        name="ctx_attn",
    )(sink, q, kt, v, y_prev)


def _ssd_activate(b, nblk, xp, xc, xn, dt, w, bias, dtb):
    t = SSD_T
    has_prev = jnp.logical_and(b != 0, b != TM // t)
    has_next = jnp.logical_and(b != CTX_LEN // t - 1, b != nblk - 1)
    prev = jnp.where(has_prev, xp, 0.0)
    nxt = jnp.where(has_next, xn, 0.0)
    row = lax.broadcasted_iota(jnp.int32, xc.shape, 0)
    halo = prev.shape[0]
    rep = lambda v: jnp.concatenate([v] * (t // halo), axis=0)
    acc = xc * w[2:3, :] + bias
    for k in (1, 2):
        down = jnp.where(row < k, rep(pltpu.roll(prev, k, 0)), pltpu.roll(xc, k, 0))
        up = jnp.where(row >= t - k, rep(pltpu.roll(nxt, halo - k, 0)), pltpu.roll(xc, t - k, 0))
        acc = acc + down * w[2 - k:3 - k, :] + up * w[2 + k:3 + k, :]
    d = dt + dtb
    return _silu(acc), jnp.maximum(d, 0.0) + jnp.log1p(jnp.exp(-jnp.abs(d)))


def _ssd_scan_kernel(tbl_ref, xp_ref, xc_ref, xn_ref, dt_ref, w_ref, b_ref, dtb_ref, arow_ref, tri_ref, e_ref, *rest,
                     direction, final, nblk):
    if final:
        yf_ref, z_ref, dsk_ref, ng_ref, o_ref, st_ref = rest
    else:
        o_ref, st_ref = rest
    step = pl.program_id(0)
    flag = tbl_ref[1, step]
    t = SSD_T
    gw = SSD_INNER // SSD_GROUPS
    hpg = SSD_HEADS // SSD_GROUPS

    @pl.when(flag == 2)
    def _():
        o_ref[...] = jnp.zeros_like(o_ref)

    @pl.when(flag == 1)
    def _():
        st_ref[...] = jnp.zeros_like(st_ref)

    @pl.when(flag != 2)
    def _():
        x, dt = _ssd_activate(tbl_ref[0, step], nblk, xp_ref[...], xc_ref[...], xn_ref[...], dt_ref[...],
                              w_ref[...], b_ref[...], dtb_ref[...])
        xs = x[:, :SSD_INNER]
        bm = x[:, SSD_INNER:SSD_INNER + SSD_BC]
        cm = x[:, SSD_INNER + SSD_BC:]
        adt = dt * arow_ref[...]
        tri = tri_ref[...]
        keep = tri > 0
        cum = _split_rhs_dot(tri, adt, 3)
        tot = cum[t - 1:t, :] if direction == 0 else cum[0:1, :]
        e = e_ref[...]
        dt_e = _split_lhs_dot(dt, e, 2)
        expc_e = _split_lhs_dot(jnp.exp(cum), e, 2)
        decs_e = _split_lhs_dot(jnp.exp(tot - cum), e, 2)
        dch_e = _split_lhs_dot(jnp.broadcast_to(jnp.exp(tot), (8, DT_PAD)), e, 2)[0:1]
        cum_t = cum.T
        xdt = xs * dt_e
        ys = []
        for g in range(SSD_GROUPS):
            bg = bm[:, g * SSD_STATE:(g + 1) * SSD_STATE]
            cgb = cm[:, g * SSD_STATE:(g + 1) * SSD_STATE].astype(BF16)
            cb = _dot_nt(cgb, bg.astype(BF16))
            stg = st_ref[g]
            y_off = _dot(cgb, stg.astype(BF16)) * expc_e[:, g * gw:(g + 1) * gw]
            parts = []
            for j in range(hpg):
                h = g * hpg + j
                r = direction * SSD_HEADS + h
                diff = cum[:, r:r + 1] - cum_t[r:r + 1, :]
                lmat = jnp.exp(jnp.where(keep, diff, NEG_INF))
                wmat = (cb * lmat).astype(BF16)
                parts.append(_dot(wmat, xdt[:, h * SSD_HEAD_DIM:(h + 1) * SSD_HEAD_DIM].astype(BF16)))
            ys.append(jnp.concatenate(parts, axis=1) + y_off)
            xw = (xdt[:, g * gw:(g + 1) * gw] * decs_e[:, g * gw:(g + 1) * gw]).astype(BF16)
            st_ref[g] = stg * dch_e[:, g * gw:(g + 1) * gw] + _dot(bg.T.astype(BF16), xw)
        y = jnp.concatenate(ys, axis=1)
        if final:
            y = yf_ref[...] + y + dsk_ref[...] * xs
            y = y * _silu(z_ref[...])
            ms = jnp.mean(y * y, axis=-1, keepdims=True)
            o_ref[...] = (y * lax.rsqrt(ms + EPS) * ng_ref[...]).astype(BF16)
        else:
            o_ref[...] = y


def _ssd_scan(tbl, xbc, dt, conv, arow, tri, e, direction, extra=None):
    n = xbc.shape[0]
    t = SSD_T
    final = extra is not None
    blk = lambda w: pl.BlockSpec((t, w), lambda s, tb: (tb[0, s], 0))
    const = lambda r, w: pl.BlockSpec((r, w), lambda s, tb: (0, 0))
    halo_p = pl.BlockSpec((8, SSD_XBC), lambda s, tb: (jnp.maximum(tb[0, s] * (t // 8) - 1, 0), 0))
    halo_n = pl.BlockSpec((8, SSD_XBC), lambda s, tb: (jnp.minimum((tb[0, s] + 1) * (t // 8), n // 8 - 1), 0))
    in_specs = [halo_p, blk(SSD_XBC), halo_n, blk(DT_PAD), const(8, SSD_XBC), const(1, SSD_XBC), const(1, DT_PAD),
                const(1, DT_PAD), const(t, t), const(DT_PAD, SSD_INNER)]
    args = [xbc, xbc, xbc, dt, *conv, arow, tri, e]
    if final:
        in_specs += [blk(SSD_INNER), blk(SSD_INNER), const(1, SSD_INNER), const(1, SSD_INNER)]
        args += list(extra)
    return pl.pallas_call(
        functools.partial(_ssd_scan_kernel, direction=direction, final=final, nblk=n // t),
        grid_spec=pltpu.PrefetchScalarGridSpec(
            num_scalar_prefetch=1, grid=(tbl.shape[1],), in_specs=in_specs,
            out_specs=blk(SSD_INNER),
            scratch_shapes=[pltpu.VMEM((SSD_GROUPS, SSD_STATE, SSD_INNER // SSD_GROUPS), F32)]),
        out_shape=jax.ShapeDtypeStruct((n, SSD_INNER), BF16 if final else F32),
        compiler_params=_params(("arbitrary",)),
        name="ssd_scan_bwd" if final else "ssd_scan_fwd",
    )(tbl, *args)


def _ssd_tables(n_rows):
    t = SSD_T
    ctx = list(range(CTX_LEN // t))
    pad = list(range(CTX_LEN // t, TM // t))
    lat = list(range(TM // t, n_rows // t))
    fwd = ctx + lat + pad
    bwd = ctx[::-1] + lat[::-1] + pad
    flags = [1] + [0] * (len(ctx) + len(lat) - 1) + [2] * len(pad)
    return (jnp.asarray(np.array([fwd, flags], np.int32)), jnp.asarray(np.array([bwd, flags], np.int32)))


def _dot_nt_split(a, b):
    a_hi = a.astype(BF16)
    b_hi = b.astype(BF16)
    a_lo = (a - a_hi.astype(F32)).astype(BF16)
    b_lo = (b - b_hi.astype(F32)).astype(BF16)
    return _dot_nt(a_hi, b_hi) + _dot_nt(a_hi, b_lo) + _dot_nt(a_lo, b_hi)


def _s5_setup_kernel(pa_ref, pb_ref, pc_ref, tile_ref, m_ref, bs_ref, csa_ref, csb_ref, lre_ref, lim_ref):
    t_len = S5_T
    hh = S5_GROUP
    a_re = pa_ref[0:1, :]
    a_im = pa_ref[1:2, :]
    dt = jnp.exp(pa_ref[2:3, :])
    mag = jnp.exp(a_re * dt)
    lr = mag * jnp.cos(a_im * dt)
    li = mag * jnp.sin(a_im * dt)
    den = a_re * a_re + a_im * a_im
    nr = lr - 1.0
    f_re = (nr * a_re + li * a_im) / den
    f_im = (li * a_re - nr * a_im) / den
    b_re, b_im = pb_ref[0:hh], pb_ref[hh:2 * hh]
    c_re, c_im = pc_ref[0:hh], pc_ref[hh:2 * hh]
    bb_re = f_re * b_re - f_im * b_im
    bb_im = f_re * b_im + f_im * b_re
    pw = [(jnp.ones_like(lr), jnp.zeros_like(lr))]
    for _ in range(t_len):
        pr, pi = pw[-1]
        pw.append((pr * lr - pi * li, pr * li + pi * lr))
    fwd = lax.broadcasted_iota(jnp.int32, lr.shape, 1) < S5_STATE
    pick = lambda kf, kb: (jnp.where(fwd, pw[kf][0], pw[kb][0]), jnp.where(fwd, pw[kf][1], pw[kb][1]))
    bs_re, bs_im, cs_re, cs_im = [], [], [], []
    for t in range(t_len):
        er, ei = pick(t_len - 1 - t, t)
        bs_re.append(er * bb_re - ei * bb_im)
        bs_im.append(er * bb_im + ei * bb_re)
        fr, fi = pick(t + 1, t_len - t)
        cs_re.append(c_re * fr - c_im * fi)
        cs_im.append(c_re * fi + c_im * fr)
    bs_full = jnp.concatenate([jnp.concatenate(bs_re, axis=0), jnp.concatenate(bs_im, axis=0)], axis=1)
    cs_full = jnp.concatenate([jnp.concatenate(cs_re, axis=0), -jnp.concatenate(cs_im, axis=0)], axis=1)
    wide = bs_full.shape
    fwd_w = (lax.broadcasted_iota(jnp.int32, wide, 1) % (2 * S5_STATE)) < S5_STATE
    bs_ref[...] = bs_full.astype(BF16)
    csa_ref[...] = jnp.where(fwd_w, cs_full, 0.0).astype(BF16)
    csb_ref[...] = jnp.where(fwd_w, 0.0, cs_full).astype(BF16)
    lre_ref[...] = pw[t_len][0]
    lim_ref[...] = pw[t_len][1]
    cc = jnp.concatenate([c_re, -c_im], axis=1)
    cc = jnp.concatenate([cc, jnp.zeros((LANES - hh, wide[1]), F32)], axis=0)
    fwd_c = (lax.broadcasted_iota(jnp.int32, cc.shape, 1) % (2 * S5_STATE)) < S5_STATE
    kr_f = _dot_nt_split(bs_full, jnp.where(fwd_c, cc, 0.0))
    k_b = _dot_nt_split(bs_full, jnp.where(fwd_c, 0.0, cc))
    tile = tile_ref[...]
    kr_w = _split_lhs_dot(kr_f, tile, 3)
    kb_w = _split_lhs_dot(k_b, tile, 3)
    lane_blk = lax.broadcasted_iota(jnp.int32, wide, 1) // hh
    m = jnp.zeros(wide, F32)
    for t in range(t_len):
        up, dn = kr_w, kb_w
        if t < t_len - 1:
            r = (t_len - 1 - t) * hh
            up = jnp.concatenate([kr_w[r:], jnp.zeros((r, wide[1]), F32)], axis=0)
        if t > 0:
            r = t * hh
            dn = jnp.concatenate([jnp.zeros((r, wide[1]), F32), kb_w[:wide[0] - r]], axis=0)
        m = m + jnp.where(lane_blk == t, up + dn, 0.0)
    m_ref[...] = m.astype(BF16)


def _s5_setup(a_re, a_im, log_dt, b_re, b_im, c_re, c_im):
    g = a_re.shape[1]
    tw = S5_T * S5_GROUP
    both = lambda v: jnp.concatenate([v[0], v[1]], axis=-1)
    ldt = jnp.broadcast_to(log_dt[:, :, None], (2, g, S5_STATE))
    pa = jnp.stack([both(a_re), both(a_im), both(ldt)], axis=1).astype(F32)
    pa = jnp.pad(pa, ((0, 0), (0, 5), (0, 0)))
    dup = lambda v: jnp.concatenate([v, v], axis=-1)
    pb = jnp.concatenate([dup(jnp.swapaxes(b_re, 1, 2)), dup(jnp.swapaxes(b_im, 1, 2))], axis=1).astype(F32)
    pc = jnp.concatenate([dup(c_re), dup(c_im)], axis=1).astype(F32)
    tile = jnp.asarray(np.tile(np.eye(LANES, S5_GROUP).astype(np.float32), (1, S5_T)), BF16)
    per_g = lambda r, c: pl.BlockSpec((None, r, c), lambda i: (i, 0, 0))
    mat = jax.ShapeDtypeStruct((g, tw, tw), BF16)
    lam = jax.ShapeDtypeStruct((g, 1, 2 * S5_STATE), F32)
    m, bs, csa, csb, lre, lim = pl.pallas_call(
        _s5_setup_kernel,
        grid=(g,),
        in_specs=[per_g(8, 2 * S5_STATE), per_g(2 * S5_GROUP, 2 * S5_STATE), per_g(2 * S5_GROUP, 2 * S5_STATE),
                  pl.BlockSpec((LANES, tw), lambda i: (0, 0))],
        out_specs=[per_g(tw, tw)] * 4 + [per_g(1, 2 * S5_STATE)] * 2,
        out_shape=[mat] * 4 + [lam] * 2,
        compiler_params=_params(("parallel",)),
        name="s5_setup",
    )(pa, pb, pc, tile)
    return m, bs, csa, csb, lre[:, 0], lim[:, 0]


def _s5_kernel(u_ref, m_ref, bs_ref, csa_ref, csb_ref, lre_ref, lim_ref, y_ref, v_ref, sa_ref, sb_ref, *, nctx, npad, nch):
    gb = S5_GB
    for j in range(gb):
        v = _dot(u_ref[j], bs_ref[j])
        v_ref.at[0][pl.ds(j, nch, stride=gb), :] = v[:, :LANES]
        v_ref.at[1][pl.ds(j, nch, stride=gb), :] = v[:, LANES:]
    lre = lre_ref[0]
    lim = lim_ref[0]
    is_fwd = lax.broadcasted_iota(jnp.int32, (gb, LANES), 1) < S5_STATE

    nlat = nch - nctx - npad

    def body(i, carry):
        sre, sim = carry
        tail = i - nlat
        rf = jnp.where(i < nctx, i, jnp.where(i < nctx + nlat, i + npad, tail)) * gb
        rb = jnp.where(i < nctx, nctx - 1 - i, jnp.where(i < nctx + nlat, nch + nctx - 1 - i, tail)) * gb
        sa_ref[0, pl.ds(rf, gb), :] = sre
        sa_ref[1, pl.ds(rf, gb), :] = sim
        sb_ref[0, pl.ds(rb, gb), :] = sre
        sb_ref[1, pl.ds(rb, gb), :] = sim
        vre = jnp.where(is_fwd, v_ref[0, pl.ds(rf, gb), :], v_ref[0, pl.ds(rb, gb), :])
        vim = jnp.where(is_fwd, v_ref[1, pl.ds(rf, gb), :], v_ref[1, pl.ds(rb, gb), :])
        return lre * sre - lim * sim + vre, lre * sim + lim * sre + vim

    zero = jnp.zeros((gb, LANES), F32)
    lax.fori_loop(0, nch, body, (zero, zero))
    for j in range(gb):
        rows = pl.ds(j, nch, stride=gb)
        sa = jnp.concatenate([sa_ref.at[0][rows, :], sa_ref.at[1][rows, :]], axis=1).astype(BF16)
        sb = jnp.concatenate([sb_ref.at[0][rows, :], sb_ref.at[1][rows, :]], axis=1).astype(BF16)
        y_ref[j] = _dot(u_ref[j], m_ref[j]) + _dot_nt(sa, csa_ref[j]) + _dot_nt(sb, csb_ref[j])


def _s5_scan(ug, m, bs, csa, csb, lam_re, lam_im):
    g, nch, tw = ug.shape
    gb = S5_GB
    sw = 4 * S5_STATE
    blk = lambda a, b: pl.BlockSpec((gb, a, b), lambda i: (i, 0, 0))
    lam_spec = pl.BlockSpec((1, gb, 2 * S5_STATE), lambda i: (i, 0, 0))
    return pl.pallas_call(
        functools.partial(_s5_kernel, nctx=CTX_LEN // S5_T, npad=(TM - CTX_LEN) // S5_T, nch=nch),
        grid=(g // gb,),
        in_specs=[blk(nch, tw), blk(tw, tw), blk(tw, sw), blk(tw, sw), blk(tw, sw), lam_spec, lam_spec],
        out_specs=blk(nch, tw),
        out_shape=jax.ShapeDtypeStruct((g, nch, tw), F32),
        scratch_shapes=[pltpu.VMEM((2, nch * gb, LANES), F32) for _ in range(3)],
        compiler_params=_params(("parallel",)),
        name="s5_scan",
    )(ug, m, bs, csa, csb, lam_re.reshape(g // gb, gb, -1), lam_im.reshape(g // gb, gb, -1))


def _lane_block_move(tiles, src_blk, dst_blk):
    gpt = LANES // S5_GROUP
    blk = lax.broadcasted_iota(jnp.int32, tiles[0].shape, 1) // S5_GROUP
    out = jnp.zeros(tiles[0].shape, tiles[0].dtype)
    for tile, a, b in zip(tiles, src_blk, dst_blk):
        moved = tile if a == b else pltpu.roll(tile, ((b - a) % gpt) * S5_GROUP, 1)
        out = jnp.where(blk == b, moved, out)
    return out


def _s5_gather_kernel(u_ref, o_ref):
    rb = o_ref.shape[1]
    gpt = LANES // S5_GROUP
    steps = [u_ref[pl.ds(t, rb, stride=S5_T), :] for t in range(S5_T)]
    for q in range(gpt):
        halves = [_lane_block_move(steps[k * gpt:(k + 1) * gpt], [q] * gpt, list(range(gpt)))
                  for k in range(S5_T // gpt)]
        o_ref[q] = jnp.concatenate(halves, axis=1).astype(BF16)


def _s5_gather(u):
    n = u.shape[0]
    gpt = LANES // S5_GROUP
    rb = n // S5_T // 4
    tw = S5_T * S5_GROUP
    return pl.pallas_call(
        _s5_gather_kernel,
        grid=(S5_WIDTH // LANES, 4),
        in_specs=[pl.BlockSpec((rb * S5_T, LANES), lambda j, r: (r, j))],
        out_specs=pl.BlockSpec((gpt, rb, tw), lambda j, r: (j, r, 0)),
        out_shape=jax.ShapeDtypeStruct((S5_GROUPS, n // S5_T, tw), BF16),
        compiler_params=_params(("parallel", "parallel")),
        name="s5_gather",
    )(u)


def _s5_scatter_kernel(y_ref, o_ref):
    rb = y_ref.shape[1]
    gpt = LANES // S5_GROUP
    ys = [y_ref[q] for q in range(gpt)]
    for t in range(S5_T):
        tiles = [y[:, (t // gpt) * LANES:(t // gpt + 1) * LANES] for y in ys]
        o_ref[pl.ds(t, rb, stride=S5_T), :] = _lane_block_move(tiles, [t % gpt] * gpt, list(range(gpt)))


def _s5_scatter(yg):
    g, nchp, tw = yg.shape
    gpt = LANES // S5_GROUP
    rb = nchp // 4
    return pl.pallas_call(
        _s5_scatter_kernel,
        grid=(S5_WIDTH // LANES, 4),
        in_specs=[pl.BlockSpec((gpt, rb, tw), lambda j, r: (j, r, 0))],
        out_specs=pl.BlockSpec((rb * S5_T, LANES), lambda j, r: (r, j)),
        out_shape=jax.ShapeDtypeStruct((nchp * S5_T, S5_WIDTH), F32),
        compiler_params=_params(("parallel", "parallel")),
        name="s5_scatter",
    )(yg)


def _s5_glu_kernel(y_ref, u_ref, d_ref, w_ref, b_ref, o_ref):
    y = jax.nn.gelu(y_ref[...] + d_ref[...] * u_ref[...])
    t = _dot(y.astype(BF16), w_ref[...]) + b_ref[...]
    o_ref[...] = (t[:, :S5_WIDTH] * jax.nn.sigmoid(t[:, S5_WIDTH:])).astype(BF16)


def _s5_glu(y, u, d, w, b, layer):
    n = y.shape[0]
    row = lambda wd: pl.BlockSpec((TM, wd), lambda i: (i, 0))
    return pl.pallas_call(
        _s5_glu_kernel,
        grid=(n // TM,),
        in_specs=[row(S5_WIDTH), row(S5_WIDTH), pl.BlockSpec((1, S5_WIDTH), lambda i: (0, 0)),
                  pl.BlockSpec((None, S5_WIDTH, 2 * S5_WIDTH), lambda i: (layer, 0, 0)),
                  pl.BlockSpec((1, 2 * S5_WIDTH), lambda i: (0, 0))],
        out_specs=row(S5_WIDTH),
        out_shape=jax.ShapeDtypeStruct((n, S5_WIDTH), BF16),
        compiler_params=_params(("parallel",)),
        name="s5_glu",
    )(y, u, d, w, b)


BR_WIDTHS = (SSD_INNER, S5_WIDTH, ATT_WIDTH, ATT_WIDTH)


def _merge_kernel(h_ref, *refs):
    y_refs, wg_refs, wb_refs, o_ref = refs[:4], refs[4:8], refs[8:12], refs[12]
    h = h_ref[...]
    acc = None
    for y_ref, wg_ref, wb_ref in zip(y_refs, wg_refs, wb_refs):
        t = jax.nn.sigmoid(_dot(h, wg_ref[...])) * _dot(y_ref[...], wb_ref[...])
        acc = t if acc is None else acc + t
    o_ref[...] = acc.astype(BF16)


def _merge(hb, ys, w_gate, w_brs, layer, tn=512):
    n = hb.shape[0]
    nj = D_MODEL // tn
    row = lambda w: pl.BlockSpec((TM, w), lambda i, j: (i, 0))
    return pl.pallas_call(
        _merge_kernel,
        grid=(n // TM, nj),
        in_specs=[row(D_MODEL)] + [row(w) for w in BR_WIDTHS]
                 + [pl.BlockSpec((None, D_MODEL, tn), lambda i, j, b=b: (layer, 0, b * nj + j)) for b in range(N_BRANCH)]
                 + [pl.BlockSpec((None, w, tn), lambda i, j: (layer, 0, j)) for w in BR_WIDTHS],
        out_specs=pl.BlockSpec((TM, tn), lambda i, j: (i, j)),
        out_shape=jax.ShapeDtypeStruct((n, D_MODEL), BF16),
        compiler_params=_params(("parallel", "arbitrary")),
        name="merge",
    )(hb, *ys, w_gate, w_gate, w_gate, w_gate, *w_brs)


def _out_kernel(x_ref, a_ref, mod_ref, w_ref, o_ref):
    o_ref[...] = x_ref[...] + mod_ref[5:6, :] * _dot(a_ref[...], w_ref[...])


def _out_proj(x, acc, mods, w_out, layer):
    n = x.shape[0]
    row = lambda: pl.BlockSpec((TM, D_MODEL), lambda i: (i, 0))
    return pl.pallas_call(
        _out_kernel,
        grid=(n // TM,),
        in_specs=[row(), row(), _mod_spec(layer),
                  pl.BlockSpec((None, D_MODEL, D_MODEL), lambda i: (layer, 0, 0), pipeline_mode=pl.Buffered(1))],
        out_specs=row(),
        out_shape=jax.ShapeDtypeStruct((n, D_MODEL), F32),
        compiler_params=_params(("parallel",)),
        name="out_proj",
    )(x, acc, mods, w_out)


def _rope_tables(seq):
    rows = seq // GRID_W
    row = jnp.repeat(jnp.arange(rows, dtype=F32), GRID_W)
    col = jnp.tile(jnp.arange(GRID_W, dtype=F32), rows)
    n_freq = HEAD_DIM // 4
    inv = ROPE_THETA ** (-jnp.arange(n_freq, dtype=F32) / n_freq)
    ang = jnp.concatenate([row[:, None] * inv, col[:, None] * inv], axis=-1)
    cos, sin = jnp.cos(ang), jnp.sin(ang)
    reps = LANES // HEAD_DIM
    cos_t = jnp.tile(jnp.concatenate([cos, cos], axis=-1), (1, reps))
    sin_t = jnp.tile(jnp.concatenate([-sin, sin], axis=-1), (1, reps))
    cos_t = jnp.concatenate([jnp.ones((CTX_LEN, LANES), F32), cos_t], axis=0)
    sin_t = jnp.concatenate([jnp.zeros((CTX_LEN, LANES), F32), sin_t], axis=0)
    return cos_t, sin_t


def _ssd_expand(direction):
    e = np.zeros((DT_PAD, SSD_INNER), np.float32)
    for h in range(SSD_HEADS):
        e[direction * SSD_HEADS + h, h * SSD_HEAD_DIM:(h + 1) * SSD_HEAD_DIM] = 1.0
    return jnp.asarray(e, BF16)


def _to_stream(ctx_rows, lat_rows):
    pad = jnp.zeros((TM - CTX_LEN,) + ctx_rows.shape[1:], ctx_rows.dtype)
    return jnp.concatenate([ctx_rows, pad, lat_rows], axis=0)


def kernel(x, c, ctx, c_ctx, w_mod, b_mod, norm_g, ffn_in, ffn_out, w_in, ssd_conv_w, ssd_conv_b, ssd_a_log, ssd_dt_bias, ssd_d, ssd_norm_g, s5_a_re, s5_a_im, s5_log_dt, s5_b_re, s5_b_im, s5_c_re, s5_c_im, s5_d, s5_glu_w, s5_glu_b, swa_qk_g, swa_sink, glb_qk_g, w_br_ssd, w_br_s5, w_br_swa, w_br_glb, w_out):
    assert x.shape[0] == 1 and ctx.shape[1] == CTX_LEN
    depth = w_mod.shape[0]
    seq = x.shape[1]
    n_rows = TM + seq
    s_len = CTX_LEN + seq

    ffn_in_b = ffn_in.astype(BF16)
    ffn_out_b = ffn_out.astype(BF16)
    *w_proj, w_gate = _split_w_in(w_in)
    w_brs = tuple(w.astype(BF16) for w in (w_br_ssd, w_br_s5, w_br_swa, w_br_glb))
    w_out_b = w_out.astype(BF16)
    glu_w_b = s5_glu_w.astype(BF16)

    mods = _mods(c, c_ctx, w_mod, b_mod)
    cos_t, sin_t = _rope_tables(seq)
    bd = jnp.asarray(np.kron(np.eye(LANES // HEAD_DIM), np.ones((HEAD_DIM, HEAD_DIM))) / HEAD_DIM, BF16)
    tbl_f, tbl_b = _ssd_tables(n_rows)
    tril = jnp.asarray(np.tril(np.ones((SSD_T, SSD_T), np.float32)), BF16)
    triu = jnp.asarray(np.triu(np.ones((SSD_T, SSD_T), np.float32)), BF16)
    e_dirs = (_ssd_expand(0), _ssd_expand(1))
    nk = s_len // GLB_TK

    xs = _to_stream(ctx[0], x[0])
    for i in range(depth):
        xs = _ffn(xs, mods, norm_g[i, 0:1], ffn_in_b, ffn_out_b, i, 0)
        hb, z, xbc, dt, u, qkv_swa, qkv_glb = _proj(xs, mods, norm_g[i, 1:2], w_proj, i)

        conv_w = jnp.pad(ssd_conv_w[i], ((0, 8 - SSD_CONV), (0, 0)))
        dt_bias = jnp.pad(ssd_dt_bias[i].reshape(1, -1), ((0, 0), (0, DT_PAD - 2 * SSD_HEADS)))
        conv = (conv_w, ssd_conv_b[i].reshape(1, -1), dt_bias)
        a_neg = -jnp.exp(ssd_a_log[i].astype(F32))
        arow = lambda d: jnp.zeros((1, DT_PAD), F32).at[0, d * SSD_HEADS:(d + 1) * SSD_HEADS].set(a_neg[d])
        y_f = _ssd_scan(tbl_f, xbc, dt, conv, arow(0), tril, e_dirs[0], 0)
        d_exp = jnp.repeat(ssd_d[i].astype(F32), SSD_HEAD_DIM).reshape(1, -1)
        y_ssd = _ssd_scan(tbl_b, xbc, dt, conv, arow(1), triu, e_dirs[1], 1,
                          extra=(y_f, z, d_exp, ssd_norm_g[i].reshape(1, -1)))

        s5m, s5bs, s5csa, s5csb, lam_re, lam_im = _s5_setup(
            s5_a_re[i], s5_a_im[i], s5_log_dt[i], s5_b_re[i], s5_b_im[i], s5_c_re[i], s5_c_im[i])
        yg = _s5_scan(_s5_gather(u), s5m, s5bs, s5csa, s5csb, lam_re, lam_im)
        y_s5 = _s5_glu(_s5_scatter(yg), u, s5_d[i].reshape(1, -1), glu_w_b, s5_glu_b[i].reshape(1, -1), i)

        g_swa = jnp.concatenate([jnp.tile(swa_qk_g[i, 0], ATT_Q_HEADS), jnp.tile(swa_qk_g[i, 1], ATT_KV_HEADS)])[None]
        q, kt, v = _attn_prep(qkv_swa, g_swa, cos_t, sin_t, bd)
        y_swa = _swa_attn(q, kt, v, swa_sink[i], n_rows)
        y_swa = _ctx_attn(q, kt, v, swa_sink[i], y_swa, True)

        g_glb = jnp.concatenate([jnp.tile(glb_qk_g[i, 0], ATT_Q_HEADS), jnp.tile(glb_qk_g[i, 1], ATT_KV_HEADS)])[None]
        q, kt, v = _attn_prep(qkv_glb, g_glb, cos_t, sin_t, bd)
        kt_c = kt.reshape(ATT_KV_HEADS, HEAD_DIM, nk, GLB_TK).transpose(0, 2, 1, 3)
        y_glb = _glb_attn(q, kt_c, v.reshape(ATT_KV_HEADS, nk, GLB_TK, LANES), n_rows)
        y_glb = _ctx_attn(q, kt, v, swa_sink[i], y_glb, False)

        acc = _merge(hb, (y_ssd, y_s5, y_swa, y_glb), w_gate, w_brs, i)
        xs = _out_proj(xs, acc, mods, w_out_b, i)
        xs = _ffn(xs, mods, norm_g[i, 2:3], ffn_in_b, ffn_out_b, i, 2, out_rows=seq if i == depth - 1 else None)
    return xs[None]
```
